```python
import jax, jax.numpy as jnp
from jax import lax
import numpy as np

D_MODEL = 1024
BATCH = 32
SEQ = 2048
DEPTH = 1

N_HEADS = 16
HEAD_DIM = 64
ATTN_WIDTH = N_HEADS * HEAD_DIM
KV_LATENT = 128
IDX_HEADS = 8
IDX_DIM = 64
TOPK_MAX = 256
Q_BLOCK = 128
RNN_WIDTH = 1024
RNN_BLOCKS = 16
RNN_BLOCK_DIM = RNN_WIDTH // RNN_BLOCKS
CONV_WIDTH = 4
LRU_C = 8.0
NORM_EPS = 1e-6
SPLITS = (ATTN_WIDTH, KV_LATENT, IDX_HEADS * IDX_DIM, IDX_DIM, IDX_HEADS, ATTN_WIDTH, RNN_WIDTH, RNN_WIDTH, 2 * D_MODEL)
IN_WIDTH = sum(SPLITS)

kernel_name = 'hybrid_dsa_rglru_gated_block'


def rms_norm(x, g):
    xf = x.astype(jnp.float32)
    y = xf * lax.rsqrt(jnp.mean(xf * xf, axis=-1, keepdims=True) + NORM_EPS)
    return (y * g.astype(jnp.float32)).astype(x.dtype)


def layer_norm(x, g, b):
    xf = x.astype(jnp.float32)
    mu = jnp.mean(xf, axis=-1, keepdims=True)
    xc = xf - mu
    y = xc * lax.rsqrt(jnp.mean(xc * xc, axis=-1, keepdims=True) + NORM_EPS)
    return (y * g.astype(jnp.float32) + b.astype(jnp.float32)).astype(x.dtype)


def alibi_slopes(n):
    return jnp.exp2(-8.0 * jnp.arange(1, n + 1, dtype=jnp.float32) / n)


def split_cols(z):
    offsets = np.cumsum(np.array(SPLITS))[:-1].tolist()
    return jnp.split(z, offsets, axis=-1)


def sparse_mla_attention(q, c_kv, q_idx, k_idx, w_idx, w_uk, w_uv):
    B, S = q.shape[0], q.shape[1]
    topk = min(TOPK_MAX, S // 4)
    tb = min(Q_BLOCK, S)
    nblk = S // tb
    f32 = jnp.float32
    q_lat = jnp.einsum('bshd,hcd->bshc', q, w_uk) * (HEAD_DIM ** -0.5)
    slopes = alibi_slopes(N_HEADS)
    key_pos = jnp.arange(S, dtype=jnp.int32)
    k_idx_f = k_idx.astype(f32)

    def to_blocks(a):
        return a.reshape((B, nblk, tb) + a.shape[2:]).swapaxes(0, 1)

    def block_fn(args):
        ql, qi, wi, t0 = args
        t = t0 + jnp.arange(tb, dtype=jnp.int32)
        s_idx = jax.nn.relu(jnp.einsum('btjd,bsd->btjs', qi.astype(f32), k_idx_f))
        s_idx = jnp.einsum('btjs,btj->bts', s_idx, wi.astype(f32))
        causal = key_pos[None, :] <= t[:, None]
        s_idx = jnp.where(causal[None], s_idx, -jnp.inf)
        _, idx = lax.top_k(s_idx, topk)
        c_sel = jax.vmap(lambda c, i: c[i])(c_kv, idx).astype(f32)
        logits = jnp.einsum('bthc,btkc->bthk', ql.astype(f32), c_sel)
        dist = (t[None, :, None] - idx).astype(f32)
        logits = logits - slopes[None, None, :, None] * dist[:, :, None, :]
        valid = idx <= t[None, :, None]
        logits = jnp.where(valid[:, :, None, :], logits, -jnp.inf)
        p = jax.nn.softmax(logits, axis=-1)
        o = jnp.einsum('bthk,btkc->bthc', p, c_sel)
        return o.astype(ql.dtype)

    t0s = jnp.arange(nblk, dtype=jnp.int32) * tb
    o_lat = lax.map(block_fn, (to_blocks(q_lat), to_blocks(q_idx), to_blocks(w_idx), t0s))
    o_lat = o_lat.swapaxes(0, 1).reshape(B, S, N_HEADS, KV_LATENT)
    return jnp.einsum('bshc,hcd->bshd', o_lat, w_uv)


def causal_depthwise_conv(x, w, b):
    S = x.shape[1]
    xp = jnp.pad(x, ((0, 0), (CONV_WIDTH - 1, 0), (0, 0)))
    out = b
    for k in range(CONV_WIDTH):
        out = out + xp[:, k:k + S] * w[k]
    return out


def rg_lru(x, w_a, b_a, w_x, b_x, lam):
    B, S, C = x.shape
    f32 = jnp.float32
    xb = x.reshape(B, S, RNN_BLOCKS, RNN_BLOCK_DIM)
    r = jax.nn.sigmoid(jnp.einsum('bsgi,gij->bsgj', xb, w_a).reshape(B, S, C) + b_a).astype(f32)
    i = jax.nn.sigmoid(jnp.einsum('bsgi,gij->bsgj', xb, w_x).reshape(B, S, C) + b_x).astype(f32)
    log_a = -LRU_C * r * jax.nn.softplus(-lam.astype(f32))
    a = jnp.exp(log_a)
    b_in = jnp.sqrt(-jnp.expm1(2.0 * log_a)) * (i * x.astype(f32))

    def combine(left, right):
        a1, b1 = left
        a2, b2 = right
        return a1 * a2, a2 * b1 + b2

    _, h = lax.associative_scan(combine, (a, b_in), axis=1)
    return h.astype(x.dtype)


def hybrid_layer(x, norm_g, w_in, b_merge, kv_norm_g, w_uk, w_uv, idx_ln_g, idx_ln_b,
                 w_attn_proj, conv_w, conv_b, w_rg_a, b_rg_a, w_rg_x, b_rg_x, lru_lambda,
                 w_rnn_proj, w_out):
    B, S, _ = x.shape
    xn = rms_norm(x, norm_g)
    z = xn @ w_in
    q, c_kv, q_idx, k_idx, w_idx, g_attn, x_rnn, g_rnn, merge = split_cols(z)
    q = q.reshape(B, S, N_HEADS, HEAD_DIM)
    c_kv = rms_norm(c_kv, kv_norm_g)
    q_idx = q_idx.reshape(B, S, IDX_HEADS, IDX_DIM) * (IDX_DIM ** -0.5)
    k_idx = layer_norm(k_idx, idx_ln_g, idx_ln_b)
    w_idx = w_idx * (IDX_HEADS ** -0.5)
    o = sparse_mla_attention(q, c_kv, q_idx, k_idx, w_idx, w_uk, w_uv).reshape(B, S, ATTN_WIDTH)
    y_attn = (o * jax.nn.silu(g_attn)) @ w_attn_proj
    h = rg_lru(causal_depthwise_conv(x_rnn, conv_w, conv_b), w_rg_a, b_rg_a, w_rg_x, b_rg_x, lru_lambda)
    y_rnn = (h * jax.nn.silu(g_rnn)) @ w_rnn_proj
    gates = jax.nn.sigmoid(merge + b_merge)
    g_a, g_r = gates[..., :D_MODEL], gates[..., D_MODEL:]
    mixed = g_a * y_attn + g_r * y_rnn
    return x + mixed @ w_out


def setup_inputs(seed: int = 0) -> dict:
    key = jax.random.key(seed)
    ks = jax.random.split(key, 24)
    f32 = jnp.float32

    def nrm(k, shape, scale):
        return jax.random.normal(k, shape, f32) * scale

    u = jax.random.uniform(ks[20], (DEPTH, RNN_WIDTH), f32, minval=0.9, maxval=0.999)
    return {
        'x': jax.random.normal(ks[0], (BATCH, SEQ, D_MODEL), f32),
        'norm_gain': 1.0 + nrm(ks[1], (DEPTH, D_MODEL), 0.01),
        'w_in': nrm(ks[2], (DEPTH, D_MODEL, IN_WIDTH), D_MODEL ** -0.5),
        'b_merge': nrm(ks[3], (DEPTH, 2 * D_MODEL), 0.01),
        'kv_norm_gain': 1.0 + nrm(ks[4], (DEPTH, KV_LATENT), 0.01),
        'w_uk': nrm(ks[5], (DEPTH, N_HEADS, KV_LATENT, HEAD_DIM), KV_LATENT ** -0.5),
        'w_uv': nrm(ks[6], (DEPTH, N_HEADS, KV_LATENT, HEAD_DIM), KV_LATENT ** -0.5),
        'idx_ln_gain': 1.0 + nrm(ks[7], (DEPTH, IDX_DIM), 0.01),
        'idx_ln_bias': nrm(ks[8], (DEPTH, IDX_DIM), 0.01),
        'w_attn_proj': nrm(ks[9], (DEPTH, ATTN_WIDTH, D_MODEL), ATTN_WIDTH ** -0.5),
        'conv_w': nrm(ks[10], (DEPTH, CONV_WIDTH, RNN_WIDTH), CONV_WIDTH ** -0.5),
        'conv_b': nrm(ks[11], (DEPTH, RNN_WIDTH), 0.01),
        'w_rg_a': nrm(ks[12], (DEPTH, RNN_BLOCKS, RNN_BLOCK_DIM, RNN_BLOCK_DIM), RNN_BLOCK_DIM ** -0.5),
        'b_rg_a': nrm(ks[13], (DEPTH, RNN_WIDTH), 0.01),
        'w_rg_x': nrm(ks[14], (DEPTH, RNN_BLOCKS, RNN_BLOCK_DIM, RNN_BLOCK_DIM), RNN_BLOCK_DIM ** -0.5),
        'b_rg_x': nrm(ks[15], (DEPTH, RNN_WIDTH), 0.01),
        'lru_lambda': jnp.log(u) - jnp.log1p(-u),
        'w_rnn_proj': nrm(ks[16], (DEPTH, RNN_WIDTH, D_MODEL), RNN_WIDTH ** -0.5),
        'w_out': nrm(ks[17], (DEPTH, D_MODEL, D_MODEL), D_MODEL ** -0.5),
        'final_norm_gain': 1.0 + nrm(ks[18], (D_MODEL,), 0.01),
    }


def reference(x, norm_gain, w_in, b_merge, kv_norm_gain, w_uk, w_uv, idx_ln_gain, idx_ln_bias,
              w_attn_proj, conv_w, conv_b, w_rg_a, b_rg_a, w_rg_x, b_rg_x, lru_lambda,
              w_rnn_proj, w_out, final_norm_gain):
    for l in range(DEPTH):
        x = hybrid_layer(x, norm_gain[l], w_in[l], b_merge[l], kv_norm_gain[l], w_uk[l], w_uv[l],
                         idx_ln_gain[l], idx_ln_bias[l], w_attn_proj[l], conv_w[l], conv_b[l],
                         w_rg_a[l], b_rg_a[l], w_rg_x[l], b_rg_x[l], lru_lambda[l],
                         w_rnn_proj[l], w_out[l])
    return rms_norm(x, final_norm_gain)
```

```python
import functools

import numpy as np
import jax
import jax.numpy as jnp
from jax import lax
from jax.experimental import pallas as pl
from jax.experimental.pallas import tpu as pltpu

D_MODEL = 1024
N_HEADS = 16
HEAD_DIM = 64
ATTN_WIDTH = N_HEADS * HEAD_DIM
KV_LATENT = 128
IDX_HEADS = 8
IDX_DIM = 64
TOPK_MAX = 256
RNN_WIDTH = 1024
RNN_BLOCKS = 16
RNN_BLOCK_DIM = RNN_WIDTH // RNN_BLOCKS
CONV_WIDTH = 4
LRU_C = 8.0
NORM_EPS = 1e-6

F32 = jnp.float32
BF16 = jnp.bfloat16

LANES = 128
SUBLANES = 8
VMEM_LIMIT_BYTES = 56 * 1024 * 1024

TM = 512
TQ = 128
KC = 128
TK = 256
VROWS = 144
BISECT_ITERS = 32
MASK_NEG = -(2.0 ** 100)
M_INIT = -1e20

C_Q = 0
C_C = C_Q + ATTN_WIDTH
C_QI = C_C + KV_LATENT
C_KW = C_QI + IDX_HEADS * IDX_DIM
C_GA = C_KW + LANES
C_XR = C_GA + ATTN_WIDTH
C_GR = C_XR + RNN_WIDTH
C_MG = C_GR + RNN_WIDTH
C_END = C_MG + 2 * D_MODEL

ALIBI_SLOPES = [float(2.0 ** (-8.0 * (h + 1) / N_HEADS)) for h in range(N_HEADS)]


def _sigmoid(v):
    return jax.nn.sigmoid(v)


def _proj_kernel(x_ref, ng_ref, w_ref, kvg_ref, lng_ref, lnb_ref, cw_ref, cb_ref, wg_ref,
                 ba_ref, bx_ref, lam_ref, wrp_ref, bm_ref,
                 q_ref, c_ref, ct_ref, qit_ref, ki_ref, wt_ref, sg_ref, ga_ref, r_ref,
                 xext_ref, a_ref, b_ref, h_ref, hc_ref):
    j = pl.program_id(1)
    x = x_ref[0]
    xn = x * lax.rsqrt(jnp.mean(x * x, axis=-1, keepdims=True) + NORM_EPS) * ng_ref[...]
    xb = xn.astype(BF16)

    def proj(lo, hi):
        return jnp.dot(xb, w_ref[:, lo:hi], preferred_element_type=F32)

    q_ref[0] = proj(C_Q, C_C).astype(BF16)

    zc = proj(C_C, C_QI)
    cn = zc * lax.rsqrt(jnp.mean(zc * zc, axis=-1, keepdims=True) + NORM_EPS) * kvg_ref[...]
    c_ref[0] = cn.astype(BF16)
    cnt = cn.T
    for k in range(TM // TK):
        ct_ref[0, k] = cnt[:, k * TK:(k + 1) * TK].astype(BF16)

    zqt = (proj(C_QI, C_KW) * (IDX_DIM ** -0.5)).T
    for qb in range(TM // TQ):
        for jh in range(IDX_HEADS):
            qit_ref[0, qb, :, jh * TQ:(jh + 1) * TQ] = (
                zqt[jh * IDX_DIM:(jh + 1) * IDX_DIM, qb * TQ:(qb + 1) * TQ].astype(BF16))

    zk = proj(C_KW, C_GA)
    lane = lax.broadcasted_iota(jnp.int32, zk.shape, 1)
    is_k = lane < IDX_DIM
    mu = jnp.sum(jnp.where(is_k, zk, 0.0), axis=-1, keepdims=True) * (1.0 / IDX_DIM)
    kc = jnp.where(is_k, zk - mu, 0.0)
    var = jnp.sum(kc * kc, axis=-1, keepdims=True) * (1.0 / IDX_DIM)
    kin = kc * lax.rsqrt(var + NORM_EPS) * lng_ref[...] + lnb_ref[...]
    ki_ref[0] = kin[:, :IDX_DIM].astype(BF16)
    wt_ref[0] = zk.T[IDX_DIM:IDX_DIM + IDX_HEADS, :] * (IDX_HEADS ** -0.5)

    zg = proj(C_GA, C_XR)
    sg_ref[0] = (zg * _sigmoid(zg)).astype(BF16)

    zx = proj(C_XR, C_GR)

    @pl.when(j == 0)
    def _():
        xext_ref[0:SUBLANES, :] = jnp.zeros((SUBLANES, RNN_WIDTH), F32)
        hc_ref[...] = jnp.zeros((SUBLANES, RNN_WIDTH), F32)

    @pl.when(j > 0)
    def _():
        xext_ref[0:SUBLANES, :] = xext_ref[TM:TM + SUBLANES, :]

    xext_ref[SUBLANES:SUBLANES + TM, :] = zx
    y = cb_ref[...] + cw_ref[CONV_WIDTH - 1:CONV_WIDTH, :] * zx
    for k in range(CONV_WIDTH - 1):
        off = SUBLANES - (CONV_WIDTH - 1) + k
        y = y + cw_ref[k:k + 1, :] * xext_ref[off:off + TM, :]

    yb = y.astype(BF16)
    ra, ix = [], []
    for p in range(RNN_WIDTH // LANES):
        g2 = jnp.dot(yb[:, p * LANES:(p + 1) * LANES], wg_ref[p], preferred_element_type=F32)
        ra.append(g2[:, :LANES])
        ix.append(g2[:, LANES:])
    rg = _sigmoid(jnp.concatenate(ra, axis=1) + ba_ref[...])
    ig = _sigmoid(jnp.concatenate(ix, axis=1) + bx_ref[...])
    nl = -lam_ref[...]
    softplus = jnp.maximum(nl, 0.0) + jnp.log1p(jnp.exp(-jnp.abs(nl)))
    log_a = (-LRU_C) * rg * softplus
    a_ref[...] = jnp.exp(log_a)
    th = jnp.tanh(log_a)
    b_ref[...] = jnp.sqrt(-2.0 * th / (1.0 - th)) * (ig * y)

    row = lax.broadcasted_iota(jnp.int32, (SUBLANES, RNN_WIDTH), 0)

    def scan_chunk(ci, hprev):
        r0 = pl.multiple_of(ci * SUBLANES, SUBLANES)
        av = a_ref[pl.ds(r0, SUBLANES), :]
        bv = b_ref[pl.ds(r0, SUBLANES), :]
        for d in (1, 2, 4):
            a_sh = jnp.where(row >= d, pltpu.roll(av, d, 0), 1.0)
            b_sh = jnp.where(row >= d, pltpu.roll(bv, d, 0), 0.0)
            bv = av * b_sh + bv
            av = av * a_sh
        hv = av * hprev + bv
        h_ref[pl.ds(r0, SUBLANES), :] = hv
        return jnp.broadcast_to(hv[SUBLANES - 1:SUBLANES, :], (SUBLANES, RNN_WIDTH))

    hc_ref[...] = lax.fori_loop(0, TM // SUBLANES, scan_chunk, hc_ref[...])

    zr = proj(C_GR, C_MG)
    u = (h_ref[...] * (zr * _sigmoid(zr))).astype(BF16)
    y_rnn = jnp.dot(u, wrp_ref[...], preferred_element_type=F32)

    gates = _sigmoid(proj(C_MG, C_END) + bm_ref[...])
    ga_ref[0] = gates[:, :D_MODEL].astype(BF16)
    r_ref[0] = (gates[:, D_MODEL:] * y_rnn).astype(BF16)


def _attn_kernel(topk, q_ref, c_ref, ct_ref, qit_ref, ki_ref, wt_ref, sg_ref, ga_ref, r_ref,
                 x_ref, wuk_ref, wuv_ref, wap_ref, wout_ref, fg_ref,
                 out_ref,
                 lhst_ref, kext_ref, vt_ref, pos_ref, sc_ref, s_ref, ot_ref):
    i = pl.program_id(1)
    seq = kext_ref.shape[0]
    n_kc = i + 1
    n_kt = (i * TQ + TQ + TK - 1) // TK

    @pl.when(i == 0)
    def _():
        kext_ref[:, 0:KV_LATENT] = c_ref[0]
        for k in range(seq // TK):
            vt_ref[k, 0:KV_LATENT, :] = ct_ref[0, k]
            rr = lax.broadcasted_iota(jnp.int32, (VROWS - KV_LATENT, TK), 0)
            vt_ref[k, KV_LATENT:VROWS, :] = jnp.where(rr == 0, 1.0, 0.0).astype(BF16)
        er = lax.broadcasted_iota(jnp.int32, (TQ, TQ), 0)
        ec = lax.broadcasted_iota(jnp.int32, (TQ, TQ), 1)
        eye = jnp.where(er == ec, 1.0, 0.0).astype(BF16)
        for h in range(N_HEADS):
            lhst_ref[KV_LATENT:KV_LATENT + TQ, h * TQ:(h + 1) * TQ] = eye
        pos_ref[...] = lax.broadcasted_iota(jnp.int32, (seq, LANES), 0).astype(F32)

    for p in range(N_HEADS // 2):
        qp = q_ref[0, :, p * LANES:(p + 1) * LANES]
        ql = lax.dot_general(wuk_ref[p], qp, (((1,), (1,)), ((), ())),
                             preferred_element_type=F32) * (HEAD_DIM ** -0.5)
        lhst_ref[0:KV_LATENT, (2 * p) * TQ:(2 * p + 1) * TQ] = ql[:KV_LATENT].astype(BF16)
        lhst_ref[0:KV_LATENT, (2 * p + 1) * TQ:(2 * p + 2) * TQ] = ql[KV_LATENT:].astype(BF16)

    qit = qit_ref[0, 0]
    wv = wt_ref[0]
    tpos = i * TQ + lax.broadcasted_iota(jnp.int32, (KC, TQ), 1)
    srow = lax.broadcasted_iota(jnp.int32, (KC, TQ), 0)

    def fold8(v, op):
        return op(v.reshape(KC // SUBLANES, SUBLANES, TQ), axis=0)

    def score_chunk(kc, carry):
        lo8, hi8 = carry
        r0 = pl.multiple_of(kc * KC, KC)
        prod = jnp.dot(ki_ref[0, pl.ds(r0, KC), :], qit, preferred_element_type=F32)
        sc = jnp.zeros((KC, TQ), F32)
        for jh in range(IDX_HEADS):
            sc = sc + jnp.maximum(prod[:, jh * TQ:(jh + 1) * TQ], 0.0) * wv[jh:jh + 1, :]
        causal = (srow + r0) <= tpos
        sc_ref[pl.ds(r0, KC), :] = jnp.where(causal, sc, -jnp.inf)
        lo8 = jnp.minimum(lo8, fold8(jnp.where(causal, sc, jnp.inf), jnp.min))
        hi8 = jnp.maximum(hi8, fold8(jnp.where(causal, sc, -jnp.inf), jnp.max))
        return lo8, hi8

    lo8, hi8 = lax.fori_loop(
        0, n_kc, score_chunk,
        (jnp.full((SUBLANES, TQ), jnp.inf, F32), jnp.full((SUBLANES, TQ), -jnp.inf, F32)))
    lo = jnp.min(lo8, axis=0, keepdims=True)
    vmax = jnp.max(hi8, axis=0, keepdims=True)
    hi = vmax + jnp.maximum(jnp.abs(vmax), 1e-30) * (2.0 ** -20)

    def count_ge(thr):
        def body(kc, c8):
            r0 = pl.multiple_of(kc * KC, KC)
            xs = sc_ref[pl.ds(r0, KC), :]
            return c8 + fold8(jnp.where(xs >= thr, 1.0, 0.0), jnp.sum)
        c8 = lax.fori_loop(0, n_kc, body, jnp.zeros((SUBLANES, TQ), F32))
        return jnp.sum(c8, axis=0, keepdims=True)

    def bisect(_, carry):
        lo_, hi_ = carry
        mid = lo_ + 0.5 * (hi_ - lo_)
        ge = count_ge(mid) >= topk
        return jnp.where(ge, mid, lo_), jnp.where(ge, hi_, mid)

    lo, hi = lax.fori_loop(0, BISECT_ITERS, bisect, (lo, hi))
    need = topk - count_ge(hi)

    tri = jnp.where(lax.broadcasted_iota(jnp.int32, (KC, KC), 1)
                    <= lax.broadcasted_iota(jnp.int32, (KC, KC), 0), 1.0, 0.0).astype(BF16)

    def mask_chunk(kc, before):
        r0 = pl.multiple_of(kc * KC, KC)
        xs = sc_ref[pl.ds(r0, KC), :]
        band = jnp.where(xs >= lo, jnp.where(xs < hi, 1.0, 0.0), 0.0)
        incl = jnp.dot(tri, band.astype(BF16), preferred_element_type=F32)
        rank = incl - band + before
        keep = jnp.where(xs >= hi, 1.0, jnp.where(rank < need, band, 0.0))
        kext_ref[pl.ds(r0, KC), KV_LATENT:KV_LATENT + TQ] = (
            jnp.where(keep > 0.0, 0.0, MASK_NEG).astype(BF16))
        return before + incl[KC - 1:KC, :]

    lax.fori_loop(0, n_kc, mask_chunk, jnp.zeros((1, TQ), F32))

    @pl.when(n_kc * KC < n_kt * TK)
    def _():
        r0 = pl.multiple_of(n_kc * KC, KC)
        kext_ref[pl.ds(r0, KC), KV_LATENT:KV_LATENT + TQ] = jnp.full((KC, TQ), MASK_NEG, BF16)

    for p in range(N_HEADS // 2):
        wq = lhst_ref[:, p * 2 * TQ:(p + 1) * 2 * TQ]
        s0 = ALIBI_SLOPES[2 * p]
        s1 = ALIBI_SLOPES[2 * p + 1]

        def qk_chunk(kt, mx, wq=wq, s0=s0, s1=s1):
            r0 = pl.multiple_of(kt * TK, TK)
            st = jnp.dot(kext_ref[pl.ds(r0, TK), :], wq, preferred_element_type=F32)
            posb = pos_ref[pl.ds(r0, TK), :]
            st = jnp.concatenate([st[:, :TQ] + posb * s0, st[:, TQ:] + posb * s1], axis=1)
            s_ref[pl.ds(r0, TK), :] = st
            return jnp.maximum(mx, jnp.max(st.reshape(TK // SUBLANES, SUBLANES, 2 * TQ), axis=0))

        mx = lax.fori_loop(0, n_kt, qk_chunk, jnp.full((SUBLANES, 2 * TQ), M_INIT, F32))
        m = jnp.max(mx, axis=0, keepdims=True)

        def pv_chunk(kt, acc, m=m):
            r0 = pl.multiple_of(kt * TK, TK)
            pt = jnp.exp(s_ref[pl.ds(r0, TK), :] - m).astype(BF16)
            return acc + jnp.dot(vt_ref[kt], pt, preferred_element_type=F32)

        acc = lax.fori_loop(0, n_kt, pv_chunk, jnp.zeros((VROWS, 2 * TQ), F32))
        ot_ref[:, p * 2 * TQ:(p + 1) * 2 * TQ] = acc[:KV_LATENT] / acc[KV_LATENT:KV_LATENT + 1]

    o_parts = []
    for p in range(N_HEADS // 2):
        xt = ot_ref[:, p * 2 * TQ:(p + 1) * 2 * TQ].T
        pair = jnp.concatenate([xt[:TQ], xt[TQ:]], axis=1).astype(BF16)
        o_parts.append(jnp.dot(pair, wuv_ref[p], preferred_element_type=F32))
    o = jnp.concatenate(o_parts, axis=1)
    u = (o * sg_ref[0].astype(F32)).astype(BF16)
    y_attn = jnp.dot(u, wap_ref[...], preferred_element_type=F32)
    mixed = (ga_ref[0].astype(F32) * y_attn + r_ref[0].astype(F32)).astype(BF16)
    res = x_ref[0] + jnp.dot(mixed, wout_ref[...], preferred_element_type=F32)
    out_ref[0] = res * lax.rsqrt(jnp.mean(res * res, axis=-1, keepdims=True) + NORM_EPS) * fg_ref[...]


def _const_spec(shape):
    nd = len(shape)
    return pl.BlockSpec(shape, lambda b, i: (0,) * nd, pipeline_mode=pl.Buffered(1))


def _pack_w_in(w_in):
    offs = np.cumsum([0, ATTN_WIDTH, KV_LATENT, IDX_HEADS * IDX_DIM, IDX_DIM, IDX_HEADS,
                      ATTN_WIDTH, RNN_WIDTH, RNN_WIDTH, 2 * D_MODEL])
    seg = [w_in[:, offs[k]:offs[k + 1]] for k in range(9)]
    pad = jnp.zeros((D_MODEL, LANES - IDX_DIM - IDX_HEADS), w_in.dtype)
    return jnp.concatenate(seg[:5] + [pad] + seg[5:], axis=1).astype(BF16)


def _pair_block_diag(w):
    n2, r, c = w.shape
    w = w.reshape(n2 // 2, 2, r, c)
    z = jnp.zeros((n2 // 2, r, c), w.dtype)
    top = jnp.concatenate([w[:, 0], z], axis=2)
    bot = jnp.concatenate([z, w[:, 1]], axis=2)
    return jnp.concatenate([top, bot], axis=1)


def _layer(x, norm_g, w_in, b_merge, kv_g, w_uk, w_uv, ln_g, ln_b, w_ap, conv_w, conv_b,
           w_a, b_a, w_x, b_x, lam, w_rp, w_out, out_g):
    B, S, _ = x.shape
    assert S % TM == 0 and TM % TK == 0 and TK % TQ == 0 and TQ == KC
    topk = min(TOPK_MAX, S // 4)
    row = lambda v: v.reshape(1, -1).astype(F32)
    lane_pad = lambda v: jnp.pad(v.astype(F32), (0, LANES - v.shape[0])).reshape(1, LANES)

    w_all = _pack_w_in(w_in)
    wg = jnp.concatenate([_pair_block_diag(w_a), _pair_block_diag(w_x)], axis=2).astype(BF16)

    nq = S // TQ
    tile = lambda n: pl.BlockSpec((1, TM, n), lambda b, j: (b, j, 0))
    outs = pl.pallas_call(
        _proj_kernel,
        grid=(B, S // TM),
        in_specs=[
            tile(D_MODEL),
            _const_spec((1, D_MODEL)),
            _const_spec((D_MODEL, C_END)),
            _const_spec((1, KV_LATENT)),
            _const_spec((1, LANES)),
            _const_spec((1, LANES)),
            _const_spec((CONV_WIDTH, RNN_WIDTH)),
            _const_spec((1, RNN_WIDTH)),
            _const_spec((RNN_WIDTH // LANES, LANES, 2 * LANES)),
            _const_spec((1, RNN_WIDTH)),
            _const_spec((1, RNN_WIDTH)),
            _const_spec((1, RNN_WIDTH)),
            _const_spec((RNN_WIDTH, D_MODEL)),
            _const_spec((1, 2 * D_MODEL)),
        ],
        out_specs=[
            tile(ATTN_WIDTH),
            tile(KV_LATENT),
            pl.BlockSpec((1, TM // TK, KV_LATENT, TK), lambda b, j: (b, j, 0, 0)),
            pl.BlockSpec((1, TM // TQ, IDX_DIM, IDX_HEADS * TQ), lambda b, j: (b, j, 0, 0)),
            tile(IDX_DIM),
            pl.BlockSpec((1, IDX_HEADS, TM), lambda b, j: (b, 0, j)),
            tile(ATTN_WIDTH),
            tile(D_MODEL),
            tile(D_MODEL),
        ],
        out_shape=[
            jax.ShapeDtypeStruct((B, S, ATTN_WIDTH), BF16),
            jax.ShapeDtypeStruct((B, S, KV_LATENT), BF16),
            jax.ShapeDtypeStruct((B, S // TK, KV_LATENT, TK), BF16),
            jax.ShapeDtypeStruct((B, nq, IDX_DIM, IDX_HEADS * TQ), BF16),
            jax.ShapeDtypeStruct((B, S, IDX_DIM), BF16),
            jax.ShapeDtypeStruct((B, IDX_HEADS, S), F32),
            jax.ShapeDtypeStruct((B, S, ATTN_WIDTH), BF16),
            jax.ShapeDtypeStruct((B, S, D_MODEL), BF16),
            jax.ShapeDtypeStruct((B, S, D_MODEL), BF16),
        ],
        scratch_shapes=[
            pltpu.VMEM((TM + SUBLANES, RNN_WIDTH), F32),
            pltpu.VMEM((TM, RNN_WIDTH), F32),
            pltpu.VMEM((TM, RNN_WIDTH), F32),
            pltpu.VMEM((TM, RNN_WIDTH), F32),
            pltpu.VMEM((SUBLANES, RNN_WIDTH), F32),
        ],
        compiler_params=pltpu.CompilerParams(
            dimension_semantics=("arbitrary", "arbitrary"),
            vmem_limit_bytes=VMEM_LIMIT_BYTES),
        name="proj_rglru",
    )(x, row(norm_g), w_all, row(kv_g), lane_pad(ln_g), lane_pad(ln_b), conv_w.astype(F32),
      row(conv_b), wg, row(b_a), row(b_x), row(lam), w_rp.astype(BF16), row(b_merge))
    q, c, ct, qit, ki, wt, sg, ga, r = outs

    wuk2 = _pair_block_diag(w_uk).astype(BF16)
    wuv2 = _pair_block_diag(w_uv).astype(BF16)

    qtile = lambda n: pl.BlockSpec((1, TQ, n), lambda b, i: (b, i, 0))
    out = pl.pallas_call(
        functools.partial(_attn_kernel, topk),
        grid=(B, nq),
        in_specs=[
            qtile(ATTN_WIDTH),
            pl.BlockSpec((1, S, KV_LATENT), lambda b, i: (b, 0, 0)),
            pl.BlockSpec((1, S // TK, KV_LATENT, TK), lambda b, i: (b, 0, 0, 0)),
            pl.BlockSpec((1, 1, IDX_DIM, IDX_HEADS * TQ), lambda b, i: (b, i, 0, 0)),
            pl.BlockSpec((1, S, IDX_DIM), lambda b, i: (b, 0, 0)),
            pl.BlockSpec((1, IDX_HEADS, TQ), lambda b, i: (b, 0, i)),
            qtile(ATTN_WIDTH),
            qtile(D_MODEL),
            qtile(D_MODEL),
            qtile(D_MODEL),
            _const_spec((N_HEADS // 2, 2 * KV_LATENT, 2 * HEAD_DIM)),
            _const_spec((N_HEADS // 2, 2 * KV_LATENT, 2 * HEAD_DIM)),
            _const_spec((ATTN_WIDTH, D_MODEL)),
            _const_spec((D_MODEL, D_MODEL)),
            _const_spec((1, D_MODEL)),
        ],
        out_specs=qtile(D_MODEL),
        out_shape=jax.ShapeDtypeStruct((B, S, D_MODEL), F32),
        scratch_shapes=[
            pltpu.VMEM((2 * KV_LATENT, N_HEADS * TQ), BF16),
            pltpu.VMEM((S, 2 * KV_LATENT), BF16),
            pltpu.VMEM((S // TK, VROWS, TK), BF16),
            pltpu.VMEM((S, LANES), F32),
            pltpu.VMEM((S, TQ), F32),
            pltpu.VMEM((S, 2 * TQ), F32),
            pltpu.VMEM((KV_LATENT, N_HEADS * TQ), F32),
        ],
        compiler_params=pltpu.CompilerParams(
            dimension_semantics=("arbitrary", "arbitrary"),
            vmem_limit_bytes=VMEM_LIMIT_BYTES),
        name="sparse_attn_out",
    )(q, c, ct, qit, ki, wt, sg, ga, r, x, wuk2, wuv2, w_ap.astype(BF16), w_out.astype(BF16),
      row(out_g))
    return out


def kernel(x, norm_gain, w_in, b_merge, kv_norm_gain, w_uk, w_uv, idx_ln_gain, idx_ln_bias,
           w_attn_proj, conv_w, conv_b, w_rg_a, b_rg_a, w_rg_x, b_rg_x, lru_lambda,
           w_rnn_proj, w_out, final_norm_gain):
    assert norm_gain.shape[0] == 1, "only the stated depth-1 stack is supported"
    return _layer(x, norm_gain[0], w_in[0], b_merge[0], kv_norm_gain[0], w_uk[0], w_uv[0],
                  idx_ln_gain[0], idx_ln_bias[0], w_attn_proj[0], conv_w[0], conv_b[0],
                  w_rg_a[0], b_rg_a[0], w_rg_x[0], b_rg_x[0], lru_lambda[0],
                  w_rnn_proj[0], w_out[0], final_norm_gain)
```

```python
import functools

import numpy as np
import jax
import jax.numpy as jnp
from jax import lax
from jax.experimental import pallas as pl
from jax.experimental.pallas import tpu as pltpu

D_MODEL = 1024
N_HEADS = 16
HEAD_DIM = 64
ATTN_WIDTH = N_HEADS * HEAD_DIM
KV_LATENT = 128
IDX_HEADS = 8
IDX_DIM = 64
TOPK_MAX = 256
RNN_WIDTH = 1024
RNN_BLOCKS = 16
RNN_BLOCK_DIM = RNN_WIDTH // RNN_BLOCKS
CONV_WIDTH = 4
LRU_C = 8.0
NORM_EPS = 1e-6

F32 = jnp.float32
BF16 = jnp.bfloat16

LANES = 128
SUBLANES = 8
VMEM_LIMIT_BYTES = 56 * 1024 * 1024

TM = 512
TQ = 128
KC = 128
TK = 256
CC = 512
VROWS = 144
BISECT_FIXED = 8
BISECT_PER_CHECK = 2
BISECT_MAX_CHECKS = 256
MASK_NEG = -(2.0 ** 100)
M_INIT = -1e20

C_Q = 0
C_C = C_Q + ATTN_WIDTH
C_QI = C_C + KV_LATENT
C_KW = C_QI + IDX_HEADS * IDX_DIM
C_GA = C_KW + LANES
C_XR = C_GA + ATTN_WIDTH
C_GR = C_XR + RNN_WIDTH
C_MG = C_GR + RNN_WIDTH
C_END = C_MG + 2 * D_MODEL

ALIBI_SLOPES = [float(2.0 ** (-8.0 * (h + 1) / N_HEADS)) for h in range(N_HEADS)]


def _sigmoid(v):
    return jax.nn.sigmoid(v)


def _proj_kernel(x_ref, ng_ref, w_ref, kvg_ref, lng_ref, lnb_ref, cw_ref, cb_ref, wg_ref,
                 ba_ref, bx_ref, lam_ref, wrp_ref, bm_ref,
                 q_ref, c_ref, ct_ref, qit_ref, ki_ref, wt_ref, sg_ref, ga_ref, r_ref,
                 xext_ref, a_ref, b_ref, h_ref, hc_ref):
    j = pl.program_id(1)
    x = x_ref[0]
    xn = x * lax.rsqrt(jnp.mean(x * x, axis=-1, keepdims=True) + NORM_EPS) * ng_ref[...]
    xb = xn.astype(BF16)

    def proj(lo, hi):
        return jnp.dot(xb, w_ref[:, lo:hi], preferred_element_type=F32)

    q_ref[0] = proj(C_Q, C_C).astype(BF16)

    zc = proj(C_C, C_QI)
    cn = zc * lax.rsqrt(jnp.mean(zc * zc, axis=-1, keepdims=True) + NORM_EPS) * kvg_ref[...]
    c_ref[0] = cn.astype(BF16)
    cnt = cn.T
    for k in range(TM // TK):
        ct_ref[0, k] = cnt[:, k * TK:(k + 1) * TK].astype(BF16)

    zqt = (proj(C_QI, C_KW) * (IDX_DIM ** -0.5)).T
    for qb in range(TM // TQ):
        for jh in range(IDX_HEADS):
            qit_ref[0, qb, :, jh * TQ:(jh + 1) * TQ] = (
                zqt[jh * IDX_DIM:(jh + 1) * IDX_DIM, qb * TQ:(qb + 1) * TQ].astype(BF16))

    zk = proj(C_KW, C_GA)
    lane = lax.broadcasted_iota(jnp.int32, zk.shape, 1)
    is_k = lane < IDX_DIM
    mu = jnp.sum(jnp.where(is_k, zk, 0.0), axis=-1, keepdims=True) * (1.0 / IDX_DIM)
    kc = jnp.where(is_k, zk - mu, 0.0)
    var = jnp.sum(kc * kc, axis=-1, keepdims=True) * (1.0 / IDX_DIM)
    kin = kc * lax.rsqrt(var + NORM_EPS) * lng_ref[...] + lnb_ref[...]
    ki_ref[0] = kin[:, :IDX_DIM].astype(BF16)
    wt_ref[0] = zk.T[IDX_DIM:IDX_DIM + IDX_HEADS, :] * (IDX_HEADS ** -0.5)

    zg = proj(C_GA, C_XR)
    sg_ref[0] = (zg * _sigmoid(zg)).astype(BF16)

    zx = proj(C_XR, C_GR)

    @pl.when(j == 0)
    def _():
        xext_ref[0:SUBLANES, :] = jnp.zeros((SUBLANES, RNN_WIDTH), F32)
        hc_ref[...] = jnp.zeros((SUBLANES, RNN_WIDTH), F32)

    @pl.when(j > 0)
    def _():
        xext_ref[0:SUBLANES, :] = xext_ref[TM:TM + SUBLANES, :]

    xext_ref[SUBLANES:SUBLANES + TM, :] = zx
    y = cb_ref[...] + cw_ref[CONV_WIDTH - 1:CONV_WIDTH, :] * zx
    for k in range(CONV_WIDTH - 1):
        off = SUBLANES - (CONV_WIDTH - 1) + k
        y = y + cw_ref[k:k + 1, :] * xext_ref[off:off + TM, :]

    yb = y.astype(BF16)
    ra, ix = [], []
    for p in range(RNN_WIDTH // LANES):
        g2 = jnp.dot(yb[:, p * LANES:(p + 1) * LANES], wg_ref[p], preferred_element_type=F32)
        ra.append(g2[:, :LANES])
        ix.append(g2[:, LANES:])
    rg = _sigmoid(jnp.concatenate(ra, axis=1) + ba_ref[...])
    ig = _sigmoid(jnp.concatenate(ix, axis=1) + bx_ref[...])
    nl = -lam_ref[...]
    softplus = jnp.maximum(nl, 0.0) + jnp.log1p(jnp.exp(-jnp.abs(nl)))
    log_a = (-LRU_C) * rg * softplus
    a_ref[...] = jnp.exp(log_a)
    th = jnp.tanh(log_a)
    b_ref[...] = jnp.sqrt(-2.0 * th / (1.0 - th)) * (ig * y)

    row = lax.broadcasted_iota(jnp.int32, (SUBLANES, RNN_WIDTH), 0)

    def scan_chunk(ci, hprev):
        r0 = pl.multiple_of(ci * SUBLANES, SUBLANES)
        av = a_ref[pl.ds(r0, SUBLANES), :]
        bv = b_ref[pl.ds(r0, SUBLANES), :]
        for d in (1, 2, 4):
            a_sh = jnp.where(row >= d, pltpu.roll(av, d, 0), 1.0)
            b_sh = jnp.where(row >= d, pltpu.roll(bv, d, 0), 0.0)
            bv = av * b_sh + bv
            av = av * a_sh
        hv = av * hprev + bv
        h_ref[pl.ds(r0, SUBLANES), :] = hv
        return jnp.broadcast_to(hv[SUBLANES - 1:SUBLANES, :], (SUBLANES, RNN_WIDTH))

    hc_ref[...] = lax.fori_loop(0, TM // SUBLANES, scan_chunk, hc_ref[...])

    zr = proj(C_GR, C_MG)
    u = (h_ref[...] * (zr * _sigmoid(zr))).astype(BF16)
    y_rnn = jnp.dot(u, wrp_ref[...], preferred_element_type=F32)

    gates = _sigmoid(proj(C_MG, C_END) + bm_ref[...])
    ga_ref[0] = gates[:, :D_MODEL].astype(BF16)
    r_ref[0] = (gates[:, D_MODEL:] * y_rnn).astype(BF16)


def _attn_kernel(topk, q_ref, c_ref, ct_ref, qit_ref, ki_ref, wt_ref, sg_ref, wuk_ref, wuv_ref,
                 u_ref,
                 lhst_ref, kext_ref, vt_ref, pos_ref, sc_ref, s_ref, acc_ref):
    i = pl.program_id(1)
    seq = kext_ref.shape[0]
    n_kt = (i * TQ + TQ + TK - 1) // TK
    n_cc = (i * TQ + TQ + CC - 1) // CC

    @pl.when(i == 0)
    def _():
        kext_ref[:, 0:KV_LATENT] = c_ref[0]
        for k in range(seq // TK):
            vt_ref[k, 0:KV_LATENT, :] = ct_ref[0, k]
            rr = lax.broadcasted_iota(jnp.int32, (VROWS - KV_LATENT, TK), 0)
            vt_ref[k, KV_LATENT:VROWS, :] = jnp.where(rr == 0, 1.0, 0.0).astype(BF16)
        er = lax.broadcasted_iota(jnp.int32, (TQ, TQ), 0)
        ec = lax.broadcasted_iota(jnp.int32, (TQ, TQ), 1)
        eye = jnp.where(er == ec, 1.0, 0.0).astype(BF16)
        for h in range(N_HEADS):
            lhst_ref[KV_LATENT:KV_LATENT + TQ, h * TQ:(h + 1) * TQ] = eye
        pos_ref[...] = lax.broadcasted_iota(jnp.int32, (seq, LANES), 0).astype(F32)

    for p in range(N_HEADS // 2):
        qp = q_ref[0, :, p * LANES:(p + 1) * LANES]
        ql = lax.dot_general(wuk_ref[p], qp, (((1,), (1,)), ((), ())),
                             preferred_element_type=F32) * (HEAD_DIM ** -0.5)
        lhst_ref[0:KV_LATENT, (2 * p) * TQ:(2 * p + 1) * TQ] = ql[:KV_LATENT].astype(BF16)
        lhst_ref[0:KV_LATENT, (2 * p + 1) * TQ:(2 * p + 2) * TQ] = ql[KV_LATENT:].astype(BF16)

    qit = qit_ref[0, 0]
    wv = wt_ref[0]
    tpos = i * TQ + lax.broadcasted_iota(jnp.int32, (TK, TQ), 1)
    srow = lax.broadcasted_iota(jnp.int32, (TK, TQ), 0)

    def fold8(v, op):
        parts = [op(v[k * KC:(k + 1) * KC].reshape(KC // SUBLANES, SUBLANES, TQ), axis=0)
                 for k in range(v.shape[0] // KC)]
        return op(jnp.stack(parts), axis=0)

    def score_chunk(kt, carry):
        lo8, hi8 = carry
        r0 = pl.multiple_of(kt * TK, TK)
        prod = jnp.dot(ki_ref[0, pl.ds(r0, TK), :], qit, preferred_element_type=F32)
        sc = jnp.maximum(prod[:, 0:TQ], 0.0) * wv[0:1, :]
        for jh in range(1, IDX_HEADS):
            sc = sc + jnp.maximum(prod[:, jh * TQ:(jh + 1) * TQ], 0.0) * wv[jh:jh + 1, :]
        causal = (srow + r0) <= tpos
        sc_ref[pl.ds(r0, TK), :] = jnp.where(causal, sc, -jnp.inf)
        lo8 = jnp.minimum(lo8, fold8(jnp.where(causal, sc, jnp.inf), jnp.min))
        hi8 = jnp.maximum(hi8, fold8(jnp.where(causal, sc, -jnp.inf), jnp.max))
        return lo8, hi8

    lo8, hi8 = lax.fori_loop(
        0, n_kt, score_chunk,
        (jnp.full((SUBLANES, TQ), jnp.inf, F32), jnp.full((SUBLANES, TQ), -jnp.inf, F32)))

    @pl.when(n_kt * TK < n_cc * CC)
    def _():
        r0 = pl.multiple_of(n_kt * TK, TK)
        sc_ref[pl.ds(r0, TK), :] = jnp.full((TK, TQ), -jnp.inf, F32)

    lo = jnp.min(lo8, axis=0, keepdims=True)
    vmax = jnp.max(hi8, axis=0, keepdims=True)
    hi = vmax + jnp.maximum(jnp.abs(vmax), 1e-30) * (2.0 ** -20)
    kf = float(topk)

    def count_ge(thr):
        def body(cc, c8):
            r0 = pl.multiple_of(cc * CC, CC)
            xs = sc_ref[pl.ds(r0, CC), :]
            return c8 + fold8(jnp.where(xs >= thr, 1.0, 0.0), jnp.sum)
        c8 = lax.fori_loop(0, n_cc, body, jnp.zeros((SUBLANES, TQ), F32))
        return jnp.sum(c8, axis=0, keepdims=True)

    def bisect(carry):
        lo_, hi_, clo_ = carry
        mid = lo_ + 0.5 * (hi_ - lo_)
        cnt = count_ge(mid)
        ge = cnt >= kf
        return jnp.where(ge, mid, lo_), jnp.where(ge, hi_, mid), jnp.where(ge, cnt, clo_)

    def unresolved(carry):
        lo_, hi_, clo_ = carry
        mid = lo_ + 0.5 * (hi_ - lo_)
        open_ = jnp.where(clo_ > kf, jnp.where(mid > lo_, jnp.where(mid < hi_, 1.0, 0.0), 0.0), 0.0)
        return (jnp.max(open_) > 0.5).astype(jnp.int32)

    n_causal = (tpos[0:1, :] + 1).astype(F32)
    carry = lax.fori_loop(0, BISECT_FIXED, lambda _, cr: bisect(cr), (lo, hi, n_causal))

    def search_more(state):
        it, _, cr = state
        for _ in range(BISECT_PER_CHECK):
            cr = bisect(cr)
        return it + 1, unresolved(cr), cr

    _, _, (lo, hi, c_lo) = lax.while_loop(
        lambda st: jnp.logical_and(st[1] > 0, st[0] < BISECT_MAX_CHECKS),
        search_more, (jnp.int32(0), unresolved(carry), carry))

    has_ties = jnp.max(jnp.where(c_lo > kf, 1.0, 0.0)) > 0.5

    @pl.when(jnp.logical_not(has_ties))
    def _():
        def mask_chunk(kt, _):
            r0 = pl.multiple_of(kt * TK, TK)
            xs = sc_ref[pl.ds(r0, TK), :]
            kext_ref[pl.ds(r0, TK), KV_LATENT:KV_LATENT + TQ] = (
                jnp.where(xs >= lo, 0.0, MASK_NEG).astype(BF16))
            return 0
        lax.fori_loop(0, n_kt, mask_chunk, 0)

    @pl.when(has_ties)
    def _():
        need = kf - count_ge(hi)
        tri = jnp.where(lax.broadcasted_iota(jnp.int32, (KC, KC), 1)
                        <= lax.broadcasted_iota(jnp.int32, (KC, KC), 0), 1.0, 0.0).astype(BF16)

        def mask_chunk(kc, before):
            r0 = pl.multiple_of(kc * KC, KC)
            xs = sc_ref[pl.ds(r0, KC), :]
            band = jnp.where(xs >= lo, jnp.where(xs < hi, 1.0, 0.0), 0.0)
            incl = jnp.dot(tri, band.astype(BF16), preferred_element_type=F32)
            rank = incl - band + before
            keep = jnp.where(xs >= hi, 1.0, jnp.where(rank < need, band, 0.0))
            kext_ref[pl.ds(r0, KC), KV_LATENT:KV_LATENT + TQ] = (
                jnp.where(keep > 0.0, 0.0, MASK_NEG).astype(BF16))
            return before + incl[KC - 1:KC, :]

        lax.fori_loop(0, (TK // KC) * n_kt, mask_chunk, jnp.zeros((1, TQ), F32))

    def qk_chunk(kt, mx):
        r0 = pl.multiple_of(kt * TK, TK)
        st = jnp.dot(kext_ref[pl.ds(r0, TK), :], lhst_ref[...], preferred_element_type=F32)
        posb = pos_ref[pl.ds(r0, TK), :]
        tops = []
        for h in range(N_HEADS):
            sh = st[:, h * TQ:(h + 1) * TQ] + posb * ALIBI_SLOPES[h]
            s_ref[pl.ds(r0, TK), h * TQ:(h + 1) * TQ] = sh
            tops.append(fold8(sh, jnp.max))
        return jnp.maximum(mx, jnp.concatenate(tops, axis=1))

    mx = lax.fori_loop(0, n_kt, qk_chunk, jnp.full((SUBLANES, N_HEADS * TQ), M_INIT, F32))
    m = jnp.max(mx, axis=0, keepdims=True)
    acc_ref[...] = jnp.zeros(acc_ref.shape, F32)

    def pv_chunk(kt, _):
        r0 = pl.multiple_of(kt * TK, TK)
        pt = jnp.exp(s_ref[pl.ds(r0, TK), :] - m).astype(BF16)
        acc_ref[...] += jnp.dot(vt_ref[kt], pt, preferred_element_type=F32)
        return 0

    lax.fori_loop(0, n_kt, pv_chunk, 0)

    o_parts = []
    for p in range(N_HEADS // 2):
        cols = slice(p * 2 * TQ, (p + 1) * 2 * TQ)
        xt = (acc_ref[0:KV_LATENT, cols] / acc_ref[KV_LATENT:KV_LATENT + 1, cols]).T
        pair = jnp.concatenate([xt[:TQ], xt[TQ:]], axis=1).astype(BF16)
        o_parts.append(jnp.dot(pair, wuv_ref[p], preferred_element_type=F32))
    o = jnp.concatenate(o_parts, axis=1)
    u_ref[0] = (o * sg_ref[0].astype(F32)).astype(BF16)


def _out_kernel(u_ref, ga_ref, r_ref, x_ref, wap_ref, wout_ref, fg_ref, out_ref):
    y_attn = jnp.dot(u_ref[0], wap_ref[...], preferred_element_type=F32)
    mixed = (ga_ref[0].astype(F32) * y_attn + r_ref[0].astype(F32)).astype(BF16)
    res = x_ref[0] + jnp.dot(mixed, wout_ref[...], preferred_element_type=F32)
    out_ref[0] = res * lax.rsqrt(jnp.mean(res * res, axis=-1, keepdims=True) + NORM_EPS) * fg_ref[...]


def _const_spec(shape):
    nd = len(shape)
    return pl.BlockSpec(shape, lambda b, i: (0,) * nd, pipeline_mode=pl.Buffered(1))


def _pack_w_in(w_in):
    offs = np.cumsum([0, ATTN_WIDTH, KV_LATENT, IDX_HEADS * IDX_DIM, IDX_DIM, IDX_HEADS,
                      ATTN_WIDTH, RNN_WIDTH, RNN_WIDTH, 2 * D_MODEL])
    seg = [w_in[:, offs[k]:offs[k + 1]] for k in range(9)]
    pad = jnp.zeros((D_MODEL, LANES - IDX_DIM - IDX_HEADS), w_in.dtype)
    return jnp.concatenate(seg[:5] + [pad] + seg[5:], axis=1).astype(BF16)


def _pair_block_diag(w):
    n2, r, c = w.shape
    w = w.reshape(n2 // 2, 2, r, c)
    z = jnp.zeros((n2 // 2, r, c), w.dtype)
    top = jnp.concatenate([w[:, 0], z], axis=2)
    bot = jnp.concatenate([z, w[:, 1]], axis=2)
    return jnp.concatenate([top, bot], axis=1)


def _layer(x, norm_g, w_in, b_merge, kv_g, w_uk, w_uv, ln_g, ln_b, w_ap, conv_w, conv_b,
           w_a, b_a, w_x, b_x, lam, w_rp, w_out, out_g):
    B, S, _ = x.shape
    assert S % TM == 0 and TM % TK == 0 and TK % TQ == 0 and TQ == KC
    assert S % CC == 0 and CC % TK == 0
    topk = min(TOPK_MAX, S // 4)
    row = lambda v: v.reshape(1, -1).astype(F32)
    lane_pad = lambda v: jnp.pad(v.astype(F32), (0, LANES - v.shape[0])).reshape(1, LANES)

    w_all = _pack_w_in(w_in)
    wg = jnp.concatenate([_pair_block_diag(w_a), _pair_block_diag(w_x)], axis=2).astype(BF16)

    nq = S // TQ
    tile = lambda n: pl.BlockSpec((1, TM, n), lambda b, j: (b, j, 0))
    outs = pl.pallas_call(
        _proj_kernel,
        grid=(B, S // TM),
        in_specs=[
            tile(D_MODEL),
            _const_spec((1, D_MODEL)),
            _const_spec((D_MODEL, C_END)),
            _const_spec((1, KV_LATENT)),
            _const_spec((1, LANES)),
            _const_spec((1, LANES)),
            _const_spec((CONV_WIDTH, RNN_WIDTH)),
            _const_spec((1, RNN_WIDTH)),
            _const_spec((RNN_WIDTH // LANES, LANES, 2 * LANES)),
            _const_spec((1, RNN_WIDTH)),
            _const_spec((1, RNN_WIDTH)),
            _const_spec((1, RNN_WIDTH)),
            _const_spec((RNN_WIDTH, D_MODEL)),
            _const_spec((1, 2 * D_MODEL)),
        ],
        out_specs=[
            tile(ATTN_WIDTH),
            tile(KV_LATENT),
            pl.BlockSpec((1, TM // TK, KV_LATENT, TK), lambda b, j: (b, j, 0, 0)),
            pl.BlockSpec((1, TM // TQ, IDX_DIM, IDX_HEADS * TQ), lambda b, j: (b, j, 0, 0)),
            tile(IDX_DIM),
            pl.BlockSpec((1, IDX_HEADS, TM), lambda b, j: (b, 0, j)),
            tile(ATTN_WIDTH),
            tile(D_MODEL),
            tile(D_MODEL),
        ],
        out_shape=[
            jax.ShapeDtypeStruct((B, S, ATTN_WIDTH), BF16),
            jax.ShapeDtypeStruct((B, S, KV_LATENT), BF16),
            jax.ShapeDtypeStruct((B, S // TK, KV_LATENT, TK), BF16),
            jax.ShapeDtypeStruct((B, nq, IDX_DIM, IDX_HEADS * TQ), BF16),
            jax.ShapeDtypeStruct((B, S, IDX_DIM), BF16),
            jax.ShapeDtypeStruct((B, IDX_HEADS, S), F32),
            jax.ShapeDtypeStruct((B, S, ATTN_WIDTH), BF16),
            jax.ShapeDtypeStruct((B, S, D_MODEL), BF16),
            jax.ShapeDtypeStruct((B, S, D_MODEL), BF16),
        ],
        scratch_shapes=[
            pltpu.VMEM((TM + SUBLANES, RNN_WIDTH), F32),
            pltpu.VMEM((TM, RNN_WIDTH), F32),
            pltpu.VMEM((TM, RNN_WIDTH), F32),
            pltpu.VMEM((TM, RNN_WIDTH), F32),
            pltpu.VMEM((SUBLANES, RNN_WIDTH), F32),
        ],
        compiler_params=pltpu.CompilerParams(
            dimension_semantics=("arbitrary", "arbitrary"),
            vmem_limit_bytes=VMEM_LIMIT_BYTES),
        name="proj_rglru",
    )(x, row(norm_g), w_all, row(kv_g), lane_pad(ln_g), lane_pad(ln_b), conv_w.astype(F32),
      row(conv_b), wg, row(b_a), row(b_x), row(lam), w_rp.astype(BF16), row(b_merge))
    q, c, ct, qit, ki, wt, sg, ga, r = outs

    wuk2 = _pair_block_diag(w_uk).astype(BF16)
    wuv2 = _pair_block_diag(w_uv).astype(BF16)

    qtile = lambda n: pl.BlockSpec((1, TQ, n), lambda b, i: (b, i, 0))
    u = pl.pallas_call(
        functools.partial(_attn_kernel, topk),
        grid=(B, nq),
        in_specs=[
            qtile(ATTN_WIDTH),
            pl.BlockSpec((1, S, KV_LATENT), lambda b, i: (b, 0, 0)),
            pl.BlockSpec((1, S // TK, KV_LATENT, TK), lambda b, i: (b, 0, 0, 0)),
            pl.BlockSpec((1, 1, IDX_DIM, IDX_HEADS * TQ), lambda b, i: (b, i, 0, 0)),
            pl.BlockSpec((1, S, IDX_DIM), lambda b, i: (b, 0, 0)),
            pl.BlockSpec((1, IDX_HEADS, TQ), lambda b, i: (b, 0, i)),
            qtile(ATTN_WIDTH),
            _const_spec((N_HEADS // 2, 2 * KV_LATENT, 2 * HEAD_DIM)),
            _const_spec((N_HEADS // 2, 2 * KV_LATENT, 2 * HEAD_DIM)),
        ],
        out_specs=qtile(ATTN_WIDTH),
        out_shape=jax.ShapeDtypeStruct((B, S, ATTN_WIDTH), BF16),
        scratch_shapes=[
            pltpu.VMEM((2 * KV_LATENT, N_HEADS * TQ), BF16),
            pltpu.VMEM((S, 2 * KV_LATENT), BF16),
            pltpu.VMEM((S // TK, VROWS, TK), BF16),
            pltpu.VMEM((S, LANES), F32),
            pltpu.VMEM((S, TQ), F32),
            pltpu.VMEM((S, N_HEADS * TQ), F32),
            pltpu.VMEM((VROWS, N_HEADS * TQ), F32),
        ],
        compiler_params=pltpu.CompilerParams(
            dimension_semantics=("arbitrary", "arbitrary"),
            vmem_limit_bytes=VMEM_LIMIT_BYTES),
        name="sparse_attn",
    )(q, c, ct, qit, ki, wt, sg, wuk2, wuv2)

    return pl.pallas_call(
        _out_kernel,
        grid=(B, S // TM),
        in_specs=[
            tile(ATTN_WIDTH),
            tile(D_MODEL),
            tile(D_MODEL),
            tile(D_MODEL),
            _const_spec((ATTN_WIDTH, D_MODEL)),
            _const_spec((D_MODEL, D_MODEL)),
            _const_spec((1, D_MODEL)),
        ],
        out_specs=tile(D_MODEL),
        out_shape=jax.ShapeDtypeStruct((B, S, D_MODEL), F32),
        compiler_params=pltpu.CompilerParams(
            dimension_semantics=("arbitrary", "arbitrary"),
            vmem_limit_bytes=VMEM_LIMIT_BYTES),
        name="merge_out",
    )(u, ga, r, x, w_ap.astype(BF16), w_out.astype(BF16), row(out_g))


def kernel(x, norm_gain, w_in, b_merge, kv_norm_gain, w_uk, w_uv, idx_ln_gain, idx_ln_bias,
           w_attn_proj, conv_w, conv_b, w_rg_a, b_rg_a, w_rg_x, b_rg_x, lru_lambda,
           w_rnn_proj, w_out, final_norm_gain):
    assert norm_gain.shape[0] == 1, "only the stated depth-1 stack is supported"
    return _layer(x, norm_gain[0], w_in[0], b_merge[0], kv_norm_gain[0], w_uk[0], w_uv[0],
                  idx_ln_gain[0], idx_ln_bias[0], w_attn_proj[0], conv_w[0], conv_b[0],
                  w_rg_a[0], b_rg_a[0], w_rg_x[0], b_rg_x[0], lru_lambda[0],
                  w_rnn_proj[0], w_out[0], final_norm_gain)
```

```python
import functools

import numpy as np
import jax
import jax.numpy as jnp
from jax import lax
from jax.experimental import pallas as pl
from jax.experimental.pallas import tpu as pltpu

D_MODEL = 1024
N_HEADS = 16
HEAD_DIM = 64
ATTN_WIDTH = N_HEADS * HEAD_DIM
KV_LATENT = 128
IDX_HEADS = 8
IDX_DIM = 64
TOPK_MAX = 256
RNN_WIDTH = 1024
RNN_BLOCKS = 16
RNN_BLOCK_DIM = RNN_WIDTH // RNN_BLOCKS
CONV_WIDTH = 4
LRU_C = 8.0
NORM_EPS = 1e-6

F32 = jnp.float32
BF16 = jnp.bfloat16

LANES = 128
SUBLANES = 8
VMEM_LIMIT_BYTES = 56 * 1024 * 1024

TM = 512
TQ = 128
KC = 128
TK = 256
CC = 512
VROWS = 144
BISECT_FIXED = 8
BISECT_PER_CHECK = 2
BISECT_MAX_CHECKS = 14
LOG2E = 1.4426950408889634
F32_MIN_NORMAL = 2.0 ** -126
MASK_NEG = -(2.0 ** 100)
M_INIT = -1e20

C_Q = 0
C_C = C_Q + ATTN_WIDTH
C_QI = C_C + KV_LATENT
C_KW = C_QI + IDX_HEADS * IDX_DIM
C_GA = C_KW + LANES
C_XR = C_GA + ATTN_WIDTH
C_GR = C_XR + RNN_WIDTH
C_MG = C_GR + RNN_WIDTH
C_END = C_MG + 2 * D_MODEL

ALIBI_SLOPES = [float(2.0 ** (-8.0 * (h + 1) / N_HEADS)) for h in range(N_HEADS)]


def _sigmoid(v):
    return jax.nn.sigmoid(v)


def _f32_to_key(v):
    b = lax.bitcast_convert_type(v, jnp.int32)
    return b ^ ((b >> 31) & 0x7FFFFFFF)


def _key_to_f32(k):
    return lax.bitcast_convert_type(k ^ ((k >> 31) & 0x7FFFFFFF), F32)


def _proj_kernel(x_ref, ng_ref, w_ref, wuk_ref, kvg_ref, lng_ref, lnb_ref, cw_ref, cb_ref, wg_ref,
                 ba_ref, bx_ref, lam_ref, wrp_ref, bm_ref,
                 lq_ref, c_ref, ct_ref, qit_ref, ki_ref, wt_ref, sg_ref, ga_ref, r_ref,
                 xext_ref, a_ref, b_ref, h_ref, hc_ref):
    j = pl.program_id(1)
    x = x_ref[0]
    xn = x * lax.rsqrt(jnp.mean(x * x, axis=-1, keepdims=True) + NORM_EPS) * ng_ref[...]
    xb = xn.astype(BF16)

    def proj(lo, hi):
        return jnp.dot(xb, w_ref[:, lo:hi], preferred_element_type=F32)

    qb16 = proj(C_Q, C_C).astype(BF16)
    ql = [jnp.dot(qb16[:, p * LANES:(p + 1) * LANES], wuk_ref[p], preferred_element_type=F32)
          for p in range(N_HEADS // 2)]
    qlt = (jnp.concatenate(ql, axis=1) * (HEAD_DIM ** -0.5 * LOG2E)).T
    for qb in range(TM // TQ):
        for h in range(N_HEADS):
            lq_ref[0, qb, :, h * TQ:(h + 1) * TQ] = (
                qlt[h * KV_LATENT:(h + 1) * KV_LATENT, qb * TQ:(qb + 1) * TQ].astype(BF16))

    zc = proj(C_C, C_QI)
    cn = zc * lax.rsqrt(jnp.mean(zc * zc, axis=-1, keepdims=True) + NORM_EPS) * kvg_ref[...]
    c_ref[0] = cn.astype(BF16)
    cnt = cn.T
    for k in range(TM // TK):
        ct_ref[0, k] = cnt[:, k * TK:(k + 1) * TK].astype(BF16)

    zqt = (proj(C_QI, C_KW) * (IDX_DIM ** -0.5)).T
    for qb in range(TM // TQ):
        for jh in range(IDX_HEADS):
            qit_ref[0, qb, :, jh * TQ:(jh + 1) * TQ] = (
                zqt[jh * IDX_DIM:(jh + 1) * IDX_DIM, qb * TQ:(qb + 1) * TQ].astype(BF16))

    zk = proj(C_KW, C_GA)
    lane = lax.broadcasted_iota(jnp.int32, zk.shape, 1)
    is_k = lane < IDX_DIM
    mu = jnp.sum(jnp.where(is_k, zk, 0.0), axis=-1, keepdims=True) * (1.0 / IDX_DIM)
    kc = jnp.where(is_k, zk - mu, 0.0)
    var = jnp.sum(kc * kc, axis=-1, keepdims=True) * (1.0 / IDX_DIM)
    kin = kc * lax.rsqrt(var + NORM_EPS) * lng_ref[...] + lnb_ref[...]
    ki_ref[0] = kin[:, :IDX_DIM].astype(BF16)
    wt_ref[0] = zk.T[IDX_DIM:IDX_DIM + IDX_HEADS, :] * (IDX_HEADS ** -0.5)

    zg = proj(C_GA, C_XR)
    sg_ref[0] = (zg * _sigmoid(zg)).astype(BF16)

    zx = proj(C_XR, C_GR)

    @pl.when(j == 0)
    def _():
        xext_ref[0:SUBLANES, :] = jnp.zeros((SUBLANES, RNN_WIDTH), F32)
        hc_ref[...] = jnp.zeros((SUBLANES, RNN_WIDTH), F32)

    @pl.when(j > 0)
    def _():
        xext_ref[0:SUBLANES, :] = xext_ref[TM:TM + SUBLANES, :]

    xext_ref[SUBLANES:SUBLANES + TM, :] = zx
    y = cb_ref[...] + cw_ref[CONV_WIDTH - 1:CONV_WIDTH, :] * zx
    for k in range(CONV_WIDTH - 1):
        off = SUBLANES - (CONV_WIDTH - 1) + k
        y = y + cw_ref[k:k + 1, :] * xext_ref[off:off + TM, :]

    yb = y.astype(BF16)
    ra, ix = [], []
    for p in range(RNN_WIDTH // LANES):
        g2 = jnp.dot(yb[:, p * LANES:(p + 1) * LANES], wg_ref[p], preferred_element_type=F32)
        ra.append(g2[:, :LANES])
        ix.append(g2[:, LANES:])
    rg = _sigmoid(jnp.concatenate(ra, axis=1) + ba_ref[...])
    ig = _sigmoid(jnp.concatenate(ix, axis=1) + bx_ref[...])
    nl = -lam_ref[...]
    softplus = jnp.maximum(nl, 0.0) + jnp.log1p(jnp.exp(-jnp.abs(nl)))
    log_a = (-LRU_C) * rg * softplus
    a_ref[...] = jnp.exp(log_a)
    th = jnp.tanh(log_a)
    b_ref[...] = jnp.sqrt(-2.0 * th / (1.0 - th)) * (ig * y)

    row = lax.broadcasted_iota(jnp.int32, (SUBLANES, RNN_WIDTH), 0)

    def scan_chunk(ci, hprev):
        r0 = pl.multiple_of(ci * SUBLANES, SUBLANES)
        av = a_ref[pl.ds(r0, SUBLANES), :]
        bv = b_ref[pl.ds(r0, SUBLANES), :]
        for d in (1, 2, 4):
            a_sh = jnp.where(row >= d, pltpu.roll(av, d, 0), 1.0)
            b_sh = jnp.where(row >= d, pltpu.roll(bv, d, 0), 0.0)
            bv = av * b_sh + bv
            av = av * a_sh
        hv = av * hprev + bv
        h_ref[pl.ds(r0, SUBLANES), :] = hv
        return jnp.broadcast_to(hv[SUBLANES - 1:SUBLANES, :], (SUBLANES, RNN_WIDTH))

    hc_ref[...] = lax.fori_loop(0, TM // SUBLANES, scan_chunk, hc_ref[...])

    zr = proj(C_GR, C_MG)
    u = (h_ref[...] * (zr * _sigmoid(zr))).astype(BF16)
    y_rnn = jnp.dot(u, wrp_ref[...], preferred_element_type=F32)

    gates = _sigmoid(proj(C_MG, C_END) + bm_ref[...])
    ga_ref[0] = gates[:, :D_MODEL].astype(BF16)
    r_ref[0] = (gates[:, D_MODEL:] * y_rnn).astype(BF16)


def _attn_kernel(topk, lq_ref, c_ref, ct_ref, qit_ref, ki_ref, wt_ref, sg_ref, wuv_ref,
                 u_ref,
                 lhst_ref, kext_ref, vt_ref, pos_ref, sc_ref, s_ref, acc_ref):
    i = pl.program_id(1)
    seq = kext_ref.shape[0]
    n_kt = (i * TQ + TQ + TK - 1) // TK
    n_cc = (i * TQ + TQ + CC - 1) // CC

    @pl.when(i == 0)
    def _():
        kext_ref[:, 0:KV_LATENT] = c_ref[0]
        for k in range(seq // TK):
            vt_ref[k, 0:KV_LATENT, :] = ct_ref[0, k]
            rr = lax.broadcasted_iota(jnp.int32, (VROWS - KV_LATENT, TK), 0)
            vt_ref[k, KV_LATENT:VROWS, :] = jnp.where(rr == 0, 1.0, 0.0).astype(BF16)
        er = lax.broadcasted_iota(jnp.int32, (TQ, TQ), 0)
        ec = lax.broadcasted_iota(jnp.int32, (TQ, TQ), 1)
        eye = jnp.where(er == ec, 1.0, 0.0).astype(BF16)
        for h in range(N_HEADS):
            lhst_ref[KV_LATENT:KV_LATENT + TQ, h * TQ:(h + 1) * TQ] = eye
        pos_ref[...] = lax.broadcasted_iota(jnp.int32, (seq, LANES), 0).astype(F32)

    lhst_ref[0:KV_LATENT, :] = lq_ref[0, 0]

    qit = qit_ref[0, 0]
    wv = wt_ref[0]
    tpos = i * TQ + lax.broadcasted_iota(jnp.int32, (TK, TQ), 1)
    srow = lax.broadcasted_iota(jnp.int32, (TK, TQ), 0)

    def fold8(v, op):
        parts = [op(v[k * KC:(k + 1) * KC].reshape(KC // SUBLANES, SUBLANES, TQ), axis=0)
                 for k in range(v.shape[0] // KC)]
        return op(jnp.stack(parts), axis=0)

    def score_chunk(kt, carry):
        lo8, hi8 = carry
        r0 = pl.multiple_of(kt * TK, TK)
        prod = jnp.dot(ki_ref[0, pl.ds(r0, TK), :], qit, preferred_element_type=F32)
        sc = jnp.maximum(prod[:, 0:TQ], 0.0) * wv[0:1, :]
        for jh in range(1, IDX_HEADS):
            sc = sc + jnp.maximum(prod[:, jh * TQ:(jh + 1) * TQ], 0.0) * wv[jh:jh + 1, :]
        causal = (srow + r0) <= tpos
        sc_ref[pl.ds(r0, TK), :] = jnp.where(causal, sc, -jnp.inf)
        lo8 = jnp.minimum(lo8, fold8(jnp.where(causal, sc, jnp.inf), jnp.min))
        hi8 = jnp.maximum(hi8, fold8(jnp.where(causal, sc, -jnp.inf), jnp.max))
        return lo8, hi8

    lo8, hi8 = lax.fori_loop(
        0, n_kt, score_chunk,
        (jnp.full((SUBLANES, TQ), jnp.inf, F32), jnp.full((SUBLANES, TQ), -jnp.inf, F32)))

    @pl.when(n_kt * TK < n_cc * CC)
    def _():
        r0 = pl.multiple_of(n_kt * TK, TK)
        sc_ref[pl.ds(r0, TK), :] = jnp.full((TK, TQ), -jnp.inf, F32)

    lo_k = _f32_to_key(jnp.min(lo8, axis=0, keepdims=True))
    vmax = jnp.max(hi8, axis=0, keepdims=True)
    hi_k = jnp.where(jnp.abs(vmax) < F32_MIN_NORMAL, _f32_to_key(jnp.full_like(vmax, F32_MIN_NORMAL)),
                     _f32_to_key(vmax) + 1)
    kf = float(topk)

    def count_ge(thr):
        def body(cc, c8):
            r0 = pl.multiple_of(cc * CC, CC)
            xs = sc_ref[pl.ds(r0, CC), :]
            return c8 + fold8(jnp.where(xs >= thr, 1.0, 0.0), jnp.sum)
        c8 = lax.fori_loop(0, n_cc, body, jnp.zeros((SUBLANES, TQ), F32))
        return jnp.sum(c8, axis=0, keepdims=True)

    def midpoint(lo_, hi_):
        return (lo_ >> 1) + (hi_ >> 1) + (lo_ & hi_ & 1)

    def bisect(carry):
        lo_, hi_, clo_ = carry
        mid = midpoint(lo_, hi_)
        cnt = count_ge(_key_to_f32(mid))
        ge = cnt >= kf
        return jnp.where(ge, mid, lo_), jnp.where(ge, hi_, mid), jnp.where(ge, cnt, clo_)

    def unresolved(carry):
        lo_, hi_, clo_ = carry
        open_ = jnp.where(clo_ > kf, jnp.where(midpoint(lo_, hi_) != lo_, 1.0, 0.0), 0.0)
        return (jnp.max(open_) > 0.5).astype(jnp.int32)

    n_causal = (tpos[0:1, :] + 1).astype(F32)
    carry = lax.fori_loop(0, BISECT_FIXED, lambda _, cr: bisect(cr), (lo_k, hi_k, n_causal))

    def search_more(state):
        it, _, cr = state
        for _ in range(BISECT_PER_CHECK):
            cr = bisect(cr)
        return it + 1, unresolved(cr), cr

    _, _, (lo_k, hi_k, c_lo) = lax.while_loop(
        lambda st: jnp.logical_and(st[1] > 0, st[0] < BISECT_MAX_CHECKS),
        search_more, (jnp.int32(0), unresolved(carry), carry))
    lo = _key_to_f32(lo_k)
    hi = _key_to_f32(hi_k)

    has_ties = jnp.max(jnp.where(c_lo > kf, 1.0, 0.0)) > 0.5

    @pl.when(jnp.logical_not(has_ties))
    def _():
        def mask_chunk(kt, _):
            r0 = pl.multiple_of(kt * TK, TK)
            xs = sc_ref[pl.ds(r0, TK), :]
            kext_ref[pl.ds(r0, TK), KV_LATENT:KV_LATENT + TQ] = (
                jnp.where(xs >= lo, 0.0, MASK_NEG).astype(BF16))
            return 0
        lax.fori_loop(0, n_kt, mask_chunk, 0)

    @pl.when(has_ties)
    def _():
        need = kf - count_ge(hi)
        tri = jnp.where(lax.broadcasted_iota(jnp.int32, (KC, KC), 1)
                        <= lax.broadcasted_iota(jnp.int32, (KC, KC), 0), 1.0, 0.0).astype(BF16)

        def mask_chunk(kc, before):
            r0 = pl.multiple_of(kc * KC, KC)
            xs = sc_ref[pl.ds(r0, KC), :]
            band = jnp.where(xs >= lo, jnp.where(xs < hi, 1.0, 0.0), 0.0)
            incl = jnp.dot(tri, band.astype(BF16), preferred_element_type=F32)
            rank = incl - band + before
            keep = jnp.where(xs >= hi, 1.0, jnp.where(rank < need, band, 0.0))
            kext_ref[pl.ds(r0, KC), KV_LATENT:KV_LATENT + TQ] = (
                jnp.where(keep > 0.0, 0.0, MASK_NEG).astype(BF16))
            return before + incl[KC - 1:KC, :]

        lax.fori_loop(0, (TK // KC) * n_kt, mask_chunk, jnp.zeros((1, TQ), F32))

    def qk_chunk(kt, mx):
        r0 = pl.multiple_of(kt * TK, TK)
        st = jnp.dot(kext_ref[pl.ds(r0, TK), :], lhst_ref[...], preferred_element_type=F32)
        posb = pos_ref[pl.ds(r0, TK), :]
        tops = []
        for h in range(N_HEADS):
            sh = st[:, h * TQ:(h + 1) * TQ] + posb * (ALIBI_SLOPES[h] * LOG2E)
            s_ref[pl.ds(r0, TK), h * TQ:(h + 1) * TQ] = sh
            tops.append(fold8(sh, jnp.max))
        return jnp.maximum(mx, jnp.concatenate(tops, axis=1))

    mx = lax.fori_loop(0, n_kt, qk_chunk, jnp.full((SUBLANES, N_HEADS * TQ), M_INIT, F32))
    m = jnp.max(mx, axis=0, keepdims=True)
    acc_ref[...] = jnp.zeros(acc_ref.shape, F32)

    def pv_chunk(kt, _):
        r0 = pl.multiple_of(kt * TK, TK)
        pt = jnp.exp2(s_ref[pl.ds(r0, TK), :] - m).astype(BF16)
        acc_ref[...] += jnp.dot(vt_ref[kt], pt, preferred_element_type=F32)
        return 0

    lax.fori_loop(0, n_kt, pv_chunk, 0)

    o_parts = []
    for p in range(N_HEADS // 2):
        cols = slice(p * 2 * TQ, (p + 1) * 2 * TQ)
        xt = (acc_ref[0:KV_LATENT, cols] / acc_ref[KV_LATENT:KV_LATENT + 1, cols]).T
        pair = jnp.concatenate([xt[:TQ], xt[TQ:]], axis=1).astype(BF16)
        o_parts.append(jnp.dot(pair, wuv_ref[p], preferred_element_type=F32))
    o = jnp.concatenate(o_parts, axis=1)
    u_ref[0] = (o * sg_ref[0].astype(F32)).astype(BF16)


def _out_kernel(u_ref, ga_ref, r_ref, x_ref, wap_ref, wout_ref, fg_ref, out_ref):
    y_attn = jnp.dot(u_ref[0], wap_ref[...], preferred_element_type=F32)
    mixed = (ga_ref[0].astype(F32) * y_attn + r_ref[0].astype(F32)).astype(BF16)
    res = x_ref[0] + jnp.dot(mixed, wout_ref[...], preferred_element_type=F32)
    out_ref[0] = res * lax.rsqrt(jnp.mean(res * res, axis=-1, keepdims=True) + NORM_EPS) * fg_ref[...]


def _const_spec(shape):
    nd = len(shape)
    return pl.BlockSpec(shape, lambda b, i: (0,) * nd, pipeline_mode=pl.Buffered(1))


def _pack_w_in(w_in):
    offs = np.cumsum([0, ATTN_WIDTH, KV_LATENT, IDX_HEADS * IDX_DIM, IDX_DIM, IDX_HEADS,
                      ATTN_WIDTH, RNN_WIDTH, RNN_WIDTH, 2 * D_MODEL])
    seg = [w_in[:, offs[k]:offs[k + 1]] for k in range(9)]
    pad = jnp.zeros((D_MODEL, LANES - IDX_DIM - IDX_HEADS), w_in.dtype)
    return jnp.concatenate(seg[:5] + [pad] + seg[5:], axis=1).astype(BF16)


def _pair_block_diag(w):
    n2, r, c = w.shape
    w = w.reshape(n2 // 2, 2, r, c)
    z = jnp.zeros((n2 // 2, r, c), w.dtype)
    top = jnp.concatenate([w[:, 0], z], axis=2)
    bot = jnp.concatenate([z, w[:, 1]], axis=2)
    return jnp.concatenate([top, bot], axis=1)


def _layer(x, norm_g, w_in, b_merge, kv_g, w_uk, w_uv, ln_g, ln_b, w_ap, conv_w, conv_b,
           w_a, b_a, w_x, b_x, lam, w_rp, w_out, out_g):
    B, S, _ = x.shape
    assert S % TM == 0 and TM % TK == 0 and TK % TQ == 0 and TQ == KC
    assert S % CC == 0 and CC % TK == 0
    topk = min(TOPK_MAX, S // 4)
    row = lambda v: v.reshape(1, -1).astype(F32)
    lane_pad = lambda v: jnp.pad(v.astype(F32), (0, LANES - v.shape[0])).reshape(1, LANES)

    w_all = _pack_w_in(w_in)
    wg = jnp.concatenate([_pair_block_diag(w_a), _pair_block_diag(w_x)], axis=2).astype(BF16)
    wuk2t = _pair_block_diag(jnp.swapaxes(w_uk, 1, 2)).astype(BF16)

    nq = S // TQ
    tile = lambda n: pl.BlockSpec((1, TM, n), lambda b, j: (b, j, 0))
    outs = pl.pallas_call(
        _proj_kernel,
        grid=(B, S // TM),
        in_specs=[
            tile(D_MODEL),
            _const_spec((1, D_MODEL)),
            _const_spec((D_MODEL, C_END)),
            _const_spec((N_HEADS // 2, 2 * HEAD_DIM, 2 * KV_LATENT)),
            _const_spec((1, KV_LATENT)),
            _const_spec((1, LANES)),
            _const_spec((1, LANES)),
            _const_spec((CONV_WIDTH, RNN_WIDTH)),
            _const_spec((1, RNN_WIDTH)),
            _const_spec((RNN_WIDTH // LANES, LANES, 2 * LANES)),
            _const_spec((1, RNN_WIDTH)),
            _const_spec((1, RNN_WIDTH)),
            _const_spec((1, RNN_WIDTH)),
            _const_spec((RNN_WIDTH, D_MODEL)),
            _const_spec((1, 2 * D_MODEL)),
        ],
        out_specs=[
            pl.BlockSpec((1, TM // TQ, KV_LATENT, N_HEADS * TQ), lambda b, j: (b, j, 0, 0)),
            tile(KV_LATENT),
            pl.BlockSpec((1, TM // TK, KV_LATENT, TK), lambda b, j: (b, j, 0, 0)),
            pl.BlockSpec((1, TM // TQ, IDX_DIM, IDX_HEADS * TQ), lambda b, j: (b, j, 0, 0)),
            tile(IDX_DIM),
            pl.BlockSpec((1, IDX_HEADS, TM), lambda b, j: (b, 0, j)),
            tile(ATTN_WIDTH),
            tile(D_MODEL),
            tile(D_MODEL),
        ],
        out_shape=[
            jax.ShapeDtypeStruct((B, nq, KV_LATENT, N_HEADS * TQ), BF16),
            jax.ShapeDtypeStruct((B, S, KV_LATENT), BF16),
            jax.ShapeDtypeStruct((B, S // TK, KV_LATENT, TK), BF16),
            jax.ShapeDtypeStruct((B, nq, IDX_DIM, IDX_HEADS * TQ), BF16),
            jax.ShapeDtypeStruct((B, S, IDX_DIM), BF16),
            jax.ShapeDtypeStruct((B, IDX_HEADS, S), F32),
            jax.ShapeDtypeStruct((B, S, ATTN_WIDTH), BF16),
            jax.ShapeDtypeStruct((B, S, D_MODEL), BF16),
            jax.ShapeDtypeStruct((B, S, D_MODEL), BF16),
        ],
        scratch_shapes=[
            pltpu.VMEM((TM + SUBLANES, RNN_WIDTH), F32),
            pltpu.VMEM((TM, RNN_WIDTH), F32),
            pltpu.VMEM((TM, RNN_WIDTH), F32),
            pltpu.VMEM((TM, RNN_WIDTH), F32),
            pltpu.VMEM((SUBLANES, RNN_WIDTH), F32),
        ],
        compiler_params=pltpu.CompilerParams(
            dimension_semantics=("arbitrary", "arbitrary"),
            vmem_limit_bytes=VMEM_LIMIT_BYTES),
        name="proj_rglru",
    )(x, row(norm_g), w_all, wuk2t, row(kv_g), lane_pad(ln_g), lane_pad(ln_b), conv_w.astype(F32),
      row(conv_b), wg, row(b_a), row(b_x), row(lam), w_rp.astype(BF16), row(b_merge))
    lq, c, ct, qit, ki, wt, sg, ga, r = outs

    wuv2 = _pair_block_diag(w_uv).astype(BF16)

    qtile = lambda n: pl.BlockSpec((1, TQ, n), lambda b, i: (b, i, 0))
    u = pl.pallas_call(
        functools.partial(_attn_kernel, topk),
        grid=(B, nq),
        in_specs=[
            pl.BlockSpec((1, 1, KV_LATENT, N_HEADS * TQ), lambda b, i: (b, i, 0, 0)),
            pl.BlockSpec((1, S, KV_LATENT), lambda b, i: (b, 0, 0)),
            pl.BlockSpec((1, S // TK, KV_LATENT, TK), lambda b, i: (b, 0, 0, 0)),
            pl.BlockSpec((1, 1, IDX_DIM, IDX_HEADS * TQ), lambda b, i: (b, i, 0, 0)),
            pl.BlockSpec((1, S, IDX_DIM), lambda b, i: (b, 0, 0)),
            pl.BlockSpec((1, IDX_HEADS, TQ), lambda b, i: (b, 0, i)),
            qtile(ATTN_WIDTH),
            _const_spec((N_HEADS // 2, 2 * KV_LATENT, 2 * HEAD_DIM)),
        ],
        out_specs=qtile(ATTN_WIDTH),
        out_shape=jax.ShapeDtypeStruct((B, S, ATTN_WIDTH), BF16),
        scratch_shapes=[
            pltpu.VMEM((2 * KV_LATENT, N_HEADS * TQ), BF16),
            pltpu.VMEM((S, 2 * KV_LATENT), BF16),
            pltpu.VMEM((S // TK, VROWS, TK), BF16),
            pltpu.VMEM((S, LANES), F32),
            pltpu.VMEM((S, TQ), F32),
            pltpu.VMEM((S, N_HEADS * TQ), F32),
            pltpu.VMEM((VROWS, N_HEADS * TQ), F32),
        ],
        compiler_params=pltpu.CompilerParams(
            dimension_semantics=("arbitrary", "arbitrary"),
            vmem_limit_bytes=VMEM_LIMIT_BYTES),
        name="sparse_attn",
    )(lq, c, ct, qit, ki, wt, sg, wuv2)

    return pl.pallas_call(
        _out_kernel,
        grid=(B, S // TM),
        in_specs=[
            tile(ATTN_WIDTH),
            tile(D_MODEL),
            tile(D_MODEL),
            tile(D_MODEL),
            _const_spec((ATTN_WIDTH, D_MODEL)),
            _const_spec((D_MODEL, D_MODEL)),
            _const_spec((1, D_MODEL)),
        ],
        out_specs=tile(D_MODEL),
        out_shape=jax.ShapeDtypeStruct((B, S, D_MODEL), F32),
        compiler_params=pltpu.CompilerParams(
            dimension_semantics=("arbitrary", "arbitrary"),
            vmem_limit_bytes=VMEM_LIMIT_BYTES),
        name="merge_out",
    )(u, ga, r, x, w_ap.astype(BF16), w_out.astype(BF16), row(out_g))


def kernel(x, norm_gain, w_in, b_merge, kv_norm_gain, w_uk, w_uv, idx_ln_gain, idx_ln_bias,
           w_attn_proj, conv_w, conv_b, w_rg_a, b_rg_a, w_rg_x, b_rg_x, lru_lambda,
           w_rnn_proj, w_out, final_norm_gain):
    assert norm_gain.shape[0] == 1, "only the stated depth-1 stack is supported"
    return _layer(x, norm_gain[0], w_in[0], b_merge[0], kv_norm_gain[0], w_uk[0], w_uv[0],
                  idx_ln_gain[0], idx_ln_bias[0], w_attn_proj[0], conv_w[0], conv_b[0],
                  w_rg_a[0], b_rg_a[0], w_rg_x[0], b_rg_x[0], lru_lambda[0],
                  w_rnn_proj[0], w_out[0], final_norm_gain)
```

```python
import functools

import numpy as np
import jax
import jax.numpy as jnp
from jax import lax
from jax.experimental import pallas as pl
from jax.experimental.pallas import tpu as pltpu

D_MODEL = 1024
N_HEADS = 16
HEAD_DIM = 64
ATTN_WIDTH = N_HEADS * HEAD_DIM
KV_LATENT = 128
IDX_HEADS = 8
IDX_DIM = 64
TOPK_MAX = 256
RNN_WIDTH = 1024
RNN_BLOCKS = 16
RNN_BLOCK_DIM = RNN_WIDTH // RNN_BLOCKS
CONV_WIDTH = 4
LRU_C = 8.0
NORM_EPS = 1e-6

F32 = jnp.float32
BF16 = jnp.bfloat16

LANES = 128
SUBLANES = 8
VMEM_LIMIT_BYTES = 56 * 1024 * 1024

TM = 512
TQ = 128
KC = 128
TK = 256
CC = 512
TB = 2 * TK
VROWS = 144
BISECT_FIXED = 8
BISECT_PER_CHECK = 2
BISECT_MAX_CHECKS = 14
LOG2E = 1.4426950408889634
F32_MIN_NORMAL = 2.0 ** -126
MASK_NEG = -(2.0 ** 100)
M_INIT = -1e20

C_Q = 0
C_C = C_Q + ATTN_WIDTH
C_QI = C_C + KV_LATENT
C_KW = C_QI + IDX_HEADS * IDX_DIM
C_GA = C_KW + LANES
C_XR = C_GA + ATTN_WIDTH
C_GR = C_XR + RNN_WIDTH
C_MG = C_GR + RNN_WIDTH
C_END = C_MG + 2 * D_MODEL

ALIBI_SLOPES = [float(2.0 ** (-8.0 * (h + 1) / N_HEADS)) for h in range(N_HEADS)]


def _sigmoid(v):
    return jax.nn.sigmoid(v)


def _f32_to_key(v):
    b = lax.bitcast_convert_type(v, jnp.int32)
    return b ^ ((b >> 31) & 0x7FFFFFFF)


def _key_to_f32(k):
    return lax.bitcast_convert_type(k ^ ((k >> 31) & 0x7FFFFFFF), F32)


def _proj_kernel(x_ref, ng_ref, w_ref, wuk_ref, kvg_ref, lng_ref, lnb_ref, cw_ref, cb_ref, wg_ref,
                 ba_ref, bx_ref, lam_ref, wrp_ref, bm_ref,
                 lq_ref, c_ref, ct_ref, qit_ref, ki_ref, wt_ref, sg_ref, ga_ref, r_ref,
                 xext_ref, a_ref, b_ref, h_ref, hc_ref):
    j = pl.program_id(1)
    x = x_ref[0]
    xn = x * lax.rsqrt(jnp.mean(x * x, axis=-1, keepdims=True) + NORM_EPS) * ng_ref[...]
    xb = xn.astype(BF16)

    def proj(lo, hi):
        return jnp.dot(xb, w_ref[:, lo:hi], preferred_element_type=F32)

    qb16 = proj(C_Q, C_C).astype(BF16)
    ql = [jnp.dot(qb16[:, p * LANES:(p + 1) * LANES], wuk_ref[p], preferred_element_type=F32)
          for p in range(N_HEADS // 2)]
    qlt = (jnp.concatenate(ql, axis=1) * (HEAD_DIM ** -0.5 * LOG2E)).T
    for qb in range(TM // TQ):
        for h in range(N_HEADS):
            lq_ref[0, qb, :, h * TQ:(h + 1) * TQ] = (
                qlt[h * KV_LATENT:(h + 1) * KV_LATENT, qb * TQ:(qb + 1) * TQ].astype(BF16))

    zc = proj(C_C, C_QI)
    cn = zc * lax.rsqrt(jnp.mean(zc * zc, axis=-1, keepdims=True) + NORM_EPS) * kvg_ref[...]
    c_ref[0] = cn.astype(BF16)
    cnt = cn.T
    for k in range(TM // TK):
        ct_ref[0, k] = cnt[:, k * TK:(k + 1) * TK].astype(BF16)

    zqt = (proj(C_QI, C_KW) * (IDX_DIM ** -0.5)).T
    for qb in range(TM // TQ):
        for jh in range(IDX_HEADS):
            qit_ref[0, qb, :, jh * TQ:(jh + 1) * TQ] = (
                zqt[jh * IDX_DIM:(jh + 1) * IDX_DIM, qb * TQ:(qb + 1) * TQ].astype(BF16))

    zk = proj(C_KW, C_GA)
    lane = lax.broadcasted_iota(jnp.int32, zk.shape, 1)
    is_k = lane < IDX_DIM
    mu = jnp.sum(jnp.where(is_k, zk, 0.0), axis=-1, keepdims=True) * (1.0 / IDX_DIM)
    kc = jnp.where(is_k, zk - mu, 0.0)
    var = jnp.sum(kc * kc, axis=-1, keepdims=True) * (1.0 / IDX_DIM)
    kin = kc * lax.rsqrt(var + NORM_EPS) * lng_ref[...] + lnb_ref[...]
    ki_ref[0] = kin[:, :IDX_DIM].astype(BF16)
    wt_ref[0] = zk.T[IDX_DIM:IDX_DIM + IDX_HEADS, :] * (IDX_HEADS ** -0.5)

    zg = proj(C_GA, C_XR)
    sg_ref[0] = (zg * _sigmoid(zg)).astype(BF16)

    zx = proj(C_XR, C_GR)

    @pl.when(j == 0)
    def _():
        xext_ref[0:SUBLANES, :] = jnp.zeros((SUBLANES, RNN_WIDTH), F32)
        hc_ref[...] = jnp.zeros((SUBLANES, RNN_WIDTH), F32)

    @pl.when(j > 0)
    def _():
        xext_ref[0:SUBLANES, :] = xext_ref[TM:TM + SUBLANES, :]

    xext_ref[SUBLANES:SUBLANES + TM, :] = zx
    y = cb_ref[...] + cw_ref[CONV_WIDTH - 1:CONV_WIDTH, :] * zx
    for k in range(CONV_WIDTH - 1):
        off = SUBLANES - (CONV_WIDTH - 1) + k
        y = y + cw_ref[k:k + 1, :] * xext_ref[off:off + TM, :]

    yb = y.astype(BF16)
    ra, ix = [], []
    for p in range(RNN_WIDTH // LANES):
        g2 = jnp.dot(yb[:, p * LANES:(p + 1) * LANES], wg_ref[p], preferred_element_type=F32)
        ra.append(g2[:, :LANES])
        ix.append(g2[:, LANES:])
    rg = _sigmoid(jnp.concatenate(ra, axis=1) + ba_ref[...])
    ig = _sigmoid(jnp.concatenate(ix, axis=1) + bx_ref[...])
    nl = -lam_ref[...]
    softplus = jnp.maximum(nl, 0.0) + jnp.log1p(jnp.exp(-jnp.abs(nl)))
    log_a = (-LRU_C) * rg * softplus
    a_ref[...] = jnp.exp(log_a)
    th = jnp.tanh(log_a)
    b_ref[...] = jnp.sqrt(-2.0 * th / (1.0 - th)) * (ig * y)

    row = lax.broadcasted_iota(jnp.int32, (SUBLANES, RNN_WIDTH), 0)

    def scan_chunk(ci, hprev):
        r0 = pl.multiple_of(ci * SUBLANES, SUBLANES)
        av = a_ref[pl.ds(r0, SUBLANES), :]
        bv = b_ref[pl.ds(r0, SUBLANES), :]
        for d in (1, 2, 4):
            a_sh = jnp.where(row >= d, pltpu.roll(av, d, 0), 1.0)
            b_sh = jnp.where(row >= d, pltpu.roll(bv, d, 0), 0.0)
            bv = av * b_sh + bv
            av = av * a_sh
        hv = av * hprev + bv
        h_ref[pl.ds(r0, SUBLANES), :] = hv
        return jnp.broadcast_to(hv[SUBLANES - 1:SUBLANES, :], (SUBLANES, RNN_WIDTH))

    hc_ref[...] = lax.fori_loop(0, TM // SUBLANES, scan_chunk, hc_ref[...])

    zr = proj(C_GR, C_MG)
    u = (h_ref[...] * (zr * _sigmoid(zr))).astype(BF16)
    y_rnn = jnp.dot(u, wrp_ref[...], preferred_element_type=F32)

    gates = _sigmoid(proj(C_MG, C_END) + bm_ref[...])
    ga_ref[0] = gates[:, :D_MODEL].astype(BF16)
    r_ref[0] = (gates[:, D_MODEL:] * y_rnn).astype(BF16)


def _attn_kernel(topk, lq_ref, c_ref, ct_ref, qit_ref, ki_ref, wt_ref, sg_ref, wuv_ref,
                 u_ref,
                 lhst_ref, kext_ref, vt_ref, pos_ref, sc_ref, s_ref, acc_ref):
    i = pl.program_id(1)
    seq = kext_ref.shape[0]
    n_kt = (i * TQ + TQ + TK - 1) // TK
    n_cc = (i * TQ + TQ + CC - 1) // CC

    @pl.when(jnp.logical_and(pl.program_id(0) == 0, i == 0))
    def _():
        er = lax.broadcasted_iota(jnp.int32, (TQ, TQ), 0)
        ec = lax.broadcasted_iota(jnp.int32, (TQ, TQ), 1)
        eye = jnp.where(er == ec, 1.0, 0.0).astype(BF16)
        for h in range(N_HEADS):
            lhst_ref[KV_LATENT:KV_LATENT + TQ, h * TQ:(h + 1) * TQ] = eye
        pos_ref[...] = lax.broadcasted_iota(jnp.int32, (seq, LANES), 0).astype(F32)
        rr = lax.broadcasted_iota(jnp.int32, (VROWS - KV_LATENT, TB), 0)
        for k in range(seq // TB):
            vt_ref[k, KV_LATENT:VROWS, :] = jnp.where(rr == 0, 1.0, 0.0).astype(BF16)

    @pl.when(i == 0)
    def _():
        kext_ref[:, 0:KV_LATENT] = c_ref[0]
        for k in range(seq // TK):
            vt_ref[k // (TB // TK), 0:KV_LATENT, (k % (TB // TK)) * TK:(k % (TB // TK) + 1) * TK] = (
                ct_ref[0, k])

    lhst_ref[0:KV_LATENT, :] = lq_ref[0, 0]

    qit = qit_ref[0, 0]
    wv = wt_ref[0]
    tpos = i * TQ + lax.broadcasted_iota(jnp.int32, (1, TQ), 1)

    def fold8(v, op):
        parts = [op(v[k * KC:(k + 1) * KC].reshape(KC // SUBLANES, SUBLANES, v.shape[1]), axis=0)
                 for k in range(v.shape[0] // KC)]
        return op(jnp.stack(parts), axis=0)

    n_big = n_kt // 2
    has_tail = n_kt % 2 == 1
    r_tail = pl.multiple_of(n_big * TB, TB)

    def score_block(r0, rows, carry):
        lo8, hi8 = carry
        prod = jnp.dot(ki_ref[0, pl.ds(r0, rows), :], qit, preferred_element_type=F32)
        sc = jnp.maximum(prod[:, 0:TQ], 0.0) * wv[0:1, :]
        for jh in range(1, IDX_HEADS):
            sc = sc + jnp.maximum(prod[:, jh * TQ:(jh + 1) * TQ], 0.0) * wv[jh:jh + 1, :]
        causal = (lax.broadcasted_iota(jnp.int32, (rows, TQ), 0) + r0) <= tpos
        sc_ref[pl.ds(r0, rows), :] = jnp.where(causal, sc, -jnp.inf)
        lo8 = jnp.minimum(lo8, fold8(jnp.where(causal, sc, jnp.inf), jnp.min))
        hi8 = jnp.maximum(hi8, fold8(jnp.where(causal, sc, -jnp.inf), jnp.max))
        return lo8, hi8

    ext = lax.fori_loop(
        0, n_big, lambda k, cr: score_block(pl.multiple_of(k * TB, TB), TB, cr),
        (jnp.full((SUBLANES, TQ), jnp.inf, F32), jnp.full((SUBLANES, TQ), -jnp.inf, F32)))
    lo8, hi8 = lax.cond(has_tail, lambda cr: score_block(r_tail, TK, cr), lambda cr: cr, ext)

    @pl.when(n_kt * TK < n_cc * CC)
    def _():
        r0 = pl.multiple_of(n_kt * TK, TK)
        sc_ref[pl.ds(r0, TK), :] = jnp.full((TK, TQ), -jnp.inf, F32)

    lo_k = _f32_to_key(jnp.min(lo8, axis=0, keepdims=True))
    vmax = jnp.max(hi8, axis=0, keepdims=True)
    hi_k = jnp.where(jnp.abs(vmax) < F32_MIN_NORMAL, _f32_to_key(jnp.full_like(vmax, F32_MIN_NORMAL)),
                     _f32_to_key(vmax) + 1)
    kf = float(topk)

    def count_ge(thr):
        def body(cc, c8):
            r0 = pl.multiple_of(cc * CC, CC)
            xs = sc_ref[pl.ds(r0, CC), :]
            return c8 + fold8(jnp.where(xs >= thr, 1.0, 0.0), jnp.sum)
        c8 = lax.fori_loop(0, n_cc, body, jnp.zeros((SUBLANES, TQ), F32))
        return jnp.sum(c8, axis=0, keepdims=True)

    def midpoint(lo_, hi_):
        return (lo_ >> 1) + (hi_ >> 1) + (lo_ & hi_ & 1)

    def bisect(carry):
        lo_, hi_, clo_ = carry
        mid = midpoint(lo_, hi_)
        cnt = count_ge(_key_to_f32(mid))
        ge = cnt >= kf
        return jnp.where(ge, mid, lo_), jnp.where(ge, hi_, mid), jnp.where(ge, cnt, clo_)

    def unresolved(carry):
        lo_, hi_, clo_ = carry
        open_ = jnp.where(clo_ > kf, jnp.where(midpoint(lo_, hi_) != lo_, 1.0, 0.0), 0.0)
        return (jnp.max(open_) > 0.5).astype(jnp.int32)

    n_causal = (tpos[0:1, :] + 1).astype(F32)
    carry = lax.fori_loop(0, BISECT_FIXED, lambda _, cr: bisect(cr), (lo_k, hi_k, n_causal))

    def search_more(state):
        it, _, cr = state
        for _ in range(BISECT_PER_CHECK):
            cr = bisect(cr)
        return it + 1, unresolved(cr), cr

    _, _, (lo_k, hi_k, c_lo) = lax.while_loop(
        lambda st: jnp.logical_and(st[1] > 0, st[0] < BISECT_MAX_CHECKS),
        search_more, (jnp.int32(0), unresolved(carry), carry))
    lo = _key_to_f32(lo_k)
    hi = _key_to_f32(hi_k)

    has_ties = jnp.max(jnp.where(c_lo > kf, 1.0, 0.0)) > 0.5

    @pl.when(jnp.logical_not(has_ties))
    def _():
        def mask_chunk(kt, _):
            r0 = pl.multiple_of(kt * TK, TK)
            xs = sc_ref[pl.ds(r0, TK), :]
            kext_ref[pl.ds(r0, TK), KV_LATENT:KV_LATENT + TQ] = (
                jnp.where(xs >= lo, 0.0, MASK_NEG).astype(BF16))
            return 0
        lax.fori_loop(0, n_kt, mask_chunk, 0)

    @pl.when(has_ties)
    def _():
        need = kf - count_ge(hi)
        tri = jnp.where(lax.broadcasted_iota(jnp.int32, (KC, KC), 1)
                        <= lax.broadcasted_iota(jnp.int32, (KC, KC), 0), 1.0, 0.0).astype(BF16)

        def mask_chunk(kc, before):
            r0 = pl.multiple_of(kc * KC, KC)
            xs = sc_ref[pl.ds(r0, KC), :]
            band = jnp.where(xs >= lo, jnp.where(xs < hi, 1.0, 0.0), 0.0)
            incl = jnp.dot(tri, band.astype(BF16), preferred_element_type=F32)
            rank = incl - band + before
            keep = jnp.where(xs >= hi, 1.0, jnp.where(rank < need, band, 0.0))
            kext_ref[pl.ds(r0, KC), KV_LATENT:KV_LATENT + TQ] = (
                jnp.where(keep > 0.0, 0.0, MASK_NEG).astype(BF16))
            return before + incl[KC - 1:KC, :]

        lax.fori_loop(0, (TK // KC) * n_kt, mask_chunk, jnp.zeros((1, TQ), F32))


    def qk_block(r0, rows, mx):
        st = jnp.dot(kext_ref[pl.ds(r0, rows), :], lhst_ref[...], preferred_element_type=F32)
        posb = pos_ref[pl.ds(r0, rows), :]
        tops = []
        for h in range(N_HEADS):
            sh = st[:, h * TQ:(h + 1) * TQ] + posb * (ALIBI_SLOPES[h] * LOG2E)
            s_ref[pl.ds(r0, rows), h * TQ:(h + 1) * TQ] = sh
            tops.append(fold8(sh, jnp.max))
        return jnp.maximum(mx, jnp.concatenate(tops, axis=1))

    mx = lax.fori_loop(0, n_big, lambda k, v: qk_block(pl.multiple_of(k * TB, TB), TB, v),
                       jnp.full((SUBLANES, N_HEADS * TQ), M_INIT, F32))
    mx = lax.cond(has_tail, lambda v: qk_block(r_tail, TK, v), lambda v: v, mx)
    m = jnp.max(mx, axis=0, keepdims=True)
    acc_ref[...] = jnp.zeros(acc_ref.shape, F32)

    def pv_block(r0, rows, vt):
        pt = jnp.exp2(s_ref[pl.ds(r0, rows), :] - m).astype(BF16)
        acc_ref[...] += jnp.dot(vt, pt, preferred_element_type=F32)

    def pv_big(k, _):
        pv_block(pl.multiple_of(k * TB, TB), TB, vt_ref[k])
        return 0

    lax.fori_loop(0, n_big, pv_big, 0)

    @pl.when(has_tail)
    def _():
        pv_block(r_tail, TK, vt_ref[n_big, :, 0:TK])

    o_parts = []
    for p in range(N_HEADS // 2):
        cols = slice(p * 2 * TQ, (p + 1) * 2 * TQ)
        xt = (acc_ref[0:KV_LATENT, cols] / acc_ref[KV_LATENT:KV_LATENT + 1, cols]).T
        pair = jnp.concatenate([xt[:TQ], xt[TQ:]], axis=1).astype(BF16)
        o_parts.append(jnp.dot(pair, wuv_ref[p], preferred_element_type=F32))
    o = jnp.concatenate(o_parts, axis=1)
    u_ref[0] = (o * sg_ref[0].astype(F32)).astype(BF16)


def _out_kernel(u_ref, ga_ref, r_ref, x_ref, wap_ref, wout_ref, fg_ref, out_ref):
    y_attn = jnp.dot(u_ref[0], wap_ref[...], preferred_element_type=F32)
    mixed = (ga_ref[0].astype(F32) * y_attn + r_ref[0].astype(F32)).astype(BF16)
    res = x_ref[0] + jnp.dot(mixed, wout_ref[...], preferred_element_type=F32)
    out_ref[0] = res * lax.rsqrt(jnp.mean(res * res, axis=-1, keepdims=True) + NORM_EPS) * fg_ref[...]


def _const_spec(shape):
    nd = len(shape)
    return pl.BlockSpec(shape, lambda b, i: (0,) * nd, pipeline_mode=pl.Buffered(1))


def _pack_w_in(w_in):
    offs = np.cumsum([0, ATTN_WIDTH, KV_LATENT, IDX_HEADS * IDX_DIM, IDX_DIM, IDX_HEADS,
                      ATTN_WIDTH, RNN_WIDTH, RNN_WIDTH, 2 * D_MODEL])
    seg = [w_in[:, offs[k]:offs[k + 1]] for k in range(9)]
    pad = jnp.zeros((D_MODEL, LANES - IDX_DIM - IDX_HEADS), w_in.dtype)
    return jnp.concatenate(seg[:5] + [pad] + seg[5:], axis=1).astype(BF16)


def _pair_block_diag(w):
    n2, r, c = w.shape
    w = w.reshape(n2 // 2, 2, r, c)
    z = jnp.zeros((n2 // 2, r, c), w.dtype)
    top = jnp.concatenate([w[:, 0], z], axis=2)
    bot = jnp.concatenate([z, w[:, 1]], axis=2)
    return jnp.concatenate([top, bot], axis=1)


def _layer(x, norm_g, w_in, b_merge, kv_g, w_uk, w_uv, ln_g, ln_b, w_ap, conv_w, conv_b,
           w_a, b_a, w_x, b_x, lam, w_rp, w_out, out_g):
    B, S, _ = x.shape
    assert S % TM == 0 and TM % TK == 0 and TK % TQ == 0 and TQ == KC
    assert S % CC == 0 and CC % TK == 0 and S % TB == 0
    topk = min(TOPK_MAX, S // 4)
    row = lambda v: v.reshape(1, -1).astype(F32)
    lane_pad = lambda v: jnp.pad(v.astype(F32), (0, LANES - v.shape[0])).reshape(1, LANES)

    w_all = _pack_w_in(w_in)
    wg = jnp.concatenate([_pair_block_diag(w_a), _pair_block_diag(w_x)], axis=2).astype(BF16)
    wuk2t = _pair_block_diag(jnp.swapaxes(w_uk, 1, 2)).astype(BF16)

    nq = S // TQ
    tile = lambda n: pl.BlockSpec((1, TM, n), lambda b, j: (b, j, 0))
    outs = pl.pallas_call(
        _proj_kernel,
        grid=(B, S // TM),
        in_specs=[
            tile(D_MODEL),
            _const_spec((1, D_MODEL)),
            _const_spec((D_MODEL, C_END)),
            _const_spec((N_HEADS // 2, 2 * HEAD_DIM, 2 * KV_LATENT)),
            _const_spec((1, KV_LATENT)),
            _const_spec((1, LANES)),
            _const_spec((1, LANES)),
            _const_spec((CONV_WIDTH, RNN_WIDTH)),
            _const_spec((1, RNN_WIDTH)),
            _const_spec((RNN_WIDTH // LANES, LANES, 2 * LANES)),
            _const_spec((1, RNN_WIDTH)),
            _const_spec((1, RNN_WIDTH)),
            _const_spec((1, RNN_WIDTH)),
            _const_spec((RNN_WIDTH, D_MODEL)),
            _const_spec((1, 2 * D_MODEL)),
        ],
        out_specs=[
            pl.BlockSpec((1, TM // TQ, KV_LATENT, N_HEADS * TQ), lambda b, j: (b, j, 0, 0)),
            tile(KV_LATENT),
            pl.BlockSpec((1, TM // TK, KV_LATENT, TK), lambda b, j: (b, j, 0, 0)),
            pl.BlockSpec((1, TM // TQ, IDX_DIM, IDX_HEADS * TQ), lambda b, j: (b, j, 0, 0)),
            tile(IDX_DIM),
            pl.BlockSpec((1, IDX_HEADS, TM), lambda b, j: (b, 0, j)),
            tile(ATTN_WIDTH),
            tile(D_MODEL),
            tile(D_MODEL),
        ],
        out_shape=[
            jax.ShapeDtypeStruct((B, nq, KV_LATENT, N_HEADS * TQ), BF16),
            jax.ShapeDtypeStruct((B, S, KV_LATENT), BF16),
            jax.ShapeDtypeStruct((B, S // TK, KV_LATENT, TK), BF16),
            jax.ShapeDtypeStruct((B, nq, IDX_DIM, IDX_HEADS * TQ), BF16),
            jax.ShapeDtypeStruct((B, S, IDX_DIM), BF16),
            jax.ShapeDtypeStruct((B, IDX_HEADS, S), F32),
            jax.ShapeDtypeStruct((B, S, ATTN_WIDTH), BF16),
            jax.ShapeDtypeStruct((B, S, D_MODEL), BF16),
            jax.ShapeDtypeStruct((B, S, D_MODEL), BF16),
        ],
        scratch_shapes=[
            pltpu.VMEM((TM + SUBLANES, RNN_WIDTH), F32),
            pltpu.VMEM((TM, RNN_WIDTH), F32),
            pltpu.VMEM((TM, RNN_WIDTH), F32),
            pltpu.VMEM((TM, RNN_WIDTH), F32),
            pltpu.VMEM((SUBLANES, RNN_WIDTH), F32),
        ],
        compiler_params=pltpu.CompilerParams(
            dimension_semantics=("arbitrary", "arbitrary"),
            vmem_limit_bytes=VMEM_LIMIT_BYTES),
        name="proj_rglru",
    )(x, row(norm_g), w_all, wuk2t, row(kv_g), lane_pad(ln_g), lane_pad(ln_b), conv_w.astype(F32),
      row(conv_b), wg, row(b_a), row(b_x), row(lam), w_rp.astype(BF16), row(b_merge))
    lq, c, ct, qit, ki, wt, sg, ga, r = outs

    wuv2 = _pair_block_diag(w_uv).astype(BF16)

    qtile = lambda n: pl.BlockSpec((1, TQ, n), lambda b, i: (b, i, 0))
    u = pl.pallas_call(
        functools.partial(_attn_kernel, topk),
        grid=(B, nq),
        in_specs=[
            pl.BlockSpec((1, 1, KV_LATENT, N_HEADS * TQ), lambda b, i: (b, i, 0, 0)),
            pl.BlockSpec((1, S, KV_LATENT), lambda b, i: (b, 0, 0)),
            pl.BlockSpec((1, S // TK, KV_LATENT, TK), lambda b, i: (b, 0, 0, 0)),
            pl.BlockSpec((1, 1, IDX_DIM, IDX_HEADS * TQ), lambda b, i: (b, i, 0, 0)),
            pl.BlockSpec((1, S, IDX_DIM), lambda b, i: (b, 0, 0)),
            pl.BlockSpec((1, IDX_HEADS, TQ), lambda b, i: (b, 0, i)),
            qtile(ATTN_WIDTH),
            _const_spec((N_HEADS // 2, 2 * KV_LATENT, 2 * HEAD_DIM)),
        ],
        out_specs=qtile(ATTN_WIDTH),
        out_shape=jax.ShapeDtypeStruct((B, S, ATTN_WIDTH), BF16),
        scratch_shapes=[
            pltpu.VMEM((2 * KV_LATENT, N_HEADS * TQ), BF16),
            pltpu.VMEM((S, 2 * KV_LATENT), BF16),
            pltpu.VMEM((S // TB, VROWS, TB), BF16),
            pltpu.VMEM((S, LANES), F32),
            pltpu.VMEM((S, TQ), F32),
            pltpu.VMEM((S, N_HEADS * TQ), F32),
            pltpu.VMEM((VROWS, N_HEADS * TQ), F32),
        ],
        compiler_params=pltpu.CompilerParams(
            dimension_semantics=("arbitrary", "arbitrary"),
            vmem_limit_bytes=VMEM_LIMIT_BYTES),
        name="sparse_attn",
    )(lq, c, ct, qit, ki, wt, sg, wuv2)

    return pl.pallas_call(
        _out_kernel,
        grid=(B, S // TM),
        in_specs=[
            tile(ATTN_WIDTH),
            tile(D_MODEL),
            tile(D_MODEL),
            tile(D_MODEL),
            _const_spec((ATTN_WIDTH, D_MODEL)),
            _const_spec((D_MODEL, D_MODEL)),
            _const_spec((1, D_MODEL)),
        ],
        out_specs=tile(D_MODEL),
        out_shape=jax.ShapeDtypeStruct((B, S, D_MODEL), F32),
        compiler_params=pltpu.CompilerParams(
            dimension_semantics=("arbitrary", "arbitrary"),
            vmem_limit_bytes=VMEM_LIMIT_BYTES),
        name="merge_out",
    )(u, ga, r, x, w_ap.astype(BF16), w_out.astype(BF16), row(out_g))


def kernel(x, norm_gain, w_in, b_merge, kv_norm_gain, w_uk, w_uv, idx_ln_gain, idx_ln_bias,
           w_attn_proj, conv_w, conv_b, w_rg_a, b_rg_a, w_rg_x, b_rg_x, lru_lambda,
           w_rnn_proj, w_out, final_norm_gain):
    assert norm_gain.shape[0] == 1, "only the stated depth-1 stack is supported"
    return _layer(x, norm_gain[0], w_in[0], b_merge[0], kv_norm_gain[0], w_uk[0], w_uv[0],
                  idx_ln_gain[0], idx_ln_bias[0], w_attn_proj[0], conv_w[0], conv_b[0],
                  w_rg_a[0], b_rg_a[0], w_rg_x[0], b_rg_x[0], lru_lambda[0],
                  w_rnn_proj[0], w_out[0], final_norm_gain)
```

```python
import functools

import numpy as np
import jax
import jax.numpy as jnp
from jax import lax
from jax.experimental import pallas as pl
from jax.experimental.pallas import tpu as pltpu

D_MODEL = 1024
N_HEADS = 16
HEAD_DIM = 64
ATTN_WIDTH = N_HEADS * HEAD_DIM
KV_LATENT = 128
IDX_HEADS = 8
IDX_DIM = 64
TOPK_MAX = 256
RNN_WIDTH = 1024
RNN_BLOCKS = 16
RNN_BLOCK_DIM = RNN_WIDTH // RNN_BLOCKS
CONV_WIDTH = 4
LRU_C = 8.0
NORM_EPS = 1e-6

F32 = jnp.float32
BF16 = jnp.bfloat16

LANES = 128
SUBLANES = 8
VMEM_LIMIT_BYTES = 56 * 1024 * 1024

TM = 512
TQ = 128
KC = 128
TK = 256
CC = 512
TB = 2 * TK
VROWS = 144
BISECT_FIXED = 18
BISECT_PER_CHECK = 2
BISECT_MAX_CHECKS = 9
LOG2E = 1.4426950408889634
F32_MIN_NORMAL = 2.0 ** -126
MASK_NEG = -(2.0 ** 100)
M_INIT = -1e20

C_Q = 0
C_C = C_Q + ATTN_WIDTH
C_QI = C_C + KV_LATENT
C_KW = C_QI + IDX_HEADS * IDX_DIM
C_GA = C_KW + LANES
C_XR = C_GA + ATTN_WIDTH
C_GR = C_XR + RNN_WIDTH
C_MG = C_GR + RNN_WIDTH
C_END = C_MG + 2 * D_MODEL

ALIBI_SLOPES = [float(2.0 ** (-8.0 * (h + 1) / N_HEADS)) for h in range(N_HEADS)]


def _sigmoid(v):
    return 0.5 * jnp.tanh(0.5 * v) + 0.5


def _f32_to_key(v):
    b = lax.bitcast_convert_type(v, jnp.int32)
    return b ^ ((b >> 31) & 0x7FFFFFFF)


def _key_to_f32(k):
    return lax.bitcast_convert_type(k ^ ((k >> 31) & 0x7FFFFFFF), F32)


def _proj_kernel(x_ref, ng_ref, w_ref, wuk_ref, kvg_ref, lng_ref, lnb_ref, cw_ref, cb_ref, wg_ref,
                 ba_ref, bx_ref, lam_ref, wrp_ref, bm_ref,
                 lq_ref, c_ref, ct_ref, qit_ref, ki_ref, wt_ref, sg_ref, ga_ref, r_ref,
                 xext_ref, a_ref, b_ref, h_ref, hc_ref):
    j = pl.program_id(1)

    @pl.when(j == 0)
    def _():
        xext_ref[0:SUBLANES, :] = jnp.zeros((SUBLANES, RNN_WIDTH), F32)
        hc_ref[...] = jnp.zeros((SUBLANES, RNN_WIDTH), F32)

    @pl.when(j > 0)
    def _():
        xext_ref[0:SUBLANES, :] = xext_ref[TM:TM + SUBLANES, :]

    x = x_ref[0]
    xn = x * lax.rsqrt(jnp.mean(x * x, axis=-1, keepdims=True) + NORM_EPS) * ng_ref[...]
    xb = xn.astype(BF16)

    def proj(lo, hi):
        return jnp.dot(xb, w_ref[:, lo:hi], preferred_element_type=F32)

    xext_ref[SUBLANES:SUBLANES + TM, :] = proj(C_XR, C_GR)

    def rnn_inputs(blk):
        cols = slice(blk * 2 * LANES, (blk + 1) * 2 * LANES)
        y = cb_ref[:, cols]
        for k in range(CONV_WIDTH):
            off = SUBLANES - (CONV_WIDTH - 1) + k
            y = y + cw_ref[k:k + 1, cols] * xext_ref[off:off + TM, cols]
        yb = y.astype(BF16)
        g = [jnp.dot(yb[:, h * LANES:(h + 1) * LANES], wg_ref[2 * blk + h],
                     preferred_element_type=F32) for h in range(2)]
        rg = _sigmoid(jnp.concatenate([g[0][:, :LANES], g[1][:, :LANES]], axis=1) + ba_ref[:, cols])
        ig = _sigmoid(jnp.concatenate([g[0][:, LANES:], g[1][:, LANES:]], axis=1) + bx_ref[:, cols])
        nl = -lam_ref[:, cols]
        softplus = jnp.maximum(nl, 0.0) + jnp.log1p(jnp.exp(-jnp.abs(nl)))
        log_a = (-LRU_C) * rg * softplus
        a_ref[:, cols] = jnp.exp(log_a)
        th = jnp.tanh(log_a)
        b_ref[:, cols] = jnp.sqrt(-2.0 * th / (1.0 - th)) * (ig * y)

    rnn_inputs(0)

    qb16 = proj(C_Q, C_C).astype(BF16)
    ql = [jnp.dot(qb16[:, p * LANES:(p + 1) * LANES], wuk_ref[p], preferred_element_type=F32)
          for p in range(N_HEADS // 2)]
    qlt = (jnp.concatenate(ql, axis=1) * (HEAD_DIM ** -0.5 * LOG2E)).T
    for qb in range(TM // TQ):
        for h in range(N_HEADS):
            lq_ref[0, qb, :, h * TQ:(h + 1) * TQ] = (
                qlt[h * KV_LATENT:(h + 1) * KV_LATENT, qb * TQ:(qb + 1) * TQ].astype(BF16))

    rnn_inputs(1)

    zc = proj(C_C, C_QI)
    cn = zc * lax.rsqrt(jnp.mean(zc * zc, axis=-1, keepdims=True) + NORM_EPS) * kvg_ref[...]
    c_ref[0] = cn.astype(BF16)
    cnt = cn.T
    for k in range(TM // TK):
        ct_ref[0, k] = cnt[:, k * TK:(k + 1) * TK].astype(BF16)

    zqt = (proj(C_QI, C_KW) * (IDX_DIM ** -0.5)).T
    for qb in range(TM // TQ):
        for jh in range(IDX_HEADS):
            qit_ref[0, qb, :, jh * TQ:(jh + 1) * TQ] = (
                zqt[jh * IDX_DIM:(jh + 1) * IDX_DIM, qb * TQ:(qb + 1) * TQ].astype(BF16))

    zk = proj(C_KW, C_GA)
    lane = lax.broadcasted_iota(jnp.int32, zk.shape, 1)
    is_k = lane < IDX_DIM
    mu = jnp.sum(jnp.where(is_k, zk, 0.0), axis=-1, keepdims=True) * (1.0 / IDX_DIM)
    kc = jnp.where(is_k, zk - mu, 0.0)
    var = jnp.sum(kc * kc, axis=-1, keepdims=True) * (1.0 / IDX_DIM)
    kin = kc * lax.rsqrt(var + NORM_EPS) * lng_ref[...] + lnb_ref[...]
    ki_ref[0] = kin[:, :IDX_DIM].astype(BF16)
    wt_ref[0] = zk.T[IDX_DIM:IDX_DIM + IDX_HEADS, :] * (IDX_HEADS ** -0.5)

    rnn_inputs(2)

    zg = proj(C_GA, C_XR)
    sg_ref[0] = (zg * _sigmoid(zg)).astype(BF16)

    rnn_inputs(3)

    row = lax.broadcasted_iota(jnp.int32, (SUBLANES, RNN_WIDTH), 0)

    def scan_chunk(ci, hprev):
        r0 = pl.multiple_of(ci * SUBLANES, SUBLANES)
        av = a_ref[pl.ds(r0, SUBLANES), :]
        bv = b_ref[pl.ds(r0, SUBLANES), :]
        for d in (1, 2, 4):
            a_sh = jnp.where(row >= d, pltpu.roll(av, d, 0), 1.0)
            b_sh = jnp.where(row >= d, pltpu.roll(bv, d, 0), 0.0)
            bv = av * b_sh + bv
            av = av * a_sh
        hv = av * hprev + bv
        h_ref[pl.ds(r0, SUBLANES), :] = hv
        return jnp.broadcast_to(hv[SUBLANES - 1:SUBLANES, :], (SUBLANES, RNN_WIDTH))

    hc_ref[...] = lax.fori_loop(0, TM // SUBLANES, scan_chunk, hc_ref[...])

    zr = proj(C_GR, C_MG)
    u = (h_ref[...] * (zr * _sigmoid(zr))).astype(BF16)
    y_rnn = jnp.dot(u, wrp_ref[...], preferred_element_type=F32)

    gates = _sigmoid(proj(C_MG, C_END) + bm_ref[...])
    ga_ref[0] = gates[:, :D_MODEL].astype(BF16)
    r_ref[0] = (gates[:, D_MODEL:] * y_rnn).astype(BF16)


def _attn_kernel(topk, lq_ref, c_ref, ct_ref, qit_ref, ki_ref, wt_ref, sg_ref, wuv_ref,
                 u_ref,
                 lhst_ref, kext_ref, vt_ref, pos_ref, sc_ref, s_ref, acc_ref):
    i = pl.program_id(1)
    seq = kext_ref.shape[0]
    n_kt = (i * TQ + TQ + TK - 1) // TK
    n_cc = (i * TQ + TQ + CC - 1) // CC

    @pl.when(jnp.logical_and(pl.program_id(0) == 0, i == 0))
    def _():
        er = lax.broadcasted_iota(jnp.int32, (TQ, TQ), 0)
        ec = lax.broadcasted_iota(jnp.int32, (TQ, TQ), 1)
        eye = jnp.where(er == ec, 1.0, 0.0).astype(BF16)
        for h in range(N_HEADS):
            lhst_ref[KV_LATENT:KV_LATENT + TQ, h * TQ:(h + 1) * TQ] = eye
        pos_ref[...] = lax.broadcasted_iota(jnp.int32, (seq, LANES), 0).astype(F32)
        rr = lax.broadcasted_iota(jnp.int32, (VROWS - KV_LATENT, TB), 0)
        for k in range(seq // TB):
            vt_ref[k, KV_LATENT:VROWS, :] = jnp.where(rr == 0, 1.0, 0.0).astype(BF16)

    @pl.when(i == 0)
    def _():
        kext_ref[:, 0:KV_LATENT] = c_ref[0]
        for k in range(seq // TK):
            vt_ref[k // (TB // TK), 0:KV_LATENT, (k % (TB // TK)) * TK:(k % (TB // TK) + 1) * TK] = (
                ct_ref[0, k])

    lhst_ref[0:KV_LATENT, :] = lq_ref[0, 0]

    qit = qit_ref[0, 0]
    wv = wt_ref[0]
    tpos = i * TQ + lax.broadcasted_iota(jnp.int32, (1, TQ), 1)

    def fold8(v, op):
        parts = [op(v[k * KC:(k + 1) * KC].reshape(KC // SUBLANES, SUBLANES, v.shape[1]), axis=0)
                 for k in range(v.shape[0] // KC)]
        return op(jnp.stack(parts), axis=0)

    n_big = n_kt // 2
    has_tail = n_kt % 2 == 1
    r_tail = pl.multiple_of(n_big * TB, TB)

    def score_block(r0, rows, carry):
        lo8, hi8 = carry
        prod = jnp.dot(ki_ref[0, pl.ds(r0, rows), :], qit, preferred_element_type=F32)
        sc = jnp.maximum(prod[:, 0:TQ], 0.0) * wv[0:1, :]
        for jh in range(1, IDX_HEADS):
            sc = sc + jnp.maximum(prod[:, jh * TQ:(jh + 1) * TQ], 0.0) * wv[jh:jh + 1, :]
        causal = (lax.broadcasted_iota(jnp.int32, (rows, TQ), 0) + r0) <= tpos
        sc_ref[pl.ds(r0, rows), :] = jnp.where(causal, sc, -jnp.inf)
        lo8 = jnp.minimum(lo8, fold8(jnp.where(causal, sc, jnp.inf), jnp.min))
        hi8 = jnp.maximum(hi8, fold8(jnp.where(causal, sc, -jnp.inf), jnp.max))
        return lo8, hi8

    ext = lax.fori_loop(
        0, n_big, lambda k, cr: score_block(pl.multiple_of(k * TB, TB), TB, cr),
        (jnp.full((SUBLANES, TQ), jnp.inf, F32), jnp.full((SUBLANES, TQ), -jnp.inf, F32)))
    lo8, hi8 = lax.cond(has_tail, lambda cr: score_block(r_tail, TK, cr), lambda cr: cr, ext)

    @pl.when(n_kt * TK < n_cc * CC)
    def _():
        r0 = pl.multiple_of(n_kt * TK, TK)
        sc_ref[pl.ds(r0, TK), :] = jnp.full((TK, TQ), -jnp.inf, F32)

    lo_k = _f32_to_key(jnp.min(lo8, axis=0, keepdims=True))
    vmax = jnp.max(hi8, axis=0, keepdims=True)
    hi_k = jnp.where(jnp.abs(vmax) < F32_MIN_NORMAL, _f32_to_key(jnp.full_like(vmax, F32_MIN_NORMAL)),
                     _f32_to_key(vmax) + 1)
    kf = float(topk)

    def count_ge(thr):
        def body(cc, c8):
            r0 = pl.multiple_of(cc * CC, CC)
            xs = sc_ref[pl.ds(r0, CC), :]
            return c8 + fold8(jnp.where(xs >= thr, 1.0, 0.0), jnp.sum)
        c8 = lax.fori_loop(0, n_cc, body, jnp.zeros((SUBLANES, TQ), F32))
        return jnp.sum(c8, axis=0, keepdims=True)

    def midpoint(lo_, hi_):
        return (lo_ >> 1) + (hi_ >> 1) + (lo_ & hi_ & 1)

    def bisect(carry):
        lo_, hi_, clo_ = carry
        mid = midpoint(lo_, hi_)
        cnt = count_ge(_key_to_f32(mid))
        ge = cnt >= kf
        return jnp.where(ge, mid, lo_), jnp.where(ge, hi_, mid), jnp.where(ge, cnt, clo_)

    def unresolved(carry):
        lo_, hi_, clo_ = carry
        open_ = jnp.where(clo_ > kf, jnp.where(midpoint(lo_, hi_) != lo_, 1.0, 0.0), 0.0)
        return (jnp.max(open_) > 0.5).astype(jnp.int32)

    n_causal = (tpos[0:1, :] + 1).astype(F32)
    n_fixed = jnp.where((i + 1) * TQ <= topk, 0, BISECT_FIXED)
    carry = lax.fori_loop(0, n_fixed, lambda _, cr: bisect(cr), (lo_k, hi_k, n_causal))

    def search_more(state):
        it, _, cr = state
        for _ in range(BISECT_PER_CHECK):
            cr = bisect(cr)
        return it + 1, unresolved(cr), cr

    _, _, (lo_k, hi_k, c_lo) = lax.while_loop(
        lambda st: jnp.logical_and(st[1] > 0, st[0] < BISECT_MAX_CHECKS),
        search_more, (jnp.int32(0), unresolved(carry), carry))
    lo = _key_to_f32(lo_k)
    hi = _key_to_f32(hi_k)

    has_ties = jnp.max(jnp.where(c_lo > kf, 1.0, 0.0)) > 0.5

    @pl.when(jnp.logical_not(has_ties))
    def _():
        def mask_chunk(kt, _):
            r0 = pl.multiple_of(kt * TK, TK)
            xs = sc_ref[pl.ds(r0, TK), :]
            kext_ref[pl.ds(r0, TK), KV_LATENT:KV_LATENT + TQ] = (
                jnp.where(xs >= lo, 0.0, MASK_NEG).astype(BF16))
            return 0
        lax.fori_loop(0, n_kt, mask_chunk, 0)

    @pl.when(has_ties)
    def _():
        need = kf - count_ge(hi)
        tri = jnp.where(lax.broadcasted_iota(jnp.int32, (KC, KC), 1)
                        <= lax.broadcasted_iota(jnp.int32, (KC, KC), 0), 1.0, 0.0).astype(BF16)

        def mask_chunk(kc, before):
            r0 = pl.multiple_of(kc * KC, KC)
            xs = sc_ref[pl.ds(r0, KC), :]
            band = jnp.where(xs >= lo, jnp.where(xs < hi, 1.0, 0.0), 0.0)
            incl = jnp.dot(tri, band.astype(BF16), preferred_element_type=F32)
            rank = incl - band + before
            keep = jnp.where(xs >= hi, 1.0, jnp.where(rank < need, band, 0.0))
            kext_ref[pl.ds(r0, KC), KV_LATENT:KV_LATENT + TQ] = (
                jnp.where(keep > 0.0, 0.0, MASK_NEG).astype(BF16))
            return before + incl[KC - 1:KC, :]

        lax.fori_loop(0, (TK // KC) * n_kt, mask_chunk, jnp.zeros((1, TQ), F32))


    def qk_block(r0, rows, mx):
        st = jnp.dot(kext_ref[pl.ds(r0, rows), :], lhst_ref[...], preferred_element_type=F32)
        posb = pos_ref[pl.ds(r0, rows), :]
        tops = []
        for h in range(N_HEADS):
            sh = st[:, h * TQ:(h + 1) * TQ] + posb * (ALIBI_SLOPES[h] * LOG2E)
            s_ref[pl.ds(r0, rows), h * TQ:(h + 1) * TQ] = sh
            tops.append(fold8(sh, jnp.max))
        return jnp.maximum(mx, jnp.concatenate(tops, axis=1))

    mx = lax.fori_loop(0, n_big, lambda k, v: qk_block(pl.multiple_of(k * TB, TB), TB, v),
                       jnp.full((SUBLANES, N_HEADS * TQ), M_INIT, F32))
    mx = lax.cond(has_tail, lambda v: qk_block(r_tail, TK, v), lambda v: v, mx)
    m = jnp.max(mx, axis=0, keepdims=True)
    acc_ref[...] = jnp.zeros(acc_ref.shape, F32)

    def pv_block(r0, rows, vt):
        pt = jnp.exp2(s_ref[pl.ds(r0, rows), :] - m).astype(BF16)
        acc_ref[...] += jnp.dot(vt, pt, preferred_element_type=F32)

    def pv_big(k, _):
        pv_block(pl.multiple_of(k * TB, TB), TB, vt_ref[k])
        return 0

    lax.fori_loop(0, n_big, pv_big, 0)

    @pl.when(has_tail)
    def _():
        pv_block(r_tail, TK, vt_ref[n_big, :, 0:TK])

    o_parts = []
    for p in range(N_HEADS // 2):
        cols = slice(p * 2 * TQ, (p + 1) * 2 * TQ)
        xt = (acc_ref[0:KV_LATENT, cols] / acc_ref[KV_LATENT:KV_LATENT + 1, cols]).T
        pair = jnp.concatenate([xt[:TQ], xt[TQ:]], axis=1).astype(BF16)
        o_parts.append(jnp.dot(pair, wuv_ref[p], preferred_element_type=F32))
    o = jnp.concatenate(o_parts, axis=1)
    u_ref[0] = (o * sg_ref[0].astype(F32)).astype(BF16)


def _out_kernel(u_ref, ga_ref, r_ref, x_ref, wap_ref, wout_ref, fg_ref, out_ref):
    y_attn = jnp.dot(u_ref[0], wap_ref[...], preferred_element_type=F32)
    mixed = (ga_ref[0].astype(F32) * y_attn + r_ref[0].astype(F32)).astype(BF16)
    res = x_ref[0] + jnp.dot(mixed, wout_ref[...], preferred_element_type=F32)
    out_ref[0] = res * lax.rsqrt(jnp.mean(res * res, axis=-1, keepdims=True) + NORM_EPS) * fg_ref[...]


def _const_spec(shape):
    nd = len(shape)
    return pl.BlockSpec(shape, lambda b, i: (0,) * nd, pipeline_mode=pl.Buffered(1))


def _pack_w_in(w_in):
    offs = np.cumsum([0, ATTN_WIDTH, KV_LATENT, IDX_HEADS * IDX_DIM, IDX_DIM, IDX_HEADS,
                      ATTN_WIDTH, RNN_WIDTH, RNN_WIDTH, 2 * D_MODEL])
    seg = [w_in[:, offs[k]:offs[k + 1]] for k in range(9)]
    pad = jnp.zeros((D_MODEL, LANES - IDX_DIM - IDX_HEADS), w_in.dtype)
    return jnp.concatenate(seg[:5] + [pad] + seg[5:], axis=1).astype(BF16)


def _pair_block_diag(w):
    n2, r, c = w.shape
    w = w.reshape(n2 // 2, 2, r, c)
    z = jnp.zeros((n2 // 2, r, c), w.dtype)
    top = jnp.concatenate([w[:, 0], z], axis=2)
    bot = jnp.concatenate([z, w[:, 1]], axis=2)
    return jnp.concatenate([top, bot], axis=1)


def _layer(x, norm_g, w_in, b_merge, kv_g, w_uk, w_uv, ln_g, ln_b, w_ap, conv_w, conv_b,
           w_a, b_a, w_x, b_x, lam, w_rp, w_out, out_g):
    B, S, _ = x.shape
    assert S % TM == 0 and TM % TK == 0 and TK % TQ == 0 and TQ == KC
    assert S % CC == 0 and CC % TK == 0 and S % TB == 0
    topk = min(TOPK_MAX, S // 4)
    row = lambda v: v.reshape(1, -1).astype(F32)
    lane_pad = lambda v: jnp.pad(v.astype(F32), (0, LANES - v.shape[0])).reshape(1, LANES)

    w_all = _pack_w_in(w_in)
    wg = jnp.concatenate([_pair_block_diag(w_a), _pair_block_diag(w_x)], axis=2).astype(BF16)
    wuk2t = _pair_block_diag(jnp.swapaxes(w_uk, 1, 2)).astype(BF16)

    nq = S // TQ
    tile = lambda n: pl.BlockSpec((1, TM, n), lambda b, j: (b, j, 0))
    outs = pl.pallas_call(
        _proj_kernel,
        grid=(B, S // TM),
        in_specs=[
            tile(D_MODEL),
            _const_spec((1, D_MODEL)),
            _const_spec((D_MODEL, C_END)),
            _const_spec((N_HEADS // 2, 2 * HEAD_DIM, 2 * KV_LATENT)),
            _const_spec((1, KV_LATENT)),
            _const_spec((1, LANES)),
            _const_spec((1, LANES)),
            _const_spec((CONV_WIDTH, RNN_WIDTH)),
            _const_spec((1, RNN_WIDTH)),
            _const_spec((RNN_WIDTH // LANES, LANES, 2 * LANES)),
            _const_spec((1, RNN_WIDTH)),
            _const_spec((1, RNN_WIDTH)),
            _const_spec((1, RNN_WIDTH)),
            _const_spec((RNN_WIDTH, D_MODEL)),
            _const_spec((1, 2 * D_MODEL)),
        ],
        out_specs=[
            pl.BlockSpec((1, TM // TQ, KV_LATENT, N_HEADS * TQ), lambda b, j: (b, j, 0, 0)),
            tile(KV_LATENT),
            pl.BlockSpec((1, TM // TK, KV_LATENT, TK), lambda b, j: (b, j, 0, 0)),
            pl.BlockSpec((1, TM // TQ, IDX_DIM, IDX_HEADS * TQ), lambda b, j: (b, j, 0, 0)),
            tile(IDX_DIM),
            pl.BlockSpec((1, IDX_HEADS, TM), lambda b, j: (b, 0, j)),
            tile(ATTN_WIDTH),
            tile(D_MODEL),
            tile(D_MODEL),
        ],
        out_shape=[
            jax.ShapeDtypeStruct((B, nq, KV_LATENT, N_HEADS * TQ), BF16),
            jax.ShapeDtypeStruct((B, S, KV_LATENT), BF16),
            jax.ShapeDtypeStruct((B, S // TK, KV_LATENT, TK), BF16),
            jax.ShapeDtypeStruct((B, nq, IDX_DIM, IDX_HEADS * TQ), BF16),
            jax.ShapeDtypeStruct((B, S, IDX_DIM), BF16),
            jax.ShapeDtypeStruct((B, IDX_HEADS, S), F32),
            jax.ShapeDtypeStruct((B, S, ATTN_WIDTH), BF16),
            jax.ShapeDtypeStruct((B, S, D_MODEL), BF16),
            jax.ShapeDtypeStruct((B, S, D_MODEL), BF16),
        ],
        scratch_shapes=[
            pltpu.VMEM((TM + SUBLANES, RNN_WIDTH), F32),
            pltpu.VMEM((TM, RNN_WIDTH), F32),
            pltpu.VMEM((TM, RNN_WIDTH), F32),
            pltpu.VMEM((TM, RNN_WIDTH), F32),
            pltpu.VMEM((SUBLANES, RNN_WIDTH), F32),
        ],
        compiler_params=pltpu.CompilerParams(
            dimension_semantics=("arbitrary", "arbitrary"),
            vmem_limit_bytes=VMEM_LIMIT_BYTES),
        name="proj_rglru",
    )(x, row(norm_g), w_all, wuk2t, row(kv_g), lane_pad(ln_g), lane_pad(ln_b), conv_w.astype(F32),
      row(conv_b), wg, row(b_a), row(b_x), row(lam), w_rp.astype(BF16), row(b_merge))
    lq, c, ct, qit, ki, wt, sg, ga, r = outs

    wuv2 = _pair_block_diag(w_uv).astype(BF16)

    qtile = lambda n: pl.BlockSpec((1, TQ, n), lambda b, i: (b, i, 0))
    u = pl.pallas_call(
        functools.partial(_attn_kernel, topk),
        grid=(B, nq),
        in_specs=[
            pl.BlockSpec((1, 1, KV_LATENT, N_HEADS * TQ), lambda b, i: (b, i, 0, 0)),
            pl.BlockSpec((1, S, KV_LATENT), lambda b, i: (b, 0, 0)),
            pl.BlockSpec((1, S // TK, KV_LATENT, TK), lambda b, i: (b, 0, 0, 0)),
            pl.BlockSpec((1, 1, IDX_DIM, IDX_HEADS * TQ), lambda b, i: (b, i, 0, 0)),
            pl.BlockSpec((1, S, IDX_DIM), lambda b, i: (b, 0, 0)),
            pl.BlockSpec((1, IDX_HEADS, TQ), lambda b, i: (b, 0, i)),
            qtile(ATTN_WIDTH),
            _const_spec((N_HEADS // 2, 2 * KV_LATENT, 2 * HEAD_DIM)),
        ],
        out_specs=qtile(ATTN_WIDTH),
        out_shape=jax.ShapeDtypeStruct((B, S, ATTN_WIDTH), BF16),
        scratch_shapes=[
            pltpu.VMEM((2 * KV_LATENT, N_HEADS * TQ), BF16),
            pltpu.VMEM((S, 2 * KV_LATENT), BF16),
            pltpu.VMEM((S // TB, VROWS, TB), BF16),
            pltpu.VMEM((S, LANES), F32),
            pltpu.VMEM((S, TQ), F32),
            pltpu.VMEM((S, N_HEADS * TQ), F32),
            pltpu.VMEM((VROWS, N_HEADS * TQ), F32),
        ],
        compiler_params=pltpu.CompilerParams(
            dimension_semantics=("arbitrary", "arbitrary"),
            vmem_limit_bytes=VMEM_LIMIT_BYTES),
        name="sparse_attn",
    )(lq, c, ct, qit, ki, wt, sg, wuv2)

    return pl.pallas_call(
        _out_kernel,
        grid=(B, S // TM),
        in_specs=[
            tile(ATTN_WIDTH),
            tile(D_MODEL),
            tile(D_MODEL),
            tile(D_MODEL),
            _const_spec((ATTN_WIDTH, D_MODEL)),
            _const_spec((D_MODEL, D_MODEL)),
            _const_spec((1, D_MODEL)),
        ],
        out_specs=tile(D_MODEL),
        out_shape=jax.ShapeDtypeStruct((B, S, D_MODEL), F32),
        compiler_params=pltpu.CompilerParams(
            dimension_semantics=("arbitrary", "arbitrary"),
            vmem_limit_bytes=VMEM_LIMIT_BYTES),
        name="merge_out",
    )(u, ga, r, x, w_ap.astype(BF16), w_out.astype(BF16), row(out_g))


def kernel(x, norm_gain, w_in, b_merge, kv_norm_gain, w_uk, w_uv, idx_ln_gain, idx_ln_bias,
           w_attn_proj, conv_w, conv_b, w_rg_a, b_rg_a, w_rg_x, b_rg_x, lru_lambda,
           w_rnn_proj, w_out, final_norm_gain):
    assert norm_gain.shape[0] == 1, "only the stated depth-1 stack is supported"
    return _layer(x, norm_gain[0], w_in[0], b_merge[0], kv_norm_gain[0], w_uk[0], w_uv[0],
                  idx_ln_gain[0], idx_ln_bias[0], w_attn_proj[0], conv_w[0], conv_b[0],
                  w_rg_a[0], b_rg_a[0], w_rg_x[0], b_rg_x[0], lru_lambda[0],
                  w_rnn_proj[0], w_out[0], final_norm_gain)
```

```python
import functools

import numpy as np
import jax
import jax.numpy as jnp
from jax import lax
from jax.experimental import pallas as pl
from jax.experimental.pallas import tpu as pltpu

D_MODEL = 1024
N_HEADS = 16
HEAD_DIM = 64
ATTN_WIDTH = N_HEADS * HEAD_DIM
KV_LATENT = 128
IDX_HEADS = 8
IDX_DIM = 64
TOPK_MAX = 256
RNN_WIDTH = 1024
RNN_BLOCKS = 16
RNN_BLOCK_DIM = RNN_WIDTH // RNN_BLOCKS
CONV_WIDTH = 4
LRU_C = 8.0
NORM_EPS = 1e-6

F32 = jnp.float32
BF16 = jnp.bfloat16

LANES = 128
SUBLANES = 8
VMEM_LIMIT_BYTES = 56 * 1024 * 1024

TM = 512
TQ = 128
KC = 128
TK = 256
CC = 512
TB = 2 * TK
VROWS = 144
BISECT_FIXED = 18
BISECT_PER_CHECK = 2
BISECT_MAX_CHECKS = 9
LOG2E = 1.4426950408889634
F32_MIN_NORMAL = 2.0 ** -126
MASK_NEG = -(2.0 ** 100)
M_INIT = -1e20

C_Q = 0
C_C = C_Q + ATTN_WIDTH
C_QI = C_C + KV_LATENT
C_KW = C_QI + IDX_HEADS * IDX_DIM
C_GA = C_KW + LANES
C_XR = C_GA + ATTN_WIDTH
C_GR = C_XR + RNN_WIDTH
C_MG = C_GR + RNN_WIDTH
C_END = C_MG + 2 * D_MODEL

ALIBI_SLOPES = [float(2.0 ** (-8.0 * (h + 1) / N_HEADS)) for h in range(N_HEADS)]


def _sigmoid(v):
    return 0.5 * jnp.tanh(0.5 * v) + 0.5


def _f32_to_key(v):
    b = lax.bitcast_convert_type(v, jnp.int32)
    return b ^ ((b >> 31) & 0x7FFFFFFF)


def _key_to_f32(k):
    return lax.bitcast_convert_type(k ^ ((k >> 31) & 0x7FFFFFFF), F32)


def _proj_kernel(x_ref, ng_ref, w_ref, wuk_ref, kvg_ref, lng_ref, lnb_ref, cw_ref, cb_ref, wg_ref,
                 ba_ref, bx_ref, lam_ref, wrp_ref, bm_ref,
                 lq_ref, c_ref, ct_ref, qit_ref, ki_ref, wt_ref, sg_ref, ga_ref, r_ref,
                 xext_ref, a_ref, b_ref, h_ref, hc_ref, szr_ref, gr_ref):
    j = pl.program_id(1)

    @pl.when(j == 0)
    def _():
        xext_ref[0:SUBLANES, :] = jnp.zeros((SUBLANES, RNN_WIDTH), F32)
        hc_ref[...] = jnp.zeros((SUBLANES, RNN_WIDTH), F32)

    @pl.when(j > 0)
    def _():
        xext_ref[0:SUBLANES, :] = xext_ref[TM:TM + SUBLANES, :]

    x = x_ref[0]
    xn = x * lax.rsqrt(jnp.mean(x * x, axis=-1, keepdims=True) + NORM_EPS) * ng_ref[...]
    xb = xn.astype(BF16)

    def proj(lo, hi):
        return jnp.dot(xb, w_ref[:, lo:hi], preferred_element_type=F32)

    xext_ref[SUBLANES:SUBLANES + TM, :] = proj(C_XR, C_GR)

    def rnn_inputs(blk):
        cols = slice(blk * 2 * LANES, (blk + 1) * 2 * LANES)
        y = cb_ref[:, cols]
        for k in range(CONV_WIDTH):
            off = SUBLANES - (CONV_WIDTH - 1) + k
            y = y + cw_ref[k:k + 1, cols] * xext_ref[off:off + TM, cols]
        yb = y.astype(BF16)
        g = [jnp.dot(yb[:, h * LANES:(h + 1) * LANES], wg_ref[2 * blk + h],
                     preferred_element_type=F32) for h in range(2)]
        rg = _sigmoid(jnp.concatenate([g[0][:, :LANES], g[1][:, :LANES]], axis=1) + ba_ref[:, cols])
        ig = _sigmoid(jnp.concatenate([g[0][:, LANES:], g[1][:, LANES:]], axis=1) + bx_ref[:, cols])
        nl = -lam_ref[:, cols]
        softplus = jnp.maximum(nl, 0.0) + jnp.log1p(jnp.exp(-jnp.abs(nl)))
        log_a = (-LRU_C) * rg * softplus
        a_ref[:, cols] = jnp.exp(log_a)
        th = jnp.tanh(log_a)
        b_ref[:, cols] = jnp.sqrt(-2.0 * th / (1.0 - th)) * (ig * y)

    def attn_queries():
        qb16 = proj(C_Q, C_C).astype(BF16)
        ql = [jnp.dot(qb16[:, p * LANES:(p + 1) * LANES], wuk_ref[p], preferred_element_type=F32)
              for p in range(N_HEADS // 2)]
        qlt = (jnp.concatenate(ql, axis=1) * (HEAD_DIM ** -0.5 * LOG2E)).T
        for qb in range(TM // TQ):
            for h in range(N_HEADS):
                lq_ref[0, qb, :, h * TQ:(h + 1) * TQ] = (
                    qlt[h * KV_LATENT:(h + 1) * KV_LATENT, qb * TQ:(qb + 1) * TQ].astype(BF16))

    def attn_keys_and_indexer():
        zc = proj(C_C, C_QI)
        cn = zc * lax.rsqrt(jnp.mean(zc * zc, axis=-1, keepdims=True) + NORM_EPS) * kvg_ref[...]
        c_ref[0] = cn.astype(BF16)
        cnt = cn.T
        for k in range(TM // TK):
            ct_ref[0, k] = cnt[:, k * TK:(k + 1) * TK].astype(BF16)

        zqt = (proj(C_QI, C_KW) * (IDX_DIM ** -0.5)).T
        for qb in range(TM // TQ):
            for jh in range(IDX_HEADS):
                qit_ref[0, qb, :, jh * TQ:(jh + 1) * TQ] = (
                    zqt[jh * IDX_DIM:(jh + 1) * IDX_DIM, qb * TQ:(qb + 1) * TQ].astype(BF16))

        zk = proj(C_KW, C_GA)
        lane = lax.broadcasted_iota(jnp.int32, zk.shape, 1)
        is_k = lane < IDX_DIM
        mu = jnp.sum(jnp.where(is_k, zk, 0.0), axis=-1, keepdims=True) * (1.0 / IDX_DIM)
        kc = jnp.where(is_k, zk - mu, 0.0)
        var = jnp.sum(kc * kc, axis=-1, keepdims=True) * (1.0 / IDX_DIM)
        kin = kc * lax.rsqrt(var + NORM_EPS) * lng_ref[...] + lnb_ref[...]
        ki_ref[0] = kin[:, :IDX_DIM].astype(BF16)
        wt_ref[0] = zk.T[IDX_DIM:IDX_DIM + IDX_HEADS, :] * (IDX_HEADS ** -0.5)

    def attn_gate():
        zg = proj(C_GA, C_XR)
        sg_ref[0] = (zg * _sigmoid(zg)).astype(BF16)

    rnn_inputs(0)
    ga_ref[0] = _sigmoid(proj(C_MG, C_MG + D_MODEL) + bm_ref[:, :D_MODEL]).astype(BF16)
    rnn_inputs(1)
    gr_ref[...] = _sigmoid(proj(C_MG + D_MODEL, C_END) + bm_ref[:, D_MODEL:])
    rnn_inputs(2)
    zr = proj(C_GR, C_MG)
    szr_ref[...] = zr * _sigmoid(zr)
    rnn_inputs(3)

    row = lax.broadcasted_iota(jnp.int32, (SUBLANES, RNN_WIDTH), 0)

    hprev = hc_ref[...]
    for ci in range(TM // SUBLANES):
        rows = slice(ci * SUBLANES, (ci + 1) * SUBLANES)
        av = a_ref[rows, :]
        bv = b_ref[rows, :]
        for d in (1, 2, 4):
            a_sh = jnp.where(row >= d, pltpu.roll(av, d, 0), 1.0)
            b_sh = jnp.where(row >= d, pltpu.roll(bv, d, 0), 0.0)
            bv = av * b_sh + bv
            av = av * a_sh
        hv = av * hprev + bv
        h_ref[rows, :] = hv
        hprev = jnp.broadcast_to(hv[SUBLANES - 1:SUBLANES, :], (SUBLANES, RNN_WIDTH))
    hc_ref[...] = hprev

    attn_queries()
    attn_keys_and_indexer()
    attn_gate()

    u = (h_ref[...] * szr_ref[...]).astype(BF16)
    y_rnn = jnp.dot(u, wrp_ref[...], preferred_element_type=F32)
    r_ref[0] = (gr_ref[...] * y_rnn).astype(BF16)


def _attn_kernel(topk, lq_ref, c_ref, ct_ref, qit_ref, ki_ref, wt_ref, sg_ref, wuv_ref,
                 u_ref,
                 lhst_ref, kext_ref, vt_ref, pos_ref, sc_ref, s_ref, acc_ref):
    i = pl.program_id(1)
    seq = kext_ref.shape[0]
    n_kt = (i * TQ + TQ + TK - 1) // TK
    n_cc = (i * TQ + TQ + CC - 1) // CC

    @pl.when(jnp.logical_and(pl.program_id(0) == 0, i == 0))
    def _():
        er = lax.broadcasted_iota(jnp.int32, (TQ, TQ), 0)
        ec = lax.broadcasted_iota(jnp.int32, (TQ, TQ), 1)
        eye = jnp.where(er == ec, 1.0, 0.0).astype(BF16)
        for h in range(N_HEADS):
            lhst_ref[KV_LATENT:KV_LATENT + TQ, h * TQ:(h + 1) * TQ] = eye
        pos_ref[...] = lax.broadcasted_iota(jnp.int32, (seq, LANES), 0).astype(F32)
        rr = lax.broadcasted_iota(jnp.int32, (VROWS - KV_LATENT, TB), 0)
        for k in range(seq // TB):
            vt_ref[k, KV_LATENT:VROWS, :] = jnp.where(rr == 0, 1.0, 0.0).astype(BF16)

    @pl.when(i == 0)
    def _():
        kext_ref[:, 0:KV_LATENT] = c_ref[0]
        for k in range(seq // TK):
            vt_ref[k // (TB // TK), 0:KV_LATENT, (k % (TB // TK)) * TK:(k % (TB // TK) + 1) * TK] = (
                ct_ref[0, k])

    lhst_ref[0:KV_LATENT, :] = lq_ref[0, 0]

    qit = qit_ref[0, 0]
    wv = wt_ref[0]
    tpos = i * TQ + lax.broadcasted_iota(jnp.int32, (1, TQ), 1)

    def fold8(v, op):
        parts = [op(v[k * KC:(k + 1) * KC].reshape(KC // SUBLANES, SUBLANES, v.shape[1]), axis=0)
                 for k in range(v.shape[0] // KC)]
        return op(jnp.stack(parts), axis=0)

    n_big = n_kt // 2
    has_tail = n_kt % 2 == 1
    r_tail = pl.multiple_of(n_big * TB, TB)

    def score_block(r0, rows, carry):
        lo8, hi8 = carry
        prod = jnp.dot(ki_ref[0, pl.ds(r0, rows), :], qit, preferred_element_type=F32)
        sc = jnp.maximum(prod[:, 0:TQ], 0.0) * wv[0:1, :]
        for jh in range(1, IDX_HEADS):
            sc = sc + jnp.maximum(prod[:, jh * TQ:(jh + 1) * TQ], 0.0) * wv[jh:jh + 1, :]
        causal = (lax.broadcasted_iota(jnp.int32, (rows, TQ), 0) + r0) <= tpos
        sc_ref[pl.ds(r0, rows), :] = jnp.where(causal, sc, -jnp.inf)
        lo8 = jnp.minimum(lo8, fold8(jnp.where(causal, sc, jnp.inf), jnp.min))
        hi8 = jnp.maximum(hi8, fold8(jnp.where(causal, sc, -jnp.inf), jnp.max))
        return lo8, hi8

    ext = lax.fori_loop(
        0, n_big, lambda k, cr: score_block(pl.multiple_of(k * TB, TB), TB, cr),
        (jnp.full((SUBLANES, TQ), jnp.inf, F32), jnp.full((SUBLANES, TQ), -jnp.inf, F32)))
    lo8, hi8 = lax.cond(has_tail, lambda cr: score_block(r_tail, TK, cr), lambda cr: cr, ext)

    @pl.when(n_kt * TK < n_cc * CC)
    def _():
        r0 = pl.multiple_of(n_kt * TK, TK)
        sc_ref[pl.ds(r0, TK), :] = jnp.full((TK, TQ), -jnp.inf, F32)

    lo_k = _f32_to_key(jnp.min(lo8, axis=0, keepdims=True))
    vmax = jnp.max(hi8, axis=0, keepdims=True)
    hi_k = jnp.where(jnp.abs(vmax) < F32_MIN_NORMAL, _f32_to_key(jnp.full_like(vmax, F32_MIN_NORMAL)),
                     _f32_to_key(vmax) + 1)
    kf = float(topk)

    def count_ge(thr):
        def body(cc, c8):
            r0 = pl.multiple_of(cc * CC, CC)
            xs = sc_ref[pl.ds(r0, CC), :]
            return c8 + fold8(jnp.where(xs >= thr, 1.0, 0.0), jnp.sum)
        c8 = lax.fori_loop(0, n_cc, body, jnp.zeros((SUBLANES, TQ), F32))
        return jnp.sum(c8, axis=0, keepdims=True)

    def midpoint(lo_, hi_):
        return (lo_ >> 1) + (hi_ >> 1) + (lo_ & hi_ & 1)

    def bisect(carry):
        lo_, hi_, clo_ = carry
        mid = midpoint(lo_, hi_)
        cnt = count_ge(_key_to_f32(mid))
        ge = cnt >= kf
        return jnp.where(ge, mid, lo_), jnp.where(ge, hi_, mid), jnp.where(ge, cnt, clo_)

    def unresolved(carry):
        lo_, hi_, clo_ = carry
        open_ = jnp.where(clo_ > kf, jnp.where(midpoint(lo_, hi_) != lo_, 1.0, 0.0), 0.0)
        return (jnp.max(open_) > 0.5).astype(jnp.int32)

    n_causal = (tpos[0:1, :] + 1).astype(F32)
    n_fixed = jnp.where((i + 1) * TQ <= topk, 0, BISECT_FIXED)
    carry = lax.fori_loop(0, n_fixed, lambda _, cr: bisect(cr), (lo_k, hi_k, n_causal))

    def search_more(state):
        it, _, cr = state
        for _ in range(BISECT_PER_CHECK):
            cr = bisect(cr)
        return it + 1, unresolved(cr), cr

    _, _, (lo_k, hi_k, c_lo) = lax.while_loop(
        lambda st: jnp.logical_and(st[1] > 0, st[0] < BISECT_MAX_CHECKS),
        search_more, (jnp.int32(0), unresolved(carry), carry))
    lo = _key_to_f32(lo_k)
    hi = _key_to_f32(hi_k)

    has_ties = jnp.max(jnp.where(c_lo > kf, 1.0, 0.0)) > 0.5

    @pl.when(jnp.logical_not(has_ties))
    def _():
        def mask_chunk(kt, _):
            r0 = pl.multiple_of(kt * TK, TK)
            xs = sc_ref[pl.ds(r0, TK), :]
            kext_ref[pl.ds(r0, TK), KV_LATENT:KV_LATENT + TQ] = (
                jnp.where(xs >= lo, 0.0, MASK_NEG).astype(BF16))
            return 0
        lax.fori_loop(0, n_kt, mask_chunk, 0)

    @pl.when(has_ties)
    def _():
        need = kf - count_ge(hi)
        tri = jnp.where(lax.broadcasted_iota(jnp.int32, (KC, KC), 1)
                        <= lax.broadcasted_iota(jnp.int32, (KC, KC), 0), 1.0, 0.0).astype(BF16)

        def mask_chunk(kc, before):
            r0 = pl.multiple_of(kc * KC, KC)
            xs = sc_ref[pl.ds(r0, KC), :]
            band = jnp.where(xs >= lo, jnp.where(xs < hi, 1.0, 0.0), 0.0)
            incl = jnp.dot(tri, band.astype(BF16), preferred_element_type=F32)
            rank = incl - band + before
            keep = jnp.where(xs >= hi, 1.0, jnp.where(rank < need, band, 0.0))
            kext_ref[pl.ds(r0, KC), KV_LATENT:KV_LATENT + TQ] = (
                jnp.where(keep > 0.0, 0.0, MASK_NEG).astype(BF16))
            return before + incl[KC - 1:KC, :]

        lax.fori_loop(0, (TK // KC) * n_kt, mask_chunk, jnp.zeros((1, TQ), F32))


    def qk_block(r0, rows, mx):
        st = jnp.dot(kext_ref[pl.ds(r0, rows), :], lhst_ref[...], preferred_element_type=F32)
        posb = pos_ref[pl.ds(r0, rows), :]
        tops = []
        for h in range(N_HEADS):
            sh = st[:, h * TQ:(h + 1) * TQ] + posb * (ALIBI_SLOPES[h] * LOG2E)
            s_ref[pl.ds(r0, rows), h * TQ:(h + 1) * TQ] = sh
            tops.append(fold8(sh, jnp.max))
        return jnp.maximum(mx, jnp.concatenate(tops, axis=1))

    mx = lax.fori_loop(0, n_big, lambda k, v: qk_block(pl.multiple_of(k * TB, TB), TB, v),
                       jnp.full((SUBLANES, N_HEADS * TQ), M_INIT, F32))
    mx = lax.cond(has_tail, lambda v: qk_block(r_tail, TK, v), lambda v: v, mx)
    m = jnp.max(mx, axis=0, keepdims=True)
    acc_ref[...] = jnp.zeros(acc_ref.shape, F32)

    def pv_block(r0, rows, vt):
        pt = jnp.exp2(s_ref[pl.ds(r0, rows), :] - m).astype(BF16)
        acc_ref[...] += jnp.dot(vt, pt, preferred_element_type=F32)

    def pv_big(k, _):
        pv_block(pl.multiple_of(k * TB, TB), TB, vt_ref[k])
        return 0

    lax.fori_loop(0, n_big, pv_big, 0)

    @pl.when(has_tail)
    def _():
        pv_block(r_tail, TK, vt_ref[n_big, :, 0:TK])

    o_parts = []
    for p in range(N_HEADS // 2):
        cols = slice(p * 2 * TQ, (p + 1) * 2 * TQ)
        xt = (acc_ref[0:KV_LATENT, cols] / acc_ref[KV_LATENT:KV_LATENT + 1, cols]).T
        pair = jnp.concatenate([xt[:TQ], xt[TQ:]], axis=1).astype(BF16)
        o_parts.append(jnp.dot(pair, wuv_ref[p], preferred_element_type=F32))
    o = jnp.concatenate(o_parts, axis=1)
    u_ref[0] = (o * sg_ref[0].astype(F32)).astype(BF16)


def _out_kernel(u_ref, ga_ref, r_ref, x_ref, wap_ref, wout_ref, fg_ref, out_ref):
    y_attn = jnp.dot(u_ref[0], wap_ref[...], preferred_element_type=F32)
    mixed = (ga_ref[0].astype(F32) * y_attn + r_ref[0].astype(F32)).astype(BF16)
    res = x_ref[0] + jnp.dot(mixed, wout_ref[...], preferred_element_type=F32)
    out_ref[0] = res * lax.rsqrt(jnp.mean(res * res, axis=-1, keepdims=True) + NORM_EPS) * fg_ref[...]


def _const_spec(shape):
    nd = len(shape)
    return pl.BlockSpec(shape, lambda b, i: (0,) * nd, pipeline_mode=pl.Buffered(1))


def _pack_w_in(w_in):
    offs = np.cumsum([0, ATTN_WIDTH, KV_LATENT, IDX_HEADS * IDX_DIM, IDX_DIM, IDX_HEADS,
                      ATTN_WIDTH, RNN_WIDTH, RNN_WIDTH, 2 * D_MODEL])
    seg = [w_in[:, offs[k]:offs[k + 1]] for k in range(9)]
    pad = jnp.zeros((D_MODEL, LANES - IDX_DIM - IDX_HEADS), w_in.dtype)
    return jnp.concatenate(seg[:5] + [pad] + seg[5:], axis=1).astype(BF16)


def _pair_block_diag(w):
    n2, r, c = w.shape
    w = w.reshape(n2 // 2, 2, r, c)
    z = jnp.zeros((n2 // 2, r, c), w.dtype)
    top = jnp.concatenate([w[:, 0], z], axis=2)
    bot = jnp.concatenate([z, w[:, 1]], axis=2)
    return jnp.concatenate([top, bot], axis=1)


def _layer(x, norm_g, w_in, b_merge, kv_g, w_uk, w_uv, ln_g, ln_b, w_ap, conv_w, conv_b,
           w_a, b_a, w_x, b_x, lam, w_rp, w_out, out_g):
    B, S, _ = x.shape
    assert S % TM == 0 and TM % TK == 0 and TK % TQ == 0 and TQ == KC
    assert S % CC == 0 and CC % TK == 0 and S % TB == 0
    topk = min(TOPK_MAX, S // 4)
    row = lambda v: v.reshape(1, -1).astype(F32)
    lane_pad = lambda v: jnp.pad(v.astype(F32), (0, LANES - v.shape[0])).reshape(1, LANES)

    w_all = _pack_w_in(w_in)
    wg = jnp.concatenate([_pair_block_diag(w_a), _pair_block_diag(w_x)], axis=2).astype(BF16)
    wuk2t = _pair_block_diag(jnp.swapaxes(w_uk, 1, 2)).astype(BF16)

    nq = S // TQ
    tile = lambda n: pl.BlockSpec((1, TM, n), lambda b, j: (b, j, 0))
    outs = pl.pallas_call(
        _proj_kernel,
        grid=(B, S // TM),
        in_specs=[
            tile(D_MODEL),
            _const_spec((1, D_MODEL)),
            _const_spec((D_MODEL, C_END)),
            _const_spec((N_HEADS // 2, 2 * HEAD_DIM, 2 * KV_LATENT)),
            _const_spec((1, KV_LATENT)),
            _const_spec((1, LANES)),
            _const_spec((1, LANES)),
            _const_spec((CONV_WIDTH, RNN_WIDTH)),
            _const_spec((1, RNN_WIDTH)),
            _const_spec((RNN_WIDTH // LANES, LANES, 2 * LANES)),
            _const_spec((1, RNN_WIDTH)),
            _const_spec((1, RNN_WIDTH)),
            _const_spec((1, RNN_WIDTH)),
            _const_spec((RNN_WIDTH, D_MODEL)),
            _const_spec((1, 2 * D_MODEL)),
        ],
        out_specs=[
            pl.BlockSpec((1, TM // TQ, KV_LATENT, N_HEADS * TQ), lambda b, j: (b, j, 0, 0)),
            tile(KV_LATENT),
            pl.BlockSpec((1, TM // TK, KV_LATENT, TK), lambda b, j: (b, j, 0, 0)),
            pl.BlockSpec((1, TM // TQ, IDX_DIM, IDX_HEADS * TQ), lambda b, j: (b, j, 0, 0)),
            tile(IDX_DIM),
            pl.BlockSpec((1, IDX_HEADS, TM), lambda b, j: (b, 0, j)),
            tile(ATTN_WIDTH),
            tile(D_MODEL),
            tile(D_MODEL),
        ],
        out_shape=[
            jax.ShapeDtypeStruct((B, nq, KV_LATENT, N_HEADS * TQ), BF16),
            jax.ShapeDtypeStruct((B, S, KV_LATENT), BF16),
            jax.ShapeDtypeStruct((B, S // TK, KV_LATENT, TK), BF16),
            jax.ShapeDtypeStruct((B, nq, IDX_DIM, IDX_HEADS * TQ), BF16),
            jax.ShapeDtypeStruct((B, S, IDX_DIM), BF16),
            jax.ShapeDtypeStruct((B, IDX_HEADS, S), F32),
            jax.ShapeDtypeStruct((B, S, ATTN_WIDTH), BF16),
            jax.ShapeDtypeStruct((B, S, D_MODEL), BF16),
            jax.ShapeDtypeStruct((B, S, D_MODEL), BF16),
        ],
        scratch_shapes=[
            pltpu.VMEM((TM + SUBLANES, RNN_WIDTH), F32),
            pltpu.VMEM((TM, RNN_WIDTH), F32),
            pltpu.VMEM((TM, RNN_WIDTH), F32),
            pltpu.VMEM((TM, RNN_WIDTH), F32),
            pltpu.VMEM((SUBLANES, RNN_WIDTH), F32),
            pltpu.VMEM((TM, RNN_WIDTH), F32),
            pltpu.VMEM((TM, D_MODEL), F32),
        ],
        compiler_params=pltpu.CompilerParams(
            dimension_semantics=("arbitrary", "arbitrary"),
            vmem_limit_bytes=VMEM_LIMIT_BYTES),
        name="proj_rglru",
    )(x, row(norm_g), w_all, wuk2t, row(kv_g), lane_pad(ln_g), lane_pad(ln_b), conv_w.astype(F32),
      row(conv_b), wg, row(b_a), row(b_x), row(lam), w_rp.astype(BF16), row(b_merge))
    lq, c, ct, qit, ki, wt, sg, ga, r = outs

    wuv2 = _pair_block_diag(w_uv).astype(BF16)

    qtile = lambda n: pl.BlockSpec((1, TQ, n), lambda b, i: (b, i, 0))
    u = pl.pallas_call(
        functools.partial(_attn_kernel, topk),
        grid=(B, nq),
        in_specs=[
            pl.BlockSpec((1, 1, KV_LATENT, N_HEADS * TQ), lambda b, i: (b, i, 0, 0)),
            pl.BlockSpec((1, S, KV_LATENT), lambda b, i: (b, 0, 0)),
            pl.BlockSpec((1, S // TK, KV_LATENT, TK), lambda b, i: (b, 0, 0, 0)),
            pl.BlockSpec((1, 1, IDX_DIM, IDX_HEADS * TQ), lambda b, i: (b, i, 0, 0)),
            pl.BlockSpec((1, S, IDX_DIM), lambda b, i: (b, 0, 0)),
            pl.BlockSpec((1, IDX_HEADS, TQ), lambda b, i: (b, 0, i)),
            qtile(ATTN_WIDTH),
            _const_spec((N_HEADS // 2, 2 * KV_LATENT, 2 * HEAD_DIM)),
        ],
        out_specs=qtile(ATTN_WIDTH),
        out_shape=jax.ShapeDtypeStruct((B, S, ATTN_WIDTH), BF16),
        scratch_shapes=[
            pltpu.VMEM((2 * KV_LATENT, N_HEADS * TQ), BF16),
            pltpu.VMEM((S, 2 * KV_LATENT), BF16),
            pltpu.VMEM((S // TB, VROWS, TB), BF16),
            pltpu.VMEM((S, LANES), F32),
            pltpu.VMEM((S, TQ), F32),
            pltpu.VMEM((S, N_HEADS * TQ), F32),
            pltpu.VMEM((VROWS, N_HEADS * TQ), F32),
        ],
        compiler_params=pltpu.CompilerParams(
            dimension_semantics=("arbitrary", "arbitrary"),
            vmem_limit_bytes=VMEM_LIMIT_BYTES),
        name="sparse_attn",
    )(lq, c, ct, qit, ki, wt, sg, wuv2)

    return pl.pallas_call(
        _out_kernel,
        grid=(B, S // TM),
        in_specs=[
            tile(ATTN_WIDTH),
            tile(D_MODEL),
            tile(D_MODEL),
            tile(D_MODEL),
            _const_spec((ATTN_WIDTH, D_MODEL)),
            _const_spec((D_MODEL, D_MODEL)),
            _const_spec((1, D_MODEL)),
        ],
        out_specs=tile(D_MODEL),
        out_shape=jax.ShapeDtypeStruct((B, S, D_MODEL), F32),
        compiler_params=pltpu.CompilerParams(
            dimension_semantics=("arbitrary", "arbitrary"),
            vmem_limit_bytes=VMEM_LIMIT_BYTES),
        name="merge_out",
    )(u, ga, r, x, w_ap.astype(BF16), w_out.astype(BF16), row(out_g))


def kernel(x, norm_gain, w_in, b_merge, kv_norm_gain, w_uk, w_uv, idx_ln_gain, idx_ln_bias,
           w_attn_proj, conv_w, conv_b, w_rg_a, b_rg_a, w_rg_x, b_rg_x, lru_lambda,
           w_rnn_proj, w_out, final_norm_gain):
    assert norm_gain.shape[0] == 1, "only the stated depth-1 stack is supported"
    return _layer(x, norm_gain[0], w_in[0], b_merge[0], kv_norm_gain[0], w_uk[0], w_uv[0],
                  idx_ln_gain[0], idx_ln_bias[0], w_attn_proj[0], conv_w[0], conv_b[0],
                  w_rg_a[0], b_rg_a[0], w_rg_x[0], b_rg_x[0], lru_lambda[0],
                  w_rnn_proj[0], w_out[0], final_norm_gain)
```

```python
import functools

import numpy as np
import jax
import jax.numpy as jnp
from jax import lax
from jax.experimental import pallas as pl
from jax.experimental.pallas import tpu as pltpu

D_MODEL = 1024
N_HEADS = 16
HEAD_DIM = 64
ATTN_WIDTH = N_HEADS * HEAD_DIM
KV_LATENT = 128
IDX_HEADS = 8
IDX_DIM = 64
TOPK_MAX = 256
RNN_WIDTH = 1024
RNN_BLOCKS = 16
RNN_BLOCK_DIM = RNN_WIDTH // RNN_BLOCKS
CONV_WIDTH = 4
LRU_C = 8.0
NORM_EPS = 1e-6

F32 = jnp.float32
BF16 = jnp.bfloat16

LANES = 128
SUBLANES = 8
VMEM_LIMIT_BYTES = 56 * 1024 * 1024

TM = 512
TQ = 128
KC = 128
TK = 256
CC = 512
TB = 2 * TK
VROWS = 144
BISECT_FIXED = 18
BISECT_PER_CHECK = 2
BISECT_MAX_CHECKS = 9
LOG2E = 1.4426950408889634
F32_MIN_NORMAL = 2.0 ** -126
MASK_NEG = -(2.0 ** 100)
M_INIT = -1e20

C_Q = 0
C_C = C_Q + ATTN_WIDTH
C_QI = C_C + KV_LATENT
C_KW = C_QI + IDX_HEADS * IDX_DIM
C_GA = C_KW + LANES
C_XR = C_GA + ATTN_WIDTH
C_GR = C_XR + RNN_WIDTH
C_MG = C_GR + RNN_WIDTH
C_END = C_MG + 2 * D_MODEL

ALIBI_SLOPES = [float(2.0 ** (-8.0 * (h + 1) / N_HEADS)) for h in range(N_HEADS)]


def _sigmoid(v):
    return 0.5 * jnp.tanh(0.5 * v) + 0.5


def _f32_to_key(v):
    b = lax.bitcast_convert_type(v, jnp.int32)
    return b ^ ((b >> 31) & 0x7FFFFFFF)


def _key_to_f32(k):
    return lax.bitcast_convert_type(k ^ ((k >> 31) & 0x7FFFFFFF), F32)


def _proj_kernel(x_ref, ng_ref, w_ref, wuk_ref, kvg_ref, lng_ref, lnb_ref, cw_ref, cb_ref, wg_ref,
                 ba_ref, bx_ref, lam_ref, wrp_ref, bm_ref,
                 lq_ref, c_ref, ct_ref, qit_ref, ki_ref, wt_ref, sg_ref, ga_ref, r_ref,
                 xext_ref, a_ref, b_ref, h_ref, hc_ref, szr_ref, gr_ref):
    j = pl.program_id(1)

    @pl.when(j == 0)
    def _():
        xext_ref[0:SUBLANES, :] = jnp.zeros((SUBLANES, RNN_WIDTH), F32)
        hc_ref[...] = jnp.zeros((SUBLANES, RNN_WIDTH), F32)

    @pl.when(j > 0)
    def _():
        xext_ref[0:SUBLANES, :] = xext_ref[TM:TM + SUBLANES, :]

    x = x_ref[0]
    xn = x * lax.rsqrt(jnp.mean(x * x, axis=-1, keepdims=True) + NORM_EPS) * ng_ref[...]
    xb = xn.astype(BF16)

    def proj(lo, hi):
        return jnp.dot(xb, w_ref[:, lo:hi], preferred_element_type=F32)

    xext_ref[SUBLANES:SUBLANES + TM, :] = proj(C_XR, C_GR)

    def rnn_inputs(blk):
        cols = slice(blk * 2 * LANES, (blk + 1) * 2 * LANES)
        y = cb_ref[:, cols]
        for k in range(CONV_WIDTH):
            off = SUBLANES - (CONV_WIDTH - 1) + k
            y = y + cw_ref[k:k + 1, cols] * xext_ref[off:off + TM, cols]
        yb = y.astype(BF16)
        g = [jnp.dot(yb[:, h * LANES:(h + 1) * LANES], wg_ref[2 * blk + h],
                     preferred_element_type=F32) for h in range(2)]
        rg = _sigmoid(jnp.concatenate([g[0][:, :LANES], g[1][:, :LANES]], axis=1) + ba_ref[:, cols])
        ig = _sigmoid(jnp.concatenate([g[0][:, LANES:], g[1][:, LANES:]], axis=1) + bx_ref[:, cols])
        nl = -lam_ref[:, cols]
        softplus = jnp.maximum(nl, 0.0) + jnp.log1p(jnp.exp(-jnp.abs(nl)))
        log_a = (-LRU_C) * rg * softplus
        a_ref[:, cols] = jnp.exp(log_a)
        th = jnp.tanh(log_a)
        b_ref[:, cols] = jnp.sqrt(-2.0 * th / (1.0 - th)) * (ig * y)

    def attn_queries():
        qb16 = proj(C_Q, C_C).astype(BF16)
        ql = [jnp.dot(qb16[:, p * LANES:(p + 1) * LANES], wuk_ref[p], preferred_element_type=F32)
              for p in range(N_HEADS // 2)]
        qlt = (jnp.concatenate(ql, axis=1) * (HEAD_DIM ** -0.5 * LOG2E)).T
        for qb in range(TM // TQ):
            for h in range(N_HEADS):
                lq_ref[0, qb, :, h * TQ:(h + 1) * TQ] = (
                    qlt[h * KV_LATENT:(h + 1) * KV_LATENT, qb * TQ:(qb + 1) * TQ].astype(BF16))

    def attn_keys_and_indexer():
        zc = proj(C_C, C_QI)
        cn = zc * lax.rsqrt(jnp.mean(zc * zc, axis=-1, keepdims=True) + NORM_EPS) * kvg_ref[...]
        c_ref[0] = cn.astype(BF16)
        cnt = cn.T
        for k in range(TM // TK):
            ct_ref[0, k] = cnt[:, k * TK:(k + 1) * TK].astype(BF16)

        zqt = (proj(C_QI, C_KW) * (IDX_DIM ** -0.5)).T
        for qb in range(TM // TQ):
            for jh in range(IDX_HEADS):
                qit_ref[0, qb, :, jh * TQ:(jh + 1) * TQ] = (
                    zqt[jh * IDX_DIM:(jh + 1) * IDX_DIM, qb * TQ:(qb + 1) * TQ].astype(BF16))

        zk = proj(C_KW, C_GA)
        lane = lax.broadcasted_iota(jnp.int32, zk.shape, 1)
        is_k = lane < IDX_DIM
        mu = jnp.sum(jnp.where(is_k, zk, 0.0), axis=-1, keepdims=True) * (1.0 / IDX_DIM)
        kc = jnp.where(is_k, zk - mu, 0.0)
        var = jnp.sum(kc * kc, axis=-1, keepdims=True) * (1.0 / IDX_DIM)
        kin = kc * lax.rsqrt(var + NORM_EPS) * lng_ref[...] + lnb_ref[...]
        ki_ref[0] = kin[:, :IDX_DIM].astype(BF16)
        wt_ref[0] = zk.T[IDX_DIM:IDX_DIM + IDX_HEADS, :] * (IDX_HEADS ** -0.5)

    def attn_gate():
        zg = proj(C_GA, C_XR)
        sg_ref[0] = (zg * _sigmoid(zg)).astype(BF16)

    rnn_inputs(0)
    ga_ref[0] = _sigmoid(proj(C_MG, C_MG + D_MODEL) + bm_ref[:, :D_MODEL]).astype(BF16)
    rnn_inputs(1)
    gr_ref[...] = _sigmoid(proj(C_MG + D_MODEL, C_END) + bm_ref[:, D_MODEL:])
    rnn_inputs(2)
    zr = proj(C_GR, C_MG)
    szr_ref[...] = zr * _sigmoid(zr)
    rnn_inputs(3)

    row = lax.broadcasted_iota(jnp.int32, (SUBLANES, RNN_WIDTH), 0)

    hprev = hc_ref[...]
    for ci in range(TM // SUBLANES):
        rows = slice(ci * SUBLANES, (ci + 1) * SUBLANES)
        av = a_ref[rows, :]
        bv = b_ref[rows, :]
        for d in (1, 2, 4):
            a_sh = jnp.where(row >= d, pltpu.roll(av, d, 0), 1.0)
            b_sh = jnp.where(row >= d, pltpu.roll(bv, d, 0), 0.0)
            bv = av * b_sh + bv
            av = av * a_sh
        hv = av * hprev + bv
        h_ref[rows, :] = hv
        hprev = jnp.broadcast_to(hv[SUBLANES - 1:SUBLANES, :], (SUBLANES, RNN_WIDTH))
    hc_ref[...] = hprev

    attn_queries()
    attn_keys_and_indexer()
    attn_gate()

    u = (h_ref[...] * szr_ref[...]).astype(BF16)
    y_rnn = jnp.dot(u, wrp_ref[...], preferred_element_type=F32)
    r_ref[0] = (gr_ref[...] * y_rnn).astype(BF16)


def _attn_kernel(topk, lq_ref, c_ref, ct_ref, qit_ref, ki_ref, wt_ref, sg_ref, wuv_ref,
                 u_ref,
                 lhst_ref, kext_ref, vt_ref, pos_ref, sc_ref, s_ref, acc_ref):
    i = pl.program_id(1)
    seq = kext_ref.shape[0]
    n_kt = (i * TQ + TQ + TK - 1) // TK
    n_cc = (i * TQ + TQ + CC - 1) // CC

    @pl.when(jnp.logical_and(pl.program_id(0) == 0, i == 0))
    def _():
        er = lax.broadcasted_iota(jnp.int32, (TQ, TQ), 0)
        ec = lax.broadcasted_iota(jnp.int32, (TQ, TQ), 1)
        eye = jnp.where(er == ec, 1.0, 0.0).astype(BF16)
        for h in range(N_HEADS):
            lhst_ref[KV_LATENT:KV_LATENT + TQ, h * TQ:(h + 1) * TQ] = eye
        pos_ref[...] = lax.broadcasted_iota(jnp.int32, (seq, LANES), 0).astype(F32)
        rr = lax.broadcasted_iota(jnp.int32, (VROWS - KV_LATENT, TB), 0)
        for k in range(seq // TB):
            vt_ref[k, KV_LATENT:VROWS, :] = jnp.where(rr == 0, 1.0, 0.0).astype(BF16)

    @pl.when(i == 0)
    def _():
        kext_ref[:, 0:KV_LATENT] = c_ref[0]
        for k in range(seq // TK):
            vt_ref[k // (TB // TK), 0:KV_LATENT, (k % (TB // TK)) * TK:(k % (TB // TK) + 1) * TK] = (
                ct_ref[0, k])

    lhst_ref[0:KV_LATENT, :] = lq_ref[0, 0]

    qit = qit_ref[0, 0]
    wv = wt_ref[0]
    tpos = i * TQ + lax.broadcasted_iota(jnp.int32, (1, TQ), 1)

    def fold8(v, op):
        parts = [op(v[k * KC:(k + 1) * KC].reshape(KC // SUBLANES, SUBLANES, v.shape[1]), axis=0)
                 for k in range(v.shape[0] // KC)]
        return op(jnp.stack(parts), axis=0)

    n_big = n_kt // 2
    has_tail = n_kt % 2 == 1
    r_tail = pl.multiple_of(n_big * TB, TB)

    def score_block(r0, rows, carry):
        lo8, hi8 = carry
        prod = jnp.dot(ki_ref[0, pl.ds(r0, rows), :], qit, preferred_element_type=F32)
        sc = jnp.maximum(prod[:, 0:TQ], 0.0) * wv[0:1, :]
        for jh in range(1, IDX_HEADS):
            sc = sc + jnp.maximum(prod[:, jh * TQ:(jh + 1) * TQ], 0.0) * wv[jh:jh + 1, :]
        causal = (lax.broadcasted_iota(jnp.int32, (rows, TQ), 0) + r0) <= tpos
        sc_ref[pl.ds(r0, rows), :] = jnp.where(causal, sc, -jnp.inf)
        lo8 = jnp.minimum(lo8, fold8(jnp.where(causal, sc, jnp.inf), jnp.min))
        hi8 = jnp.maximum(hi8, fold8(jnp.where(causal, sc, -jnp.inf), jnp.max))
        return lo8, hi8

    ext = lax.fori_loop(
        0, n_big, lambda k, cr: score_block(pl.multiple_of(k * TB, TB), TB, cr),
        (jnp.full((SUBLANES, TQ), jnp.inf, F32), jnp.full((SUBLANES, TQ), -jnp.inf, F32)))
    lo8, hi8 = lax.cond(has_tail, lambda cr: score_block(r_tail, TK, cr), lambda cr: cr, ext)

    @pl.when(n_kt * TK < n_cc * CC)
    def _():
        r0 = pl.multiple_of(n_kt * TK, TK)
        sc_ref[pl.ds(r0, TK), :] = jnp.full((TK, TQ), -jnp.inf, F32)

    lo_k = _f32_to_key(jnp.min(lo8, axis=0, keepdims=True))
    vmax = jnp.max(hi8, axis=0, keepdims=True)
    hi_k = jnp.where(jnp.abs(vmax) < F32_MIN_NORMAL, _f32_to_key(jnp.full_like(vmax, F32_MIN_NORMAL)),
                     _f32_to_key(vmax) + 1)
    kf = float(topk)

    def count_ge(thr, n_chunks):
        xs = sc_ref[0:n_chunks * CC, :]
        return jnp.sum(fold8(jnp.where(xs >= thr, 1.0, 0.0), jnp.sum), axis=0, keepdims=True)

    def midpoint(lo_, hi_):
        return (lo_ >> 1) + (hi_ >> 1) + (lo_ & hi_ & 1)

    def unresolved(carry):
        lo_, hi_, clo_ = carry
        open_ = jnp.where(clo_ > kf, jnp.where(midpoint(lo_, hi_) != lo_, 1.0, 0.0), 0.0)
        return (jnp.max(open_) > 0.5).astype(jnp.int32)

    n_causal = (tpos[0:1, :] + 1).astype(F32)
    n_fixed = jnp.where((i + 1) * TQ <= topk, 0, BISECT_FIXED)

    def make_search(n_chunks):
        def bisect(carry):
            lo_, hi_, clo_ = carry
            mid = midpoint(lo_, hi_)
            cnt = count_ge(_key_to_f32(mid), n_chunks)
            ge = cnt >= kf
            return jnp.where(ge, mid, lo_), jnp.where(ge, hi_, mid), jnp.where(ge, cnt, clo_)

        def search_more(state):
            it, _, cr = state
            for _ in range(BISECT_PER_CHECK):
                cr = bisect(cr)
            return it + 1, unresolved(cr), cr

        def search(carry):
            carry = lax.fori_loop(0, n_fixed, lambda _, cr: bisect(cr), carry)
            _, _, (lo_, hi_, clo_) = lax.while_loop(
                lambda st: jnp.logical_and(st[1] > 0, st[0] < BISECT_MAX_CHECKS),
                search_more, (jnp.int32(0), unresolved(carry), carry))
            need_ = kf - count_ge(_key_to_f32(hi_), n_chunks)
            return lo_, hi_, clo_, need_

        return search

    lo_k, hi_k, c_lo, need = lax.switch(
        n_cc - 1, [make_search(n) for n in range(1, seq // CC + 1)], (lo_k, hi_k, n_causal))
    lo = _key_to_f32(lo_k)
    hi = _key_to_f32(hi_k)

    has_ties = jnp.max(jnp.where(c_lo > kf, 1.0, 0.0)) > 0.5

    @pl.when(jnp.logical_not(has_ties))
    def _():
        def mask_chunk(kt, _):
            r0 = pl.multiple_of(kt * TK, TK)
            xs = sc_ref[pl.ds(r0, TK), :]
            kext_ref[pl.ds(r0, TK), KV_LATENT:KV_LATENT + TQ] = (
                jnp.where(xs >= lo, 0.0, MASK_NEG).astype(BF16))
            return 0
        lax.fori_loop(0, n_kt, mask_chunk, 0)

    @pl.when(has_ties)
    def _():
        tri =jnp.where(lax.broadcasted_iota(jnp.int32, (KC, KC), 1)
                        <= lax.broadcasted_iota(jnp.int32, (KC, KC), 0), 1.0, 0.0).astype(BF16)

        def mask_chunk(kc, before):
            r0 = pl.multiple_of(kc * KC, KC)
            xs = sc_ref[pl.ds(r0, KC), :]
            band = jnp.where(xs >= lo, jnp.where(xs < hi, 1.0, 0.0), 0.0)
            incl = jnp.dot(tri, band.astype(BF16), preferred_element_type=F32)
            rank = incl - band + before
            keep = jnp.where(xs >= hi, 1.0, jnp.where(rank < need, band, 0.0))
            kext_ref[pl.ds(r0, KC), KV_LATENT:KV_LATENT + TQ] = (
                jnp.where(keep > 0.0, 0.0, MASK_NEG).astype(BF16))
            return before + incl[KC - 1:KC, :]

        lax.fori_loop(0, (TK // KC) * n_kt, mask_chunk, jnp.zeros((1, TQ), F32))


    def qk_block(r0, rows, mx):
        st = jnp.dot(kext_ref[pl.ds(r0, rows), :], lhst_ref[...], preferred_element_type=F32)
        posb = pos_ref[pl.ds(r0, rows), :]
        tops = []
        for h in range(N_HEADS):
            sh = st[:, h * TQ:(h + 1) * TQ] + posb * (ALIBI_SLOPES[h] * LOG2E)
            s_ref[pl.ds(r0, rows), h * TQ:(h + 1) * TQ] = sh
            tops.append(fold8(sh, jnp.max))
        return jnp.maximum(mx, jnp.concatenate(tops, axis=1))

    mx = lax.fori_loop(0, n_big, lambda k, v: qk_block(pl.multiple_of(k * TB, TB), TB, v),
                       jnp.full((SUBLANES, N_HEADS * TQ), M_INIT, F32))
    mx = lax.cond(has_tail, lambda v: qk_block(r_tail, TK, v), lambda v: v, mx)
    m = jnp.max(mx, axis=0, keepdims=True)
    acc_ref[...] = jnp.zeros(acc_ref.shape, F32)

    def pv_block(r0, rows, vt):
        pt = jnp.exp2(s_ref[pl.ds(r0, rows), :] - m).astype(BF16)
        acc_ref[...] += jnp.dot(vt, pt, preferred_element_type=F32)

    def pv_big(k, _):
        pv_block(pl.multiple_of(k * TB, TB), TB, vt_ref[k])
        return 0

    lax.fori_loop(0, n_big, pv_big, 0)

    @pl.when(has_tail)
    def _():
        pv_block(r_tail, TK, vt_ref[n_big, :, 0:TK])

    o_parts = []
    for p in range(N_HEADS // 2):
        cols = slice(p * 2 * TQ, (p + 1) * 2 * TQ)
        xt = (acc_ref[0:KV_LATENT, cols] / acc_ref[KV_LATENT:KV_LATENT + 1, cols]).T
        pair = jnp.concatenate([xt[:TQ], xt[TQ:]], axis=1).astype(BF16)
        o_parts.append(jnp.dot(pair, wuv_ref[p], preferred_element_type=F32))
    o = jnp.concatenate(o_parts, axis=1)
    u_ref[0] = (o * sg_ref[0].astype(F32)).astype(BF16)


def _out_kernel(u_ref, ga_ref, r_ref, x_ref, wap_ref, wout_ref, fg_ref, out_ref):
    y_attn = jnp.dot(u_ref[0], wap_ref[...], preferred_element_type=F32)
    mixed = (ga_ref[0].astype(F32) * y_attn + r_ref[0].astype(F32)).astype(BF16)
    res = x_ref[0] + jnp.dot(mixed, wout_ref[...], preferred_element_type=F32)
    out_ref[0] = res * lax.rsqrt(jnp.mean(res * res, axis=-1, keepdims=True) + NORM_EPS) * fg_ref[...]


def _const_spec(shape):
    nd = len(shape)
    return pl.BlockSpec(shape, lambda b, i: (0,) * nd, pipeline_mode=pl.Buffered(1))


def _pack_w_in(w_in):
    offs = np.cumsum([0, ATTN_WIDTH, KV_LATENT, IDX_HEADS * IDX_DIM, IDX_DIM, IDX_HEADS,
                      ATTN_WIDTH, RNN_WIDTH, RNN_WIDTH, 2 * D_MODEL])
    seg = [w_in[:, offs[k]:offs[k + 1]] for k in range(9)]
    pad = jnp.zeros((D_MODEL, LANES - IDX_DIM - IDX_HEADS), w_in.dtype)
    return jnp.concatenate(seg[:5] + [pad] + seg[5:], axis=1).astype(BF16)


def _pair_block_diag(w):
    n2, r, c = w.shape
    w = w.reshape(n2 // 2, 2, r, c)
    z = jnp.zeros((n2 // 2, r, c), w.dtype)
    top = jnp.concatenate([w[:, 0], z], axis=2)
    bot = jnp.concatenate([z, w[:, 1]], axis=2)
    return jnp.concatenate([top, bot], axis=1)


def _layer(x, norm_g, w_in, b_merge, kv_g, w_uk, w_uv, ln_g, ln_b, w_ap, conv_w, conv_b,
           w_a, b_a, w_x, b_x, lam, w_rp, w_out, out_g):
    B, S, _ = x.shape
    assert S % TM == 0 and TM % TK == 0 and TK % TQ == 0 and TQ == KC
    assert S % CC == 0 and CC % TK == 0 and S % TB == 0
    topk = min(TOPK_MAX, S // 4)
    row = lambda v: v.reshape(1, -1).astype(F32)
    lane_pad = lambda v: jnp.pad(v.astype(F32), (0, LANES - v.shape[0])).reshape(1, LANES)

    w_all = _pack_w_in(w_in)
    wg = jnp.concatenate([_pair_block_diag(w_a), _pair_block_diag(w_x)], axis=2).astype(BF16)
    wuk2t = _pair_block_diag(jnp.swapaxes(w_uk, 1, 2)).astype(BF16)

    nq = S // TQ
    tile = lambda n: pl.BlockSpec((1, TM, n), lambda b, j: (b, j, 0))
    outs = pl.pallas_call(
        _proj_kernel,
        grid=(B, S // TM),
        in_specs=[
            tile(D_MODEL),
            _const_spec((1, D_MODEL)),
            _const_spec((D_MODEL, C_END)),
            _const_spec((N_HEADS // 2, 2 * HEAD_DIM, 2 * KV_LATENT)),
            _const_spec((1, KV_LATENT)),
            _const_spec((1, LANES)),
            _const_spec((1, LANES)),
            _const_spec((CONV_WIDTH, RNN_WIDTH)),
            _const_spec((1, RNN_WIDTH)),
            _const_spec((RNN_WIDTH // LANES, LANES, 2 * LANES)),
            _const_spec((1, RNN_WIDTH)),
            _const_spec((1, RNN_WIDTH)),
            _const_spec((1, RNN_WIDTH)),
            _const_spec((RNN_WIDTH, D_MODEL)),
            _const_spec((1, 2 * D_MODEL)),
        ],
        out_specs=[
            pl.BlockSpec((1, TM // TQ, KV_LATENT, N_HEADS * TQ), lambda b, j: (b, j, 0, 0)),
            tile(KV_LATENT),
            pl.BlockSpec((1, TM // TK, KV_LATENT, TK), lambda b, j: (b, j, 0, 0)),
            pl.BlockSpec((1, TM // TQ, IDX_DIM, IDX_HEADS * TQ), lambda b, j: (b, j, 0, 0)),
            tile(IDX_DIM),
            pl.BlockSpec((1, IDX_HEADS, TM), lambda b, j: (b, 0, j)),
            tile(ATTN_WIDTH),
            tile(D_MODEL),
            tile(D_MODEL),
        ],
        out_shape=[
            jax.ShapeDtypeStruct((B, nq, KV_LATENT, N_HEADS * TQ), BF16),
            jax.ShapeDtypeStruct((B, S, KV_LATENT), BF16),
            jax.ShapeDtypeStruct((B, S // TK, KV_LATENT, TK), BF16),
            jax.ShapeDtypeStruct((B, nq, IDX_DIM, IDX_HEADS * TQ), BF16),
            jax.ShapeDtypeStruct((B, S, IDX_DIM), BF16),
            jax.ShapeDtypeStruct((B, IDX_HEADS, S), F32),
            jax.ShapeDtypeStruct((B, S, ATTN_WIDTH), BF16),
            jax.ShapeDtypeStruct((B, S, D_MODEL), BF16),
            jax.ShapeDtypeStruct((B, S, D_MODEL), BF16),
        ],
        scratch_shapes=[
            pltpu.VMEM((TM + SUBLANES, RNN_WIDTH), F32),
            pltpu.VMEM((TM, RNN_WIDTH), F32),
            pltpu.VMEM((TM, RNN_WIDTH), F32),
            pltpu.VMEM((TM, RNN_WIDTH), F32),
            pltpu.VMEM((SUBLANES, RNN_WIDTH), F32),
            pltpu.VMEM((TM, RNN_WIDTH), F32),
            pltpu.VMEM((TM, D_MODEL), F32),
        ],
        compiler_params=pltpu.CompilerParams(
            dimension_semantics=("arbitrary", "arbitrary"),
            vmem_limit_bytes=VMEM_LIMIT_BYTES),
        name="proj_rglru",
    )(x, row(norm_g), w_all, wuk2t, row(kv_g), lane_pad(ln_g), lane_pad(ln_b), conv_w.astype(F32),
      row(conv_b), wg, row(b_a), row(b_x), row(lam), w_rp.astype(BF16), row(b_merge))
    lq, c, ct, qit, ki, wt, sg, ga, r = outs

    wuv2 = _pair_block_diag(w_uv).astype(BF16)

    qtile = lambda n: pl.BlockSpec((1, TQ, n), lambda b, i: (b, i, 0))
    u = pl.pallas_call(
        functools.partial(_attn_kernel, topk),
        grid=(B, nq),
        in_specs=[
            pl.BlockSpec((1, 1, KV_LATENT, N_HEADS * TQ), lambda b, i: (b, i, 0, 0)),
            pl.BlockSpec((1, S, KV_LATENT), lambda b, i: (b, 0, 0)),
            pl.BlockSpec((1, S // TK, KV_LATENT, TK), lambda b, i: (b, 0, 0, 0)),
            pl.BlockSpec((1, 1, IDX_DIM, IDX_HEADS * TQ), lambda b, i: (b, i, 0, 0)),
            pl.BlockSpec((1, S, IDX_DIM), lambda b, i: (b, 0, 0)),
            pl.BlockSpec((1, IDX_HEADS, TQ), lambda b, i: (b, 0, i)),
            qtile(ATTN_WIDTH),
            _const_spec((N_HEADS // 2, 2 * KV_LATENT, 2 * HEAD_DIM)),
        ],
        out_specs=qtile(ATTN_WIDTH),
        out_shape=jax.ShapeDtypeStruct((B, S, ATTN_WIDTH), BF16),
        scratch_shapes=[
            pltpu.VMEM((2 * KV_LATENT, N_HEADS * TQ), BF16),
            pltpu.VMEM((S, 2 * KV_LATENT), BF16),
            pltpu.VMEM((S // TB, VROWS, TB), BF16),
            pltpu.VMEM((S, LANES), F32),
            pltpu.VMEM((S, TQ), F32),
            pltpu.VMEM((S, N_HEADS * TQ), F32),
            pltpu.VMEM((VROWS, N_HEADS * TQ), F32),
        ],
        compiler_params=pltpu.CompilerParams(
            dimension_semantics=("arbitrary", "arbitrary"),
            vmem_limit_bytes=VMEM_LIMIT_BYTES),
        name="sparse_attn",
    )(lq, c, ct, qit, ki, wt, sg, wuv2)

    return pl.pallas_call(
        _out_kernel,
        grid=(B, S // TM),
        in_specs=[
            tile(ATTN_WIDTH),
            tile(D_MODEL),
            tile(D_MODEL),
            tile(D_MODEL),
            _const_spec((ATTN_WIDTH, D_MODEL)),
            _const_spec((D_MODEL, D_MODEL)),
            _const_spec((1, D_MODEL)),
        ],
        out_specs=tile(D_MODEL),
        out_shape=jax.ShapeDtypeStruct((B, S, D_MODEL), F32),
        compiler_params=pltpu.CompilerParams(
            dimension_semantics=("arbitrary", "arbitrary"),
            vmem_limit_bytes=VMEM_LIMIT_BYTES),
        name="merge_out",
    )(u, ga, r, x, w_ap.astype(BF16), w_out.astype(BF16), row(out_g))


def kernel(x, norm_gain, w_in, b_merge, kv_norm_gain, w_uk, w_uv, idx_ln_gain, idx_ln_bias,
           w_attn_proj, conv_w, conv_b, w_rg_a, b_rg_a, w_rg_x, b_rg_x, lru_lambda,
           w_rnn_proj, w_out, final_norm_gain):
    assert norm_gain.shape[0] == 1, "only the stated depth-1 stack is supported"
    return _layer(x, norm_gain[0], w_in[0], b_merge[0], kv_norm_gain[0], w_uk[0], w_uv[0],
                  idx_ln_gain[0], idx_ln_bias[0], w_attn_proj[0], conv_w[0], conv_b[0],
                  w_rg_a[0], b_rg_a[0], w_rg_x[0], b_rg_x[0], lru_lambda[0],
                  w_rnn_proj[0], w_out[0], final_norm_gain)
```

```python
import functools

import numpy as np
import jax
import jax.numpy as jnp
from jax import lax
from jax.experimental import pallas as pl
from jax.experimental.pallas import tpu as pltpu

D_MODEL = 1024
N_HEADS = 16
HEAD_DIM = 64
ATTN_WIDTH = N_HEADS * HEAD_DIM
KV_LATENT = 128
IDX_HEADS = 8
IDX_DIM = 64
TOPK_MAX = 256
RNN_WIDTH = 1024
RNN_BLOCKS = 16
RNN_BLOCK_DIM = RNN_WIDTH // RNN_BLOCKS
CONV_WIDTH = 4
LRU_C = 8.0
NORM_EPS = 1e-6

F32 = jnp.float32
BF16 = jnp.bfloat16

LANES = 128
SUBLANES = 8
VMEM_LIMIT_BYTES = 56 * 1024 * 1024

TM = 512
TQ = 128
KC = 128
TK = 256
CC = 512
TB = 2 * TK
VROWS = 144
BISECT_FIXED = 14
BISECT_PER_CHECK = 2
BISECT_MAX_CHECKS = 10
LOG2E = 1.4426950408889634
F32_MIN_NORMAL = 2.0 ** -126
MASK_NEG = -(2.0 ** 100)
M_INIT = -1e20

C_Q = 0
C_C = C_Q + ATTN_WIDTH
C_QI = C_C + KV_LATENT
C_KW = C_QI + IDX_HEADS * IDX_DIM
C_GA = C_KW + LANES
C_XR = C_GA + ATTN_WIDTH
C_GR = C_XR + RNN_WIDTH
C_MG = C_GR + RNN_WIDTH
C_END = C_MG + 2 * D_MODEL

ALIBI_SLOPES = [float(2.0 ** (-8.0 * (h + 1) / N_HEADS)) for h in range(N_HEADS)]


def _sigmoid(v):
    return 0.5 * jnp.tanh(0.5 * v) + 0.5


def _f32_to_key(v):
    b = lax.bitcast_convert_type(v, jnp.int32)
    return b ^ ((b >> 31) & 0x7FFFFFFF)


def _key_to_f32(k):
    return lax.bitcast_convert_type(k ^ ((k >> 31) & 0x7FFFFFFF), F32)


def _proj_kernel(x_ref, ng_ref, w_ref, wuk_ref, kvg_ref, lng_ref, lnb_ref, cw_ref, cb_ref, wg_ref,
                 ba_ref, bx_ref, lam_ref, wrp_ref, bm_ref,
                 lq_ref, c_ref, ct_ref, qit_ref, ki_ref, wt_ref, sg_ref, ga_ref, r_ref,
                 xext_ref, a_ref, b_ref, h_ref, hc_ref, szr_ref, gr_ref):
    j = pl.program_id(1)

    @pl.when(j == 0)
    def _():
        xext_ref[0:SUBLANES, :] = jnp.zeros((SUBLANES, RNN_WIDTH), F32)
        hc_ref[...] = jnp.zeros((SUBLANES, RNN_WIDTH), F32)

    @pl.when(j > 0)
    def _():
        xext_ref[0:SUBLANES, :] = xext_ref[TM:TM + SUBLANES, :]

    x = x_ref[0]
    xn = x * lax.rsqrt(jnp.mean(x * x, axis=-1, keepdims=True) + NORM_EPS) * ng_ref[...]
    xb = xn.astype(BF16)

    def proj(lo, hi):
        return jnp.dot(xb, w_ref[:, lo:hi], preferred_element_type=F32)

    xext_ref[SUBLANES:SUBLANES + TM, :] = proj(C_XR, C_GR)

    def rnn_inputs(blk):
        cols = slice(blk * 2 * LANES, (blk + 1) * 2 * LANES)
        y = cb_ref[:, cols]
        for k in range(CONV_WIDTH):
            off = SUBLANES - (CONV_WIDTH - 1) + k
            y = y + cw_ref[k:k + 1, cols] * xext_ref[off:off + TM, cols]
        yb = y.astype(BF16)
        g = [jnp.dot(yb[:, h * LANES:(h + 1) * LANES], wg_ref[2 * blk + h],
                     preferred_element_type=F32) for h in range(2)]
        rg = _sigmoid(jnp.concatenate([g[0][:, :LANES], g[1][:, :LANES]], axis=1) + ba_ref[:, cols])
        ig = _sigmoid(jnp.concatenate([g[0][:, LANES:], g[1][:, LANES:]], axis=1) + bx_ref[:, cols])
        nl = -lam_ref[:, cols]
        softplus = jnp.maximum(nl, 0.0) + jnp.log1p(jnp.exp(-jnp.abs(nl)))
        log_a = (-LRU_C) * rg * softplus
        a_ref[:, cols] = jnp.exp(log_a)
        th = jnp.tanh(log_a)
        b_ref[:, cols] = jnp.sqrt(-2.0 * th / (1.0 - th)) * (ig * y)

    def attn_queries():
        qb16 = proj(C_Q, C_C).astype(BF16)
        ql = [jnp.dot(qb16[:, p * LANES:(p + 1) * LANES], wuk_ref[p], preferred_element_type=F32)
              for p in range(N_HEADS // 2)]
        qlt = (jnp.concatenate(ql, axis=1) * (HEAD_DIM ** -0.5 * LOG2E)).T
        for qb in range(TM // TQ):
            for h in range(N_HEADS):
                lq_ref[0, qb, :, h * TQ:(h + 1) * TQ] = (
                    qlt[h * KV_LATENT:(h + 1) * KV_LATENT, qb * TQ:(qb + 1) * TQ].astype(BF16))

    def attn_keys_and_indexer():
        zc = proj(C_C, C_QI)
        cn = zc * lax.rsqrt(jnp.mean(zc * zc, axis=-1, keepdims=True) + NORM_EPS) * kvg_ref[...]
        c_ref[0] = cn.astype(BF16)
        cnt = cn.T
        for k in range(TM // TK):
            ct_ref[0, k] = cnt[:, k * TK:(k + 1) * TK].astype(BF16)

        zqt = (proj(C_QI, C_KW) * (IDX_DIM ** -0.5)).T
        for qb in range(TM // TQ):
            for jh in range(IDX_HEADS):
                qit_ref[0, qb, :, jh * TQ:(jh + 1) * TQ] = (
                    zqt[jh * IDX_DIM:(jh + 1) * IDX_DIM, qb * TQ:(qb + 1) * TQ].astype(BF16))

        zk = proj(C_KW, C_GA)
        lane = lax.broadcasted_iota(jnp.int32, zk.shape, 1)
        is_k = lane < IDX_DIM
        mu = jnp.sum(jnp.where(is_k, zk, 0.0), axis=-1, keepdims=True) * (1.0 / IDX_DIM)
        kc = jnp.where(is_k, zk - mu, 0.0)
        var = jnp.sum(kc * kc, axis=-1, keepdims=True) * (1.0 / IDX_DIM)
        kin = kc * lax.rsqrt(var + NORM_EPS) * lng_ref[...] + lnb_ref[...]
        ki_ref[0] = kin[:, :IDX_DIM].astype(BF16)
        wt_ref[0] = zk.T[IDX_DIM:IDX_DIM + IDX_HEADS, :] * (IDX_HEADS ** -0.5)

    def attn_gate():
        zg = proj(C_GA, C_XR)
        sg_ref[0] = (zg * _sigmoid(zg)).astype(BF16)

    rnn_inputs(0)
    ga_ref[0] = _sigmoid(proj(C_MG, C_MG + D_MODEL) + bm_ref[:, :D_MODEL]).astype(BF16)
    rnn_inputs(1)
    gr_ref[...] = _sigmoid(proj(C_MG + D_MODEL, C_END) + bm_ref[:, D_MODEL:])
    rnn_inputs(2)
    zr = proj(C_GR, C_MG)
    szr_ref[...] = zr * _sigmoid(zr)
    rnn_inputs(3)

    row = lax.broadcasted_iota(jnp.int32, (SUBLANES, RNN_WIDTH), 0)

    hprev = hc_ref[...]
    for ci in range(TM // SUBLANES):
        rows = slice(ci * SUBLANES, (ci + 1) * SUBLANES)
        av = a_ref[rows, :]
        bv = b_ref[rows, :]
        for d in (1, 2, 4):
            a_sh = jnp.where(row >= d, pltpu.roll(av, d, 0), 1.0)
            b_sh = jnp.where(row >= d, pltpu.roll(bv, d, 0), 0.0)
            bv = av * b_sh + bv
            av = av * a_sh
        hv = av * hprev + bv
        h_ref[rows, :] = hv
        hprev = jnp.broadcast_to(hv[SUBLANES - 1:SUBLANES, :], (SUBLANES, RNN_WIDTH))
    hc_ref[...] = hprev

    attn_queries()
    attn_keys_and_indexer()
    attn_gate()

    u = (h_ref[...] * szr_ref[...]).astype(BF16)
    y_rnn = jnp.dot(u, wrp_ref[...], preferred_element_type=F32)
    r_ref[0] = (gr_ref[...] * y_rnn).astype(BF16)


def _attn_kernel(topk, lq_ref, c_ref, ct_ref, qit_ref, ki_ref, wt_ref, sg_ref, wuv_ref,
                 u_ref,
                 lhst_ref, kext_ref, vt_ref, pos_ref, sc_ref, s_ref, acc_ref):
    i = pl.program_id(1)
    seq = kext_ref.shape[0]
    n_kt = (i * TQ + TQ + TK - 1) // TK
    n_cc = (i * TQ + TQ + CC - 1) // CC

    @pl.when(jnp.logical_and(pl.program_id(0) == 0, i == 0))
    def _():
        er = lax.broadcasted_iota(jnp.int32, (TQ, TQ), 0)
        ec = lax.broadcasted_iota(jnp.int32, (TQ, TQ), 1)
        eye = jnp.where(er == ec, 1.0, 0.0).astype(BF16)
        for h in range(N_HEADS):
            lhst_ref[KV_LATENT:KV_LATENT + TQ, h * TQ:(h + 1) * TQ] = eye
        pos_ref[...] = lax.broadcasted_iota(jnp.int32, (seq, LANES), 0).astype(F32)
        rr = lax.broadcasted_iota(jnp.int32, (VROWS - KV_LATENT, TB), 0)
        for k in range(seq // TB):
            vt_ref[k, KV_LATENT:VROWS, :] = jnp.where(rr == 0, 1.0, 0.0).astype(BF16)

    @pl.when(i == 0)
    def _():
        kext_ref[:, 0:KV_LATENT] = c_ref[0]
        for k in range(seq // TK):
            vt_ref[k // (TB // TK), 0:KV_LATENT, (k % (TB // TK)) * TK:(k % (TB // TK) + 1) * TK] = (
                ct_ref[0, k])

    lhst_ref[0:KV_LATENT, :] = lq_ref[0, 0]

    qit = qit_ref[0, 0]
    wv = wt_ref[0]
    tpos = i * TQ + lax.broadcasted_iota(jnp.int32, (1, TQ), 1)

    def fold8(v, op):
        parts = [op(v[k * KC:(k + 1) * KC].reshape(KC // SUBLANES, SUBLANES, v.shape[1]), axis=0)
                 for k in range(v.shape[0] // KC)]
        return op(jnp.stack(parts), axis=0)

    n_big = n_kt // 2
    has_tail = n_kt % 2 == 1
    r_tail = pl.multiple_of(n_big * TB, TB)

    def score_block(r0, rows, carry):
        lo8, hi8, pos8, neg8 = carry
        prod = jnp.dot(ki_ref[0, pl.ds(r0, rows), :], qit, preferred_element_type=F32)
        sc = jnp.maximum(prod[:, 0:TQ], 0.0) * wv[0:1, :]
        for jh in range(1, IDX_HEADS):
            sc = sc + jnp.maximum(prod[:, jh * TQ:(jh + 1) * TQ], 0.0) * wv[jh:jh + 1, :]
        causal = (lax.broadcasted_iota(jnp.int32, (rows, TQ), 0) + r0) <= tpos
        sc_ref[pl.ds(r0, rows), :] = jnp.where(causal, sc, -jnp.inf)
        lo8 = jnp.minimum(lo8, fold8(jnp.where(causal, sc, jnp.inf), jnp.min))
        hi8 = jnp.maximum(hi8, fold8(jnp.where(causal, sc, -jnp.inf), jnp.max))
        scz = jnp.where(causal, sc, 0.0)
        pos8 = jnp.minimum(pos8, fold8(jnp.where(scz > 0.0, scz, jnp.inf), jnp.min))
        neg8 = jnp.maximum(neg8, fold8(jnp.where(scz < 0.0, scz, -jnp.inf), jnp.max))
        return lo8, hi8, pos8, neg8

    inf8 = jnp.full((SUBLANES, TQ), jnp.inf, F32)
    ext = lax.fori_loop(
        0, n_big, lambda k, cr: score_block(pl.multiple_of(k * TB, TB), TB, cr),
        (inf8, -inf8, inf8, -inf8))
    lo8, hi8, pos8, neg8 = lax.cond(
        has_tail, lambda cr: score_block(r_tail, TK, cr), lambda cr: cr, ext)

    @pl.when(n_kt * TK < n_cc * CC)
    def _():
        r0 = pl.multiple_of(n_kt * TK, TK)
        sc_ref[pl.ds(r0, TK), :] = jnp.full((TK, TQ), -jnp.inf, F32)

    lo_k = _f32_to_key(jnp.min(lo8, axis=0, keepdims=True))
    vmax = jnp.max(hi8, axis=0, keepdims=True)
    hi_k = jnp.where(jnp.abs(vmax) < F32_MIN_NORMAL, _f32_to_key(jnp.full_like(vmax, F32_MIN_NORMAL)),
                     _f32_to_key(vmax) + 1)
    kf = float(topk)

    def count_ge(thr):
        def body(cc, c8):
            r0 = pl.multiple_of(cc * CC, CC)
            xs = sc_ref[pl.ds(r0, CC), :]
            return c8 + fold8(jnp.where(xs >= thr, 1.0, 0.0), jnp.sum)
        c8 = lax.fori_loop(0, n_cc, body, jnp.zeros((SUBLANES, TQ), F32))
        return jnp.sum(c8, axis=0, keepdims=True)

    def midpoint(lo_, hi_):
        return (lo_ >> 1) + (hi_ >> 1) + (lo_ & hi_ & 1)

    def probe(carry, thr_k):
        lo_, hi_, clo_, done_ = carry
        cnt = count_ge(_key_to_f32(thr_k))
        ge = cnt >= kf
        up = jnp.logical_and(ge, thr_k > lo_)
        down = jnp.logical_and(jnp.logical_not(ge), thr_k < hi_)
        return (jnp.where(up, thr_k, lo_), jnp.where(down, thr_k, hi_),
                jnp.where(up, cnt, clo_), done_), ge

    def bisect(carry):
        return probe(carry, midpoint(carry[0], carry[1]))[0]

    def unresolved(carry):
        lo_, hi_, clo_, done_ = carry
        open_ = jnp.where(clo_ > kf, jnp.where(midpoint(lo_, hi_) != lo_, 1.0 - done_, 0.0), 0.0)
        return (jnp.max(open_) > 0.5).astype(jnp.int32)

    n_causal = (tpos[0:1, :] + 1).astype(F32)
    searching = (i + 1) * TQ > topk
    carry = (lo_k, hi_k, n_causal, jnp.zeros((1, TQ), F32))

    def zero_steps(cr):
        zero_k = jnp.zeros((1, TQ), jnp.int32)
        cr, ge0 = probe(cr, zero_k)
        pmin = jnp.min(pos8, axis=0, keepdims=True)
        nmax = jnp.max(neg8, axis=0, keepdims=True)
        near = jnp.where(ge0, pmin, nmax)
        has_near = jnp.abs(near) < jnp.inf
        near_k = jnp.where(has_near, _f32_to_key(near), zero_k)
        cr, ge1 = probe(cr, near_k)
        settled = jnp.logical_and(has_near, ge0 != ge1)
        return cr[0], cr[1], cr[2], jnp.where(settled, 1.0, 0.0)

    carry = lax.cond(searching, zero_steps, lambda cr: cr, carry)
    n_fixed = jnp.where(searching, BISECT_FIXED, 0)
    carry = lax.fori_loop(0, n_fixed, lambda _, cr: bisect(cr), carry)

    def search_more(state):
        it, _, cr = state
        for _ in range(BISECT_PER_CHECK):
            cr = bisect(cr)
        return it + 1, unresolved(cr), cr

    _, _, (lo_k, hi_k, c_lo, _) = lax.while_loop(
        lambda st: jnp.logical_and(st[1] > 0, st[0] < BISECT_MAX_CHECKS),
        search_more, (jnp.int32(0), unresolved(carry), carry))
    lo = _key_to_f32(lo_k)
    hi = _key_to_f32(hi_k)

    has_ties = jnp.max(jnp.where(c_lo > kf, 1.0, 0.0)) > 0.5

    @pl.when(jnp.logical_not(has_ties))
    def _():
        def mask_chunk(kt, _):
            r0 = pl.multiple_of(kt * TK, TK)
            xs = sc_ref[pl.ds(r0, TK), :]
            kext_ref[pl.ds(r0, TK), KV_LATENT:KV_LATENT + TQ] = (
                jnp.where(xs >= lo, 0.0, MASK_NEG).astype(BF16))
            return 0
        lax.fori_loop(0, n_kt, mask_chunk, 0)

    @pl.when(has_ties)
    def _():
        need = kf - count_ge(hi)
        tri = jnp.where(lax.broadcasted_iota(jnp.int32, (TK, TK), 1)
                        <= lax.broadcasted_iota(jnp.int32, (TK, TK), 0), 1.0, 0.0).astype(BF16)

        def mask_chunk(kt, before):
            r0 = pl.multiple_of(kt * TK, TK)
            xs = sc_ref[pl.ds(r0, TK), :]
            band = jnp.where(xs >= lo, jnp.where(xs < hi, 1.0, 0.0), 0.0)
            incl = jnp.dot(tri, band.astype(BF16), preferred_element_type=F32)
            rank = incl - band + before
            keep = jnp.where(xs >= hi, 1.0, jnp.where(rank < need, band, 0.0))
            kext_ref[pl.ds(r0, TK), KV_LATENT:KV_LATENT + TQ] = (
                jnp.where(keep > 0.0, 0.0, MASK_NEG).astype(BF16))
            return before + incl[TK - 1:TK, :]

        lax.fori_loop(0, n_kt, mask_chunk, jnp.zeros((1, TQ), F32))


    def qk_block(r0, rows, mx):
        st = jnp.dot(kext_ref[pl.ds(r0, rows), :], lhst_ref[...], preferred_element_type=F32)
        posb = pos_ref[pl.ds(r0, rows), :]
        tops = []
        for h in range(N_HEADS):
            sh = st[:, h * TQ:(h + 1) * TQ] + posb * (ALIBI_SLOPES[h] * LOG2E)
            s_ref[pl.ds(r0, rows), h * TQ:(h + 1) * TQ] = sh
            tops.append(fold8(sh, jnp.max))
        return jnp.maximum(mx, jnp.concatenate(tops, axis=1))

    mx = lax.fori_loop(0, n_big, lambda k, v: qk_block(pl.multiple_of(k * TB, TB), TB, v),
                       jnp.full((SUBLANES, N_HEADS * TQ), M_INIT, F32))
    mx = lax.cond(has_tail, lambda v: qk_block(r_tail, TK, v), lambda v: v, mx)
    m = jnp.max(mx, axis=0, keepdims=True)
    acc_ref[...] = jnp.zeros(acc_ref.shape, F32)

    def pv_block(r0, rows, vt):
        pt = jnp.exp2(s_ref[pl.ds(r0, rows), :] - m).astype(BF16)
        acc_ref[...] += jnp.dot(vt, pt, preferred_element_type=F32)

    def pv_big(k, _):
        pv_block(pl.multiple_of(k * TB, TB), TB, vt_ref[k])
        return 0

    lax.fori_loop(0, n_big, pv_big, 0)

    @pl.when(has_tail)
    def _():
        pv_block(r_tail, TK, vt_ref[n_big, :, 0:TK])

    o_parts = []
    for p in range(N_HEADS // 2):
        cols = slice(p * 2 * TQ, (p + 1) * 2 * TQ)
        xt = (acc_ref[0:KV_LATENT, cols] / acc_ref[KV_LATENT:KV_LATENT + 1, cols]).T
        pair = jnp.concatenate([xt[:TQ], xt[TQ:]], axis=1).astype(BF16)
        o_parts.append(jnp.dot(pair, wuv_ref[p], preferred_element_type=F32))
    o = jnp.concatenate(o_parts, axis=1)
    u_ref[0] = (o * sg_ref[0].astype(F32)).astype(BF16)


def _out_kernel(u_ref, ga_ref, r_ref, x_ref, wap_ref, wout_ref, fg_ref, out_ref):
    y_attn = jnp.dot(u_ref[0], wap_ref[...], preferred_element_type=F32)
    mixed = (ga_ref[0].astype(F32) * y_attn + r_ref[0].astype(F32)).astype(BF16)
    res = x_ref[0] + jnp.dot(mixed, wout_ref[...], preferred_element_type=F32)
    out_ref[0] = res * lax.rsqrt(jnp.mean(res * res, axis=-1, keepdims=True) + NORM_EPS) * fg_ref[...]


def _const_spec(shape):
    nd = len(shape)
    return pl.BlockSpec(shape, lambda b, i: (0,) * nd, pipeline_mode=pl.Buffered(1))


def _pack_w_in(w_in):
    offs = np.cumsum([0, ATTN_WIDTH, KV_LATENT, IDX_HEADS * IDX_DIM, IDX_DIM, IDX_HEADS,
                      ATTN_WIDTH, RNN_WIDTH, RNN_WIDTH, 2 * D_MODEL])
    seg = [w_in[:, offs[k]:offs[k + 1]] for k in range(9)]
    pad = jnp.zeros((D_MODEL, LANES - IDX_DIM - IDX_HEADS), w_in.dtype)
    return jnp.concatenate(seg[:5] + [pad] + seg[5:], axis=1).astype(BF16)


def _pair_block_diag(w):
    n2, r, c = w.shape
    w = w.reshape(n2 // 2, 2, r, c)
    z = jnp.zeros((n2 // 2, r, c), w.dtype)
    top = jnp.concatenate([w[:, 0], z], axis=2)
    bot = jnp.concatenate([z, w[:, 1]], axis=2)
    return jnp.concatenate([top, bot], axis=1)


def _layer(x, norm_g, w_in, b_merge, kv_g, w_uk, w_uv, ln_g, ln_b, w_ap, conv_w, conv_b,
           w_a, b_a, w_x, b_x, lam, w_rp, w_out, out_g):
    B, S, _ = x.shape
    assert S % TM == 0 and TM % TK == 0 and TK % TQ == 0 and TQ == KC
    assert S % CC == 0 and CC % TK == 0 and S % TB == 0
    topk = min(TOPK_MAX, S // 4)
    row = lambda v: v.reshape(1, -1).astype(F32)
    lane_pad = lambda v: jnp.pad(v.astype(F32), (0, LANES - v.shape[0])).reshape(1, LANES)

    w_all = _pack_w_in(w_in)
    wg = jnp.concatenate([_pair_block_diag(w_a), _pair_block_diag(w_x)], axis=2).astype(BF16)
    wuk2t = _pair_block_diag(jnp.swapaxes(w_uk, 1, 2)).astype(BF16)

    nq = S // TQ
    tile = lambda n: pl.BlockSpec((1, TM, n), lambda b, j: (b, j, 0))
    outs = pl.pallas_call(
        _proj_kernel,
        grid=(B, S // TM),
        in_specs=[
            tile(D_MODEL),
            _const_spec((1, D_MODEL)),
            _const_spec((D_MODEL, C_END)),
            _const_spec((N_HEADS // 2, 2 * HEAD_DIM, 2 * KV_LATENT)),
            _const_spec((1, KV_LATENT)),
            _const_spec((1, LANES)),
            _const_spec((1, LANES)),
            _const_spec((CONV_WIDTH, RNN_WIDTH)),
            _const_spec((1, RNN_WIDTH)),
            _const_spec((RNN_WIDTH // LANES, LANES, 2 * LANES)),
            _const_spec((1, RNN_WIDTH)),
            _const_spec((1, RNN_WIDTH)),
            _const_spec((1, RNN_WIDTH)),
            _const_spec((RNN_WIDTH, D_MODEL)),
            _const_spec((1, 2 * D_MODEL)),
        ],
        out_specs=[
            pl.BlockSpec((1, TM // TQ, KV_LATENT, N_HEADS * TQ), lambda b, j: (b, j, 0, 0)),
            tile(KV_LATENT),
            pl.BlockSpec((1, TM // TK, KV_LATENT, TK), lambda b, j: (b, j, 0, 0)),
            pl.BlockSpec((1, TM // TQ, IDX_DIM, IDX_HEADS * TQ), lambda b, j: (b, j, 0, 0)),
            tile(IDX_DIM),
            pl.BlockSpec((1, IDX_HEADS, TM), lambda b, j: (b, 0, j)),
            tile(ATTN_WIDTH),
            tile(D_MODEL),
            tile(D_MODEL),
        ],
        out_shape=[
            jax.ShapeDtypeStruct((B, nq, KV_LATENT, N_HEADS * TQ), BF16),
            jax.ShapeDtypeStruct((B, S, KV_LATENT), BF16),
            jax.ShapeDtypeStruct((B, S // TK, KV_LATENT, TK), BF16),
            jax.ShapeDtypeStruct((B, nq, IDX_DIM, IDX_HEADS * TQ), BF16),
            jax.ShapeDtypeStruct((B, S, IDX_DIM), BF16),
            jax.ShapeDtypeStruct((B, IDX_HEADS, S), F32),
            jax.ShapeDtypeStruct((B, S, ATTN_WIDTH), BF16),
            jax.ShapeDtypeStruct((B, S, D_MODEL), BF16),
            jax.ShapeDtypeStruct((B, S, D_MODEL), BF16),
        ],
        scratch_shapes=[
            pltpu.VMEM((TM + SUBLANES, RNN_WIDTH), F32),
            pltpu.VMEM((TM, RNN_WIDTH), F32),
            pltpu.VMEM((TM, RNN_WIDTH), F32),
            pltpu.VMEM((TM, RNN_WIDTH), F32),
            pltpu.VMEM((SUBLANES, RNN_WIDTH), F32),
            pltpu.VMEM((TM, RNN_WIDTH), F32),
            pltpu.VMEM((TM, D_MODEL), F32),
        ],
        compiler_params=pltpu.CompilerParams(
            dimension_semantics=("arbitrary", "arbitrary"),
            vmem_limit_bytes=VMEM_LIMIT_BYTES),
        name="proj_rglru",
    )(x, row(norm_g), w_all, wuk2t, row(kv_g), lane_pad(ln_g), lane_pad(ln_b), conv_w.astype(F32),
      row(conv_b), wg, row(b_a), row(b_x), row(lam), w_rp.astype(BF16), row(b_merge))
    lq, c, ct, qit, ki, wt, sg, ga, r = outs

    wuv2 = _pair_block_diag(w_uv).astype(BF16)

    qtile = lambda n: pl.BlockSpec((1, TQ, n), lambda b, i: (b, i, 0))
    u = pl.pallas_call(
        functools.partial(_attn_kernel, topk),
        grid=(B, nq),
        in_specs=[
            pl.BlockSpec((1, 1, KV_LATENT, N_HEADS * TQ), lambda b, i: (b, i, 0, 0)),
            pl.BlockSpec((1, S, KV_LATENT), lambda b, i: (b, 0, 0)),
            pl.BlockSpec((1, S // TK, KV_LATENT, TK), lambda b, i: (b, 0, 0, 0)),
            pl.BlockSpec((1, 1, IDX_DIM, IDX_HEADS * TQ), lambda b, i: (b, i, 0, 0)),
            pl.BlockSpec((1, S, IDX_DIM), lambda b, i: (b, 0, 0)),
            pl.BlockSpec((1, IDX_HEADS, TQ), lambda b, i: (b, 0, i)),
            qtile(ATTN_WIDTH),
            _const_spec((N_HEADS // 2, 2 * KV_LATENT, 2 * HEAD_DIM)),
        ],
        out_specs=qtile(ATTN_WIDTH),
        out_shape=jax.ShapeDtypeStruct((B, S, ATTN_WIDTH), BF16),
        scratch_shapes=[
            pltpu.VMEM((2 * KV_LATENT, N_HEADS * TQ), BF16),
            pltpu.VMEM((S, 2 * KV_LATENT), BF16),
            pltpu.VMEM((S // TB, VROWS, TB), BF16),
            pltpu.VMEM((S, LANES), F32),
            pltpu.VMEM((S, TQ), F32),
            pltpu.VMEM((S, N_HEADS * TQ), F32),
            pltpu.VMEM((VROWS, N_HEADS * TQ), F32),
        ],
        compiler_params=pltpu.CompilerParams(
            dimension_semantics=("arbitrary", "arbitrary"),
            vmem_limit_bytes=VMEM_LIMIT_BYTES),
        name="sparse_attn",
    )(lq, c, ct, qit, ki, wt, sg, wuv2)

    return pl.pallas_call(
        _out_kernel,
        grid=(B, S // TM),
        in_specs=[
            tile(ATTN_WIDTH),
            tile(D_MODEL),
            tile(D_MODEL),
            tile(D_MODEL),
            _const_spec((ATTN_WIDTH, D_MODEL)),
            _const_spec((D_MODEL, D_MODEL)),
            _const_spec((1, D_MODEL)),
        ],
        out_specs=tile(D_MODEL),
        out_shape=jax.ShapeDtypeStruct((B, S, D_MODEL), F32),
        compiler_params=pltpu.CompilerParams(
            dimension_semantics=("arbitrary", "arbitrary"),
            vmem_limit_bytes=VMEM_LIMIT_BYTES),
        name="merge_out",
    )(u, ga, r, x, w_ap.astype(BF16), w_out.astype(BF16), row(out_g))


def kernel(x, norm_gain, w_in, b_merge, kv_norm_gain, w_uk, w_uv, idx_ln_gain, idx_ln_bias,
           w_attn_proj, conv_w, conv_b, w_rg_a, b_rg_a, w_rg_x, b_rg_x, lru_lambda,
           w_rnn_proj, w_out, final_norm_gain):
    assert norm_gain.shape[0] == 1, "only the stated depth-1 stack is supported"
    return _layer(x, norm_gain[0], w_in[0], b_merge[0], kv_norm_gain[0], w_uk[0], w_uv[0],
                  idx_ln_gain[0], idx_ln_bias[0], w_attn_proj[0], conv_w[0], conv_b[0],
                  w_rg_a[0], b_rg_a[0], w_rg_x[0], b_rg_x[0], lru_lambda[0],
                  w_rnn_proj[0], w_out[0], final_norm_gain)
```

```python
import functools

import numpy as np
import jax
import jax.numpy as jnp
from jax import lax
from jax.experimental import pallas as pl
from jax.experimental.pallas import tpu as pltpu

D_MODEL = 1024
N_HEADS = 16
HEAD_DIM = 64
ATTN_WIDTH = N_HEADS * HEAD_DIM
KV_LATENT = 128
IDX_HEADS = 8
IDX_DIM = 64
TOPK_MAX = 256
RNN_WIDTH = 1024
RNN_BLOCKS = 16
RNN_BLOCK_DIM = RNN_WIDTH // RNN_BLOCKS
CONV_WIDTH = 4
LRU_C = 8.0
NORM_EPS = 1e-6

F32 = jnp.float32
BF16 = jnp.bfloat16

LANES = 128
SUBLANES = 8
VMEM_LIMIT_BYTES = 56 * 1024 * 1024

TM = 512
TQ = 128
KC = 128
TK = 256
CC = 512
VROWS = 144
BISECT_FIXED = 18
BISECT_PER_CHECK = 2
BISECT_MAX_CHECKS = 8
LOG2E = 1.4426950408889634
F32_MIN_NORMAL = 2.0 ** -126
MASK_NEG = -(2.0 ** 100)
M_INIT = -1e20

C_Q = 0
C_C = C_Q + ATTN_WIDTH
C_QI = C_C + KV_LATENT
C_KW = C_QI + IDX_HEADS * IDX_DIM
C_GA = C_KW + LANES
C_XR = C_GA + ATTN_WIDTH
C_GR = C_XR + RNN_WIDTH
C_MG = C_GR + RNN_WIDTH
C_END = C_MG + 2 * D_MODEL

ALIBI_SLOPES = [float(2.0 ** (-8.0 * (h + 1) / N_HEADS)) for h in range(N_HEADS)]


def _sigmoid_of_half(h):
    return 0.5 * jnp.tanh(h) + 0.5


def _silu(v):
    h = 0.5 * v
    return h * jnp.tanh(h) + h


def _f32_to_key(v):
    b = lax.bitcast_convert_type(v, jnp.int32)
    return b ^ ((b >> 31) & 0x7FFFFFFF)


def _key_to_f32(k):
    return lax.bitcast_convert_type(k ^ ((k >> 31) & 0x7FFFFFFF), F32)


def _proj_kernel(x_ref, ng_ref, w_ref, wuk_ref, kvg_ref, lng_ref, lnb_ref, cw_ref, cb_ref, wg_ref,
                 ba_ref, bx_ref, lam_ref, wrp_ref, bm_ref,
                 lq_ref, c_ref, ct_ref, qit_ref, ki_ref, wt_ref, sg_ref, ga_ref, r_ref,
                 xext_ref, a_ref, b_ref, h_ref, hc_ref, szr_ref, gr_ref):
    j = pl.program_id(1)

    @pl.when(j == 0)
    def _():
        xext_ref[0:SUBLANES, :] = jnp.zeros((SUBLANES, RNN_WIDTH), F32)
        hc_ref[...] = jnp.zeros((SUBLANES, RNN_WIDTH), F32)

    @pl.when(j > 0)
    def _():
        xext_ref[0:SUBLANES, :] = xext_ref[TM:TM + SUBLANES, :]

    x = x_ref[0]
    xn = x * lax.rsqrt(jnp.mean(x * x, axis=-1, keepdims=True) + NORM_EPS) * ng_ref[...]
    xb = xn.astype(BF16)

    def proj(lo, hi):
        return jnp.dot(xb, w_ref[:, lo:hi], preferred_element_type=F32)

    xext_ref[SUBLANES:SUBLANES + TM, :] = proj(C_XR, C_GR)

    def rnn_inputs(blk):
        cols = slice(blk * 2 * LANES, (blk + 1) * 2 * LANES)
        y = cb_ref[:, cols]
        for k in range(CONV_WIDTH):
            off = SUBLANES - (CONV_WIDTH - 1) + k
            y = y + cw_ref[k:k + 1, cols] * xext_ref[off:off + TM, cols]
        yb = y.astype(BF16)
        g = [jnp.dot(yb[:, h * LANES:(h + 1) * LANES], wg_ref[2 * blk + h],
                     preferred_element_type=F32) for h in range(2)]
        ta = jnp.tanh(jnp.concatenate([g[0][:, :LANES], g[1][:, :LANES]], axis=1) + ba_ref[:, cols])
        ig = _sigmoid_of_half(
            jnp.concatenate([g[0][:, LANES:], g[1][:, LANES:]], axis=1) + bx_ref[:, cols])
        nl = -lam_ref[:, cols]
        softplus = jnp.maximum(nl, 0.0) + jnp.log1p(jnp.exp(-jnp.abs(nl)))
        half_rate = (-0.5 * LRU_C) * softplus
        log_a = half_rate * ta + half_rate
        a_ref[:, cols] = jnp.exp(log_a)
        th = jnp.tanh(log_a)
        b_ref[:, cols] = jnp.sqrt(-2.0 * th / (1.0 - th)) * (ig * y)

    def attn_queries():
        qb16 = proj(C_Q, C_C).astype(BF16)
        ql = [jnp.dot(qb16[:, p * LANES:(p + 1) * LANES], wuk_ref[p], preferred_element_type=F32)
              for p in range(N_HEADS // 2)]
        qlt = (jnp.concatenate(ql, axis=1) * (HEAD_DIM ** -0.5 * LOG2E)).T
        for qb in range(TM // TQ):
            for h in range(N_HEADS):
                lq_ref[0, qb, :, h * TQ:(h + 1) * TQ] = (
                    qlt[h * KV_LATENT:(h + 1) * KV_LATENT, qb * TQ:(qb + 1) * TQ].astype(BF16))

    def attn_keys_and_indexer():
        zc = proj(C_C, C_QI)
        cn = zc * lax.rsqrt(jnp.mean(zc * zc, axis=-1, keepdims=True) + NORM_EPS) * kvg_ref[...]
        c_ref[0] = cn.astype(BF16)
        cnt = cn.T
        for k in range(TM // TK):
            ct_ref[0, k] = cnt[:, k * TK:(k + 1) * TK].astype(BF16)

        zqt = (proj(C_QI, C_KW) * (IDX_DIM ** -0.5)).T
        for qb in range(TM // TQ):
            for jh in range(IDX_HEADS):
                qit_ref[0, qb, :, jh * TQ:(jh + 1) * TQ] = (
                    zqt[jh * IDX_DIM:(jh + 1) * IDX_DIM, qb * TQ:(qb + 1) * TQ].astype(BF16))

        zk = proj(C_KW, C_GA)
        lane = lax.broadcasted_iota(jnp.int32, zk.shape, 1)
        is_k = lane < IDX_DIM
        mu = jnp.sum(jnp.where(is_k, zk, 0.0), axis=-1, keepdims=True) * (1.0 / IDX_DIM)
        kc = jnp.where(is_k, zk - mu, 0.0)
        var = jnp.sum(kc * kc, axis=-1, keepdims=True) * (1.0 / IDX_DIM)
        kin = kc * lax.rsqrt(var + NORM_EPS) * lng_ref[...] + lnb_ref[...]
        ki_ref[0] = kin[:, :IDX_DIM].astype(BF16)
        wt_ref[0] = zk.T[IDX_DIM:IDX_DIM + IDX_HEADS, :] * (IDX_HEADS ** -0.5)

    def attn_gate():
        zg = proj(C_GA, C_XR)
        sg_ref[0] = _silu(zg).astype(BF16)

    rnn_inputs(0)
    ga_ref[0] = _sigmoid_of_half(proj(C_MG, C_MG + D_MODEL) + bm_ref[:, :D_MODEL]).astype(BF16)
    rnn_inputs(1)
    gr_ref[...] = _sigmoid_of_half(proj(C_MG + D_MODEL, C_END) + bm_ref[:, D_MODEL:])
    rnn_inputs(2)
    zr = proj(C_GR, C_MG)
    szr_ref[...] = _silu(zr)
    rnn_inputs(3)

    row = lax.broadcasted_iota(jnp.int32, (SUBLANES, RNN_WIDTH), 0)

    hprev = hc_ref[...]
    for ci in range(TM // SUBLANES):
        rows = slice(ci * SUBLANES, (ci + 1) * SUBLANES)
        av = a_ref[rows, :]
        bv = b_ref[rows, :]
        for d in (1, 2, 4):
            a_sh = jnp.where(row >= d, pltpu.roll(av, d, 0), 1.0)
            b_sh = jnp.where(row >= d, pltpu.roll(bv, d, 0), 0.0)
            bv = av * b_sh + bv
            av = av * a_sh
        hv = av * hprev + bv
        h_ref[rows, :] = hv
        hprev = jnp.broadcast_to(hv[SUBLANES - 1:SUBLANES, :], (SUBLANES, RNN_WIDTH))
    hc_ref[...] = hprev

    attn_queries()
    attn_keys_and_indexer()
    attn_gate()

    u = (h_ref[...] * szr_ref[...]).astype(BF16)
    y_rnn = jnp.dot(u, wrp_ref[...], preferred_element_type=F32)
    r_ref[0] = (gr_ref[...] * y_rnn).astype(BF16)


def _attn_kernel(topk, lq_ref, c_ref, ct_ref, qit_ref, ki_ref, wt_ref, sg_ref, wuv_ref,
                 u_ref,
                 lhst_ref, kext_ref, vt_ref, pos_ref, sc_ref, s_ref, acc_ref):
    i = pl.program_id(1)
    seq = kext_ref.shape[0]
    n_kt = (i * TQ + TQ + TK - 1) // TK
    n_cc = (i * TQ + TQ + CC - 1) // CC

    @pl.when(jnp.logical_and(pl.program_id(0) == 0, i == 0))
    def _():
        er = lax.broadcasted_iota(jnp.int32, (TQ, TQ), 0)
        ec = lax.broadcasted_iota(jnp.int32, (TQ, TQ), 1)
        eye = jnp.where(er == ec, 1.0, 0.0).astype(BF16)
        for h in range(N_HEADS):
            lhst_ref[KV_LATENT:KV_LATENT + TQ, h * TQ:(h + 1) * TQ] = eye
        pos_ref[...] = lax.broadcasted_iota(jnp.int32, (seq, LANES), 0).astype(F32)
        rr = lax.broadcasted_iota(jnp.int32, (VROWS - KV_LATENT, TK), 0)
        for k in range(seq // TK):
            vt_ref[k, KV_LATENT:VROWS, :] = jnp.where(rr == 0, 1.0, 0.0).astype(BF16)

    @pl.when(i == 0)
    def _():
        kext_ref[:, 0:KV_LATENT] = c_ref[0]
        for k in range(seq // TK):
            vt_ref[k, 0:KV_LATENT, :] = ct_ref[0, k]

    lhst_ref[0:KV_LATENT, :] = lq_ref[0, 0]

    qit = qit_ref[0, 0]
    wv = wt_ref[0]
    tpos = i * TQ + lax.broadcasted_iota(jnp.int32, (1, TQ), 1)

    def fold8(v, op):
        parts = [op(v[k * KC:(k + 1) * KC].reshape(KC // SUBLANES, SUBLANES, v.shape[1]), axis=0)
                 for k in range(v.shape[0] // KC)]
        return op(jnp.stack(parts), axis=0)

    def walk(block_fn, carry):
        n4 = n_kt // 4
        carry = lax.fori_loop(
            0, n4, lambda k, cr: block_fn(pl.multiple_of(k * 4 * TK, 4 * TK), 4 * TK, k * 4, cr),
            carry)
        kt2 = n4 * 4
        has2 = n_kt % 4 >= 2
        carry = lax.cond(
            has2, lambda cr: block_fn(pl.multiple_of(kt2 * TK, 2 * TK), 2 * TK, kt2, cr),
            lambda cr: cr, carry)
        kt1 = kt2 + jnp.where(has2, 2, 0)
        return lax.cond(
            n_kt % 2 == 1, lambda cr: block_fn(pl.multiple_of(kt1 * TK, TK), TK, kt1, cr),
            lambda cr: cr, carry)

    def score_block(r0, rows, kt0, carry):
        lo8, hi8, pos8, neg8 = carry
        prod = jnp.dot(ki_ref[0, pl.ds(r0, rows), :], qit, preferred_element_type=F32)
        sc = jnp.maximum(prod[:, 0:TQ], 0.0) * wv[0:1, :]
        for jh in range(1, IDX_HEADS):
            sc = sc + jnp.maximum(prod[:, jh * TQ:(jh + 1) * TQ], 0.0) * wv[jh:jh + 1, :]
        causal = (lax.broadcasted_iota(jnp.int32, (rows, TQ), 0) + r0) <= tpos
        sc_ref[pl.ds(r0, rows), :] = jnp.where(causal, sc, -jnp.inf)
        lo8 = jnp.minimum(lo8, fold8(jnp.where(causal, sc, jnp.inf), jnp.min))
        hi8 = jnp.maximum(hi8, fold8(jnp.where(causal, sc, -jnp.inf), jnp.max))
        scz = jnp.where(causal, sc, 0.0)
        pos8 = jnp.minimum(pos8, fold8(jnp.where(scz > 0.0, scz, jnp.inf), jnp.min))
        neg8 = jnp.maximum(neg8, fold8(jnp.where(scz < 0.0, scz, -jnp.inf), jnp.max))
        return lo8, hi8, pos8, neg8

    inf8 = jnp.full((SUBLANES, TQ), jnp.inf, F32)
    lo8, hi8, pos8, neg8 = walk(score_block, (inf8, -inf8, inf8, -inf8))

    @pl.when(n_kt * TK < n_cc * CC)
    def _():
        r0 = pl.multiple_of(n_kt * TK, TK)
        sc_ref[pl.ds(r0, TK), :] = jnp.full((TK, TQ), -jnp.inf, F32)

    lo_k = _f32_to_key(jnp.min(lo8, axis=0, keepdims=True))
    vmax = jnp.max(hi8, axis=0, keepdims=True)
    hi_k = jnp.where(jnp.abs(vmax) < F32_MIN_NORMAL, _f32_to_key(jnp.full_like(vmax, F32_MIN_NORMAL)),
                     _f32_to_key(vmax) + 1)
    kf = float(topk)

    def count_ge(thr):
        def body(cc, c8):
            r0 = pl.multiple_of(cc * CC, CC)
            xs = sc_ref[pl.ds(r0, CC), :]
            return c8 + fold8(jnp.where(xs >= thr, 1.0, 0.0), jnp.sum)
        c8 = lax.fori_loop(0, n_cc, body, jnp.zeros((SUBLANES, TQ), F32))
        return jnp.sum(c8, axis=0, keepdims=True)

    def midpoint(lo_, hi_):
        return (lo_ >> 1) + (hi_ >> 1) + (lo_ & hi_ & 1)

    def probe(carry, thr_k):
        lo_, hi_, clo_, done_ = carry
        cnt = count_ge(_key_to_f32(thr_k))
        ge = cnt >= kf
        up = jnp.logical_and(ge, thr_k > lo_)
        down = jnp.logical_and(jnp.logical_not(ge), thr_k < hi_)
        return (jnp.where(up, thr_k, lo_), jnp.where(down, thr_k, hi_),
                jnp.where(up, cnt, clo_), done_), ge

    def bisect(carry):
        return probe(carry, midpoint(carry[0], carry[1]))[0]

    def unresolved(carry):
        lo_, hi_, clo_, done_ = carry
        open_ = jnp.where(clo_ > kf, jnp.where(midpoint(lo_, hi_) != lo_, 1.0 - done_, 0.0), 0.0)
        return (jnp.max(open_) > 0.5).astype(jnp.int32)

    n_causal = (tpos[0:1, :] + 1).astype(F32)
    searching = (i + 1) * TQ > topk
    carry = (lo_k, hi_k, n_causal, jnp.zeros((1, TQ), F32))

    def zero_steps(cr):
        zero_k = jnp.zeros((1, TQ), jnp.int32)
        cr, ge0 = probe(cr, zero_k)
        pmin = jnp.min(pos8, axis=0, keepdims=True)
        nmax = jnp.max(neg8, axis=0, keepdims=True)
        near = jnp.where(ge0, pmin, nmax)
        has_near = jnp.abs(near) < jnp.inf
        near_k = jnp.where(has_near, _f32_to_key(near), zero_k)
        cr, ge1 = probe(cr, near_k)
        settled = jnp.logical_and(has_near, ge0 != ge1)
        return cr[0], cr[1], cr[2], jnp.where(settled, 1.0, 0.0)

    carry = lax.cond(searching, zero_steps, lambda cr: cr, carry)
    n_fixed = jnp.where(searching, BISECT_FIXED, 0)
    carry = lax.fori_loop(0, n_fixed, lambda _, cr: bisect(cr), carry)

    def search_more(state):
        it, _, cr = state
        for _ in range(BISECT_PER_CHECK):
            cr = bisect(cr)
        return it + 1, unresolved(cr), cr

    _, _, (lo_k, hi_k, c_lo, _) = lax.while_loop(
        lambda st: jnp.logical_and(st[1] > 0, st[0] < BISECT_MAX_CHECKS),
        search_more, (jnp.int32(0), unresolved(carry), carry))
    lo = _key_to_f32(lo_k)
    hi = _key_to_f32(hi_k)

    has_ties = jnp.max(jnp.where(c_lo > kf, 1.0, 0.0)) > 0.5

    @pl.when(jnp.logical_not(has_ties))
    def _():
        def mask_chunk(kt, _):
            r0 = pl.multiple_of(kt * TK, TK)
            xs = sc_ref[pl.ds(r0, TK), :]
            kext_ref[pl.ds(r0, TK), KV_LATENT:KV_LATENT + TQ] = (
                jnp.where(xs >= lo, 0.0, MASK_NEG).astype(BF16))
            return 0
        lax.fori_loop(0, n_kt, mask_chunk, 0)

    @pl.when(has_ties)
    def _():
        need = kf - count_ge(hi)
        tri = jnp.where(lax.broadcasted_iota(jnp.int32, (TK, TK), 1)
                        <= lax.broadcasted_iota(jnp.int32, (TK, TK), 0), 1.0, 0.0).astype(BF16)

        def mask_chunk(kt, before):
            r0 = pl.multiple_of(kt * TK, TK)
            xs = sc_ref[pl.ds(r0, TK), :]
            band = jnp.where(xs >= lo, jnp.where(xs < hi, 1.0, 0.0), 0.0)
            incl = jnp.dot(tri, band.astype(BF16), preferred_element_type=F32)
            rank = incl - band + before
            keep = jnp.where(xs >= hi, 1.0, jnp.where(rank < need, band, 0.0))
            kext_ref[pl.ds(r0, TK), KV_LATENT:KV_LATENT + TQ] = (
                jnp.where(keep > 0.0, 0.0, MASK_NEG).astype(BF16))
            return before + incl[TK - 1:TK, :]

        lax.fori_loop(0, n_kt, mask_chunk, jnp.zeros((1, TQ), F32))


    def qk_block(r0, rows, kt0, mx):
        st = jnp.dot(kext_ref[pl.ds(r0, rows), :], lhst_ref[...], preferred_element_type=F32)
        posb = pos_ref[pl.ds(r0, rows), :]
        tops = []
        for h in range(N_HEADS):
            sh = st[:, h * TQ:(h + 1) * TQ] + posb * (ALIBI_SLOPES[h] * LOG2E)
            s_ref[pl.ds(r0, rows), h * TQ:(h + 1) * TQ] = sh
            tops.append(fold8(sh, jnp.max))
        return jnp.maximum(mx, jnp.concatenate(tops, axis=1))

    mx = walk(qk_block, jnp.full((SUBLANES, N_HEADS * TQ), M_INIT, F32))
    m = jnp.max(mx, axis=0, keepdims=True)
    acc_ref[...] = jnp.zeros(acc_ref.shape, F32)

    def pv_block(r0, rows, kt0, _):
        pt = jnp.exp2(s_ref[pl.ds(r0, rows), :] - m).astype(BF16)
        vt = jnp.concatenate([vt_ref[kt0 + k] for k in range(rows // TK)], axis=1)
        acc_ref[...] += jnp.dot(vt, pt, preferred_element_type=F32)
        return 0

    walk(pv_block, 0)

    o_parts = []
    for p in range(N_HEADS // 2):
        cols = slice(p * 2 * TQ, (p + 1) * 2 * TQ)
        xt = (acc_ref[0:KV_LATENT, cols] / acc_ref[KV_LATENT:KV_LATENT + 1, cols]).T
        pair = jnp.concatenate([xt[:TQ], xt[TQ:]], axis=1).astype(BF16)
        o_parts.append(jnp.dot(pair, wuv_ref[p], preferred_element_type=F32))
    o = jnp.concatenate(o_parts, axis=1)
    u_ref[0] = (o * sg_ref[0].astype(F32)).astype(BF16)


def _out_kernel(u_ref, ga_ref, r_ref, x_ref, wap_ref, wout_ref, fg_ref, out_ref):
    y_attn = jnp.dot(u_ref[0], wap_ref[...], preferred_element_type=F32)
    mixed = (ga_ref[0].astype(F32) * y_attn + r_ref[0].astype(F32)).astype(BF16)
    res = x_ref[0] + jnp.dot(mixed, wout_ref[...], preferred_element_type=F32)
    out_ref[0] = res * lax.rsqrt(jnp.mean(res * res, axis=-1, keepdims=True) + NORM_EPS) * fg_ref[...]


def _const_spec(shape):
    nd = len(shape)
    return pl.BlockSpec(shape, lambda b, i: (0,) * nd, pipeline_mode=pl.Buffered(1))


def _pack_w_in(w_in):
    offs = np.cumsum([0, ATTN_WIDTH, KV_LATENT, IDX_HEADS * IDX_DIM, IDX_DIM, IDX_HEADS,
                      ATTN_WIDTH, RNN_WIDTH, RNN_WIDTH, 2 * D_MODEL])
    seg = [w_in[:, offs[k]:offs[k + 1]] for k in range(9)]
    seg[8] = seg[8] * 0.5
    pad = jnp.zeros((D_MODEL, LANES - IDX_DIM - IDX_HEADS), w_in.dtype)
    return jnp.concatenate(seg[:5] + [pad] + seg[5:], axis=1).astype(BF16)


def _pair_block_diag(w):
    n2, r, c = w.shape
    w = w.reshape(n2 // 2, 2, r, c)
    z = jnp.zeros((n2 // 2, r, c), w.dtype)
    top = jnp.concatenate([w[:, 0], z], axis=2)
    bot = jnp.concatenate([z, w[:, 1]], axis=2)
    return jnp.concatenate([top, bot], axis=1)


def _layer(x, norm_g, w_in, b_merge, kv_g, w_uk, w_uv, ln_g, ln_b, w_ap, conv_w, conv_b,
           w_a, b_a, w_x, b_x, lam, w_rp, w_out, out_g):
    B, S, _ = x.shape
    assert S % TM == 0 and TM % TK == 0 and TK % TQ == 0 and TQ == KC
    assert S % CC == 0 and CC % TK == 0 and S % (4 * TK) == 0
    topk = min(TOPK_MAX, S // 4)
    row = lambda v: v.reshape(1, -1).astype(F32)
    lane_pad = lambda v: jnp.pad(v.astype(F32), (0, LANES - v.shape[0])).reshape(1, LANES)

    w_all = _pack_w_in(w_in)
    wg = (0.5 * jnp.concatenate([_pair_block_diag(w_a), _pair_block_diag(w_x)], axis=2)).astype(BF16)
    wuk2t = _pair_block_diag(jnp.swapaxes(w_uk, 1, 2)).astype(BF16)

    nq = S // TQ
    tile = lambda n: pl.BlockSpec((1, TM, n), lambda b, j: (b, j, 0))
    outs = pl.pallas_call(
        _proj_kernel,
        grid=(B, S // TM),
        in_specs=[
            tile(D_MODEL),
            _const_spec((1, D_MODEL)),
            _const_spec((D_MODEL, C_END)),
            _const_spec((N_HEADS // 2, 2 * HEAD_DIM, 2 * KV_LATENT)),
            _const_spec((1, KV_LATENT)),
            _const_spec((1, LANES)),
            _const_spec((1, LANES)),
            _const_spec((CONV_WIDTH, RNN_WIDTH)),
            _const_spec((1, RNN_WIDTH)),
            _const_spec((RNN_WIDTH // LANES, LANES, 2 * LANES)),
            _const_spec((1, RNN_WIDTH)),
            _const_spec((1, RNN_WIDTH)),
            _const_spec((1, RNN_WIDTH)),
            _const_spec((RNN_WIDTH, D_MODEL)),
            _const_spec((1, 2 * D_MODEL)),
        ],
        out_specs=[
            pl.BlockSpec((1, TM // TQ, KV_LATENT, N_HEADS * TQ), lambda b, j: (b, j, 0, 0)),
            tile(KV_LATENT),
            pl.BlockSpec((1, TM // TK, KV_LATENT, TK), lambda b, j: (b, j, 0, 0)),
            pl.BlockSpec((1, TM // TQ, IDX_DIM, IDX_HEADS * TQ), lambda b, j: (b, j, 0, 0)),
            tile(IDX_DIM),
            pl.BlockSpec((1, IDX_HEADS, TM), lambda b, j: (b, 0, j)),
            tile(ATTN_WIDTH),
            tile(D_MODEL),
            tile(D_MODEL),
        ],
        out_shape=[
            jax.ShapeDtypeStruct((B, nq, KV_LATENT, N_HEADS * TQ), BF16),
            jax.ShapeDtypeStruct((B, S, KV_LATENT), BF16),
            jax.ShapeDtypeStruct((B, S // TK, KV_LATENT, TK), BF16),
            jax.ShapeDtypeStruct((B, nq, IDX_DIM, IDX_HEADS * TQ), BF16),
            jax.ShapeDtypeStruct((B, S, IDX_DIM), BF16),
            jax.ShapeDtypeStruct((B, IDX_HEADS, S), F32),
            jax.ShapeDtypeStruct((B, S, ATTN_WIDTH), BF16),
            jax.ShapeDtypeStruct((B, S, D_MODEL), BF16),
            jax.ShapeDtypeStruct((B, S, D_MODEL), BF16),
        ],
        scratch_shapes=[
            pltpu.VMEM((TM + SUBLANES, RNN_WIDTH), F32),
            pltpu.VMEM((TM, RNN_WIDTH), F32),
            pltpu.VMEM((TM, RNN_WIDTH), F32),
            pltpu.VMEM((TM, RNN_WIDTH), F32),
            pltpu.VMEM((SUBLANES, RNN_WIDTH), F32),
            pltpu.VMEM((TM, RNN_WIDTH), F32),
            pltpu.VMEM((TM, D_MODEL), F32),
        ],
        compiler_params=pltpu.CompilerParams(
            dimension_semantics=("arbitrary", "arbitrary"),
            vmem_limit_bytes=VMEM_LIMIT_BYTES),
        name="proj_rglru",
    )(x, row(norm_g), w_all, wuk2t, row(kv_g), lane_pad(ln_g), lane_pad(ln_b), conv_w.astype(F32),
      row(conv_b), wg, row(0.5 * b_a), row(0.5 * b_x), row(lam), w_rp.astype(BF16),
      row(0.5 * b_merge))
    lq, c, ct, qit, ki, wt, sg, ga, r = outs

    wuv2 = _pair_block_diag(w_uv).astype(BF16)

    qtile = lambda n: pl.BlockSpec((1, TQ, n), lambda b, i: (b, i, 0))
    u = pl.pallas_call(
        functools.partial(_attn_kernel, topk),
        grid=(B, nq),
        in_specs=[
            pl.BlockSpec((1, 1, KV_LATENT, N_HEADS * TQ), lambda b, i: (b, i, 0, 0)),
            pl.BlockSpec((1, S, KV_LATENT), lambda b, i: (b, 0, 0)),
            pl.BlockSpec((1, S // TK, KV_LATENT, TK), lambda b, i: (b, 0, 0, 0)),
            pl.BlockSpec((1, 1, IDX_DIM, IDX_HEADS * TQ), lambda b, i: (b, i, 0, 0)),
            pl.BlockSpec((1, S, IDX_DIM), lambda b, i: (b, 0, 0)),
            pl.BlockSpec((1, IDX_HEADS, TQ), lambda b, i: (b, 0, i)),
            qtile(ATTN_WIDTH),
            _const_spec((N_HEADS // 2, 2 * KV_LATENT, 2 * HEAD_DIM)),
        ],
        out_specs=qtile(ATTN_WIDTH),
        out_shape=jax.ShapeDtypeStruct((B, S, ATTN_WIDTH), BF16),
        scratch_shapes=[
            pltpu.VMEM((2 * KV_LATENT, N_HEADS * TQ), BF16),
            pltpu.VMEM((S, 2 * KV_LATENT), BF16),
            pltpu.VMEM((S // TK, VROWS, TK), BF16),
            pltpu.VMEM((S, LANES), F32),
            pltpu.VMEM((S, TQ), F32),
            pltpu.VMEM((S, N_HEADS * TQ), F32),
            pltpu.VMEM((VROWS, N_HEADS * TQ), F32),
        ],
        compiler_params=pltpu.CompilerParams(
            dimension_semantics=("arbitrary", "arbitrary"),
            vmem_limit_bytes=VMEM_LIMIT_BYTES),
        name="sparse_attn",
    )(lq, c, ct, qit, ki, wt, sg, wuv2)

    return pl.pallas_call(
        _out_kernel,
        grid=(B, S // TM),
        in_specs=[
            tile(ATTN_WIDTH),
            tile(D_MODEL),
            tile(D_MODEL),
            tile(D_MODEL),
            _const_spec((ATTN_WIDTH, D_MODEL)),
            _const_spec((D_MODEL, D_MODEL)),
            _const_spec((1, D_MODEL)),
        ],
        out_specs=tile(D_MODEL),
        out_shape=jax.ShapeDtypeStruct((B, S, D_MODEL), F32),
        compiler_params=pltpu.CompilerParams(
            dimension_semantics=("arbitrary", "arbitrary"),
            vmem_limit_bytes=VMEM_LIMIT_BYTES),
        name="merge_out",
    )(u, ga, r, x, w_ap.astype(BF16), w_out.astype(BF16), row(out_g))


def kernel(x, norm_gain, w_in, b_merge, kv_norm_gain, w_uk, w_uv, idx_ln_gain, idx_ln_bias,
           w_attn_proj, conv_w, conv_b, w_rg_a, b_rg_a, w_rg_x, b_rg_x, lru_lambda,
           w_rnn_proj, w_out, final_norm_gain):
    assert norm_gain.shape[0] == 1, "only the stated depth-1 stack is supported"
    return _layer(x, norm_gain[0], w_in[0], b_merge[0], kv_norm_gain[0], w_uk[0], w_uv[0],
                  idx_ln_gain[0], idx_ln_bias[0], w_attn_proj[0], conv_w[0], conv_b[0],
                  w_rg_a[0], b_rg_a[0], w_rg_x[0], b_rg_x[0], lru_lambda[0],
                  w_rnn_proj[0], w_out[0], final_norm_gain)
```

```python
import functools

import numpy as np
import jax
import jax.numpy as jnp
from jax import lax
from jax.experimental import pallas as pl
from jax.experimental.pallas import tpu as pltpu

D_MODEL = 1024
N_HEADS = 16
HEAD_DIM = 64
ATTN_WIDTH = N_HEADS * HEAD_DIM
KV_LATENT = 128
IDX_HEADS = 8
IDX_DIM = 64
TOPK_MAX = 256
RNN_WIDTH = 1024
RNN_BLOCKS = 16
RNN_BLOCK_DIM = RNN_WIDTH // RNN_BLOCKS
CONV_WIDTH = 4
LRU_C = 8.0
NORM_EPS = 1e-6

F32 = jnp.float32
BF16 = jnp.bfloat16

LANES = 128
SUBLANES = 8
VMEM_LIMIT_BYTES = 56 * 1024 * 1024

TM = 512
TQ = 128
KC = 128
TK = 256
CC = 512
VROWS = 144
BISECT_FIXED = 18
BISECT_PER_CHECK = 2
BISECT_MAX_CHECKS = 8
LOG2E = 1.4426950408889634
F32_MIN_NORMAL = 2.0 ** -126
MASK_NEG = -(2.0 ** 100)
M_INIT = -1e20

C_Q = 0
C_C = C_Q + ATTN_WIDTH
C_QI = C_C + KV_LATENT
C_KW = C_QI + IDX_HEADS * IDX_DIM
C_GA = C_KW + LANES
C_XR = C_GA + ATTN_WIDTH
C_GR = C_XR + RNN_WIDTH
C_MG = C_GR + RNN_WIDTH
C_END = C_MG + 2 * D_MODEL

ALIBI_SLOPES = [float(2.0 ** (-8.0 * (h + 1) / N_HEADS)) for h in range(N_HEADS)]


def _sigmoid_of_half(h):
    return 0.5 * jnp.tanh(h) + 0.5


def _silu(v):
    h = 0.5 * v
    return h * jnp.tanh(h) + h


def _f32_to_key(v):
    b = lax.bitcast_convert_type(v, jnp.int32)
    return b ^ ((b >> 31) & 0x7FFFFFFF)


def _key_to_f32(k):
    return lax.bitcast_convert_type(k ^ ((k >> 31) & 0x7FFFFFFF), F32)


def _proj_kernel(x_ref, ng_ref, w_ref, wuk_ref, kvg_ref, lng_ref, lnb_ref, cw_ref, cb_ref, wg_ref,
                 ba_ref, bx_ref, lam_ref, wrp_ref, bm_ref,
                 lq_ref, c_ref, ct_ref, qit_ref, ki_ref, wt_ref, sg_ref, ga_ref, r_ref,
                 xext_ref, a_ref, b_ref, h_ref, hc_ref, szr_ref, gr_ref):
    j = pl.program_id(1)

    @pl.when(j == 0)
    def _():
        xext_ref[0:SUBLANES, :] = jnp.zeros((SUBLANES, RNN_WIDTH), F32)
        hc_ref[...] = jnp.zeros((SUBLANES, RNN_WIDTH), F32)

    @pl.when(j > 0)
    def _():
        xext_ref[0:SUBLANES, :] = xext_ref[TM:TM + SUBLANES, :]

    x = x_ref[0]
    xn = x * lax.rsqrt(jnp.mean(x * x, axis=-1, keepdims=True) + NORM_EPS) * ng_ref[...]
    xb = xn.astype(BF16)

    def proj(lo, hi):
        return jnp.dot(xb, w_ref[:, lo:hi], preferred_element_type=F32)

    xext_ref[SUBLANES:SUBLANES + TM, :] = proj(C_XR, C_GR)

    def rnn_inputs(blk):
        cols = slice(blk * 2 * LANES, (blk + 1) * 2 * LANES)
        xe = xext_ref[:, cols]
        y = cb_ref[:, cols]
        for k in range(CONV_WIDTH):
            d = CONV_WIDTH - 1 - k
            tap = xe if d == 0 else pltpu.roll(xe, d, 0)
            y = y + cw_ref[k:k + 1, cols] * tap[SUBLANES:SUBLANES + TM]
        yb = y.astype(BF16)
        g = [jnp.dot(yb[:, h * LANES:(h + 1) * LANES], wg_ref[2 * blk + h],
                     preferred_element_type=F32) for h in range(2)]
        ta = jnp.tanh(jnp.concatenate([g[0][:, :LANES], g[1][:, :LANES]], axis=1) + ba_ref[:, cols])
        ig = _sigmoid_of_half(
            jnp.concatenate([g[0][:, LANES:], g[1][:, LANES:]], axis=1) + bx_ref[:, cols])
        nl = -lam_ref[:, cols]
        softplus = jnp.maximum(nl, 0.0) + jnp.log1p(jnp.exp(-jnp.abs(nl)))
        half_rate = (-0.5 * LRU_C) * softplus
        log_a = half_rate * ta + half_rate
        a_ref[:, cols] = jnp.exp(log_a)
        th = jnp.tanh(log_a)
        b_ref[:, cols] = jnp.sqrt(-2.0 * th / (1.0 - th)) * (ig * y)

    def attn_queries():
        qb16 = proj(C_Q, C_C).astype(BF16)
        ql = [jnp.dot(qb16[:, p * LANES:(p + 1) * LANES], wuk_ref[p], preferred_element_type=F32)
              for p in range(N_HEADS // 2)]
        qlt = (jnp.concatenate(ql, axis=1) * (HEAD_DIM ** -0.5 * LOG2E)).T
        for qb in range(TM // TQ):
            for h in range(N_HEADS):
                lq_ref[0, qb, :, h * TQ:(h + 1) * TQ] = (
                    qlt[h * KV_LATENT:(h + 1) * KV_LATENT, qb * TQ:(qb + 1) * TQ].astype(BF16))

    def attn_keys_and_indexer():
        zc = proj(C_C, C_QI)
        cn = zc * lax.rsqrt(jnp.mean(zc * zc, axis=-1, keepdims=True) + NORM_EPS) * kvg_ref[...]
        c_ref[0] = cn.astype(BF16)
        cnt = cn.T
        for k in range(TM // TK):
            ct_ref[0, k] = cnt[:, k * TK:(k + 1) * TK].astype(BF16)

        zqt = (proj(C_QI, C_KW) * (IDX_DIM ** -0.5)).T
        for qb in range(TM // TQ):
            for jh in range(IDX_HEADS):
                qit_ref[0, qb, :, jh * TQ:(jh + 1) * TQ] = (
                    zqt[jh * IDX_DIM:(jh + 1) * IDX_DIM, qb * TQ:(qb + 1) * TQ].astype(BF16))

        zk = proj(C_KW, C_GA)
        lane = lax.broadcasted_iota(jnp.int32, zk.shape, 1)
        is_k = lane < IDX_DIM
        mu = jnp.sum(jnp.where(is_k, zk, 0.0), axis=-1, keepdims=True) * (1.0 / IDX_DIM)
        kc = jnp.where(is_k, zk - mu, 0.0)
        var = jnp.sum(kc * kc, axis=-1, keepdims=True) * (1.0 / IDX_DIM)
        kin = kc * lax.rsqrt(var + NORM_EPS) * lng_ref[...] + lnb_ref[...]
        ki_ref[0] = kin[:, :IDX_DIM].astype(BF16)
        wt_ref[0] = zk.T[IDX_DIM:IDX_DIM + IDX_HEADS, :] * (IDX_HEADS ** -0.5)

    def attn_gate():
        zg = proj(C_GA, C_XR)
        sg_ref[0] = _silu(zg).astype(BF16)

    rnn_inputs(0)
    ga_ref[0] = _sigmoid_of_half(proj(C_MG, C_MG + D_MODEL) + bm_ref[:, :D_MODEL]).astype(BF16)
    rnn_inputs(1)
    gr_ref[...] = _sigmoid_of_half(proj(C_MG + D_MODEL, C_END) + bm_ref[:, D_MODEL:])
    rnn_inputs(2)
    zr = proj(C_GR, C_MG)
    szr_ref[...] = _silu(zr)
    rnn_inputs(3)

    row = lax.broadcasted_iota(jnp.int32, (SUBLANES, RNN_WIDTH), 0)

    hprev = hc_ref[...]
    for ci in range(TM // SUBLANES):
        rows = slice(ci * SUBLANES, (ci + 1) * SUBLANES)
        av = a_ref[rows, :]
        bv = b_ref[rows, :]
        for d in (1, 2, 4):
            a_sh = jnp.where(row >= d, pltpu.roll(av, d, 0), 1.0)
            b_sh = jnp.where(row >= d, pltpu.roll(bv, d, 0), 0.0)
            bv = av * b_sh + bv
            av = av * a_sh
        hv = av * hprev + bv
        h_ref[rows, :] = hv
        hprev = jnp.broadcast_to(hv[SUBLANES - 1:SUBLANES, :], (SUBLANES, RNN_WIDTH))
    hc_ref[...] = hprev

    attn_queries()
    attn_keys_and_indexer()
    attn_gate()

    u = (h_ref[...] * szr_ref[...]).astype(BF16)
    y_rnn = jnp.dot(u, wrp_ref[...], preferred_element_type=F32)
    r_ref[0] = (gr_ref[...] * y_rnn).astype(BF16)


def _attn_kernel(topk, lq_ref, c_ref, ct_ref, qit_ref, ki_ref, wt_ref, sg_ref, wuv_ref,
                 u_ref,
                 lhst_ref, kext_ref, vt_ref, pos_ref, sc_ref, s_ref, acc_ref):
    i = pl.program_id(1)
    seq = kext_ref.shape[0]
    n_kt = (i * TQ + TQ + TK - 1) // TK
    n_cc = (i * TQ + TQ + CC - 1) // CC

    @pl.when(jnp.logical_and(pl.program_id(0) == 0, i == 0))
    def _():
        er = lax.broadcasted_iota(jnp.int32, (TQ, TQ), 0)
        ec = lax.broadcasted_iota(jnp.int32, (TQ, TQ), 1)
        eye = jnp.where(er == ec, 1.0, 0.0).astype(BF16)
        for h in range(N_HEADS):
            lhst_ref[KV_LATENT:KV_LATENT + TQ, h * TQ:(h + 1) * TQ] = eye
        pos_ref[...] = lax.broadcasted_iota(jnp.int32, (seq, LANES), 0).astype(F32)
        rr = lax.broadcasted_iota(jnp.int32, (VROWS - KV_LATENT, TK), 0)
        for k in range(seq // TK):
            vt_ref[k, KV_LATENT:VROWS, :] = jnp.where(rr == 0, 1.0, 0.0).astype(BF16)

    @pl.when(i == 0)
    def _():
        kext_ref[:, 0:KV_LATENT] = c_ref[0]
        for k in range(seq // TK):
            vt_ref[k, 0:KV_LATENT, :] = ct_ref[0, k]

    lhst_ref[0:KV_LATENT, :] = lq_ref[0, 0]

    qit = qit_ref[0, 0]
    wv = wt_ref[0]
    tpos = i * TQ + lax.broadcasted_iota(jnp.int32, (1, TQ), 1)

    def fold8(v, op):
        parts = [op(v[k * KC:(k + 1) * KC].reshape(KC // SUBLANES, SUBLANES, v.shape[1]), axis=0)
                 for k in range(v.shape[0] // KC)]
        return op(jnp.stack(parts), axis=0)

    def walk(block_fn, carry):
        n4 = n_kt // 4
        carry = lax.fori_loop(
            0, n4, lambda k, cr: block_fn(pl.multiple_of(k * 4 * TK, 4 * TK), 4 * TK, k * 4, cr),
            carry)
        kt2 = n4 * 4
        has2 = n_kt % 4 >= 2
        carry = lax.cond(
            has2, lambda cr: block_fn(pl.multiple_of(kt2 * TK, 2 * TK), 2 * TK, kt2, cr),
            lambda cr: cr, carry)
        kt1 = kt2 + jnp.where(has2, 2, 0)
        return lax.cond(
            n_kt % 2 == 1, lambda cr: block_fn(pl.multiple_of(kt1 * TK, TK), TK, kt1, cr),
            lambda cr: cr, carry)

    def score_block(r0, rows, kt0, carry):
        lo8, hi8, pos8, neg8 = carry
        prod = jnp.dot(ki_ref[0, pl.ds(r0, rows), :], qit, preferred_element_type=F32)
        sc = jnp.maximum(prod[:, 0:TQ], 0.0) * wv[0:1, :]
        for jh in range(1, IDX_HEADS):
            sc = sc + jnp.maximum(prod[:, jh * TQ:(jh + 1) * TQ], 0.0) * wv[jh:jh + 1, :]
        causal = (lax.broadcasted_iota(jnp.int32, (rows, TQ), 0) + r0) <= tpos
        sc_ref[pl.ds(r0, rows), :] = jnp.where(causal, sc, -jnp.inf)
        lo8 = jnp.minimum(lo8, fold8(jnp.where(causal, sc, jnp.inf), jnp.min))
        hi8 = jnp.maximum(hi8, fold8(jnp.where(causal, sc, -jnp.inf), jnp.max))
        scz = jnp.where(causal, sc, 0.0)
        pos8 = jnp.minimum(pos8, fold8(jnp.where(scz > 0.0, scz, jnp.inf), jnp.min))
        neg8 = jnp.maximum(neg8, fold8(jnp.where(scz < 0.0, scz, -jnp.inf), jnp.max))
        return lo8, hi8, pos8, neg8

    inf8 = jnp.full((SUBLANES, TQ), jnp.inf, F32)
    lo8, hi8, pos8, neg8 = walk(score_block, (inf8, -inf8, inf8, -inf8))

    @pl.when(n_kt * TK < n_cc * CC)
    def _():
        r0 = pl.multiple_of(n_kt * TK, TK)
        sc_ref[pl.ds(r0, TK), :] = jnp.full((TK, TQ), -jnp.inf, F32)

    lo_k = _f32_to_key(jnp.min(lo8, axis=0, keepdims=True))
    vmax = jnp.max(hi8, axis=0, keepdims=True)
    hi_k = jnp.where(jnp.abs(vmax) < F32_MIN_NORMAL, _f32_to_key(jnp.full_like(vmax, F32_MIN_NORMAL)),
                     _f32_to_key(vmax) + 1)
    kf = float(topk)

    def count_ge(thr):
        def body(cc, c8):
            r0 = pl.multiple_of(cc * CC, CC)
            xs = sc_ref[pl.ds(r0, CC), :]
            return c8 + fold8(jnp.where(xs >= thr, 1.0, 0.0), jnp.sum)
        c8 = lax.fori_loop(0, n_cc, body, jnp.zeros((SUBLANES, TQ), F32))
        return jnp.sum(c8, axis=0, keepdims=True)

    def midpoint(lo_, hi_):
        return (lo_ >> 1) + (hi_ >> 1) + (lo_ & hi_ & 1)

    def probe(carry, thr_k):
        lo_, hi_, clo_, done_ = carry
        cnt = count_ge(_key_to_f32(thr_k))
        ge = cnt >= kf
        up = jnp.logical_and(ge, thr_k > lo_)
        down = jnp.logical_and(jnp.logical_not(ge), thr_k < hi_)
        return (jnp.where(up, thr_k, lo_), jnp.where(down, thr_k, hi_),
                jnp.where(up, cnt, clo_), done_), ge

    def bisect(carry):
        return probe(carry, midpoint(carry[0], carry[1]))[0]

    def unresolved(carry):
        lo_, hi_, clo_, done_ = carry
        open_ = jnp.where(clo_ > kf, jnp.where(midpoint(lo_, hi_) != lo_, 1.0 - done_, 0.0), 0.0)
        return (jnp.max(open_) > 0.5).astype(jnp.int32)

    n_causal = (tpos[0:1, :] + 1).astype(F32)
    searching = (i + 1) * TQ > topk
    carry = (lo_k, hi_k, n_causal, jnp.zeros((1, TQ), F32))

    def zero_steps(cr):
        zero_k = jnp.zeros((1, TQ), jnp.int32)
        cr, ge0 = probe(cr, zero_k)
        pmin = jnp.min(pos8, axis=0, keepdims=True)
        nmax = jnp.max(neg8, axis=0, keepdims=True)
        near = jnp.where(ge0, pmin, nmax)
        has_near = jnp.abs(near) < jnp.inf
        near_k = jnp.where(has_near, _f32_to_key(near), zero_k)
        cr, ge1 = probe(cr, near_k)
        settled = jnp.logical_and(has_near, ge0 != ge1)
        return cr[0], cr[1], cr[2], jnp.where(settled, 1.0, 0.0)

    carry = lax.cond(searching, zero_steps, lambda cr: cr, carry)
    n_fixed = jnp.where(searching, BISECT_FIXED, 0)
    carry = lax.fori_loop(0, n_fixed, lambda _, cr: bisect(cr), carry)

    def search_more(state):
        it, _, cr = state
        for _ in range(BISECT_PER_CHECK):
            cr = bisect(cr)
        return it + 1, unresolved(cr), cr

    _, _, (lo_k, hi_k, c_lo, _) = lax.while_loop(
        lambda st: jnp.logical_and(st[1] > 0, st[0] < BISECT_MAX_CHECKS),
        search_more, (jnp.int32(0), unresolved(carry), carry))
    lo = _key_to_f32(lo_k)
    hi = _key_to_f32(hi_k)

    has_ties = jnp.max(jnp.where(c_lo > kf, 1.0, 0.0)) > 0.5

    @pl.when(jnp.logical_not(has_ties))
    def _():
        def mask_chunk(kt, _):
            r0 = pl.multiple_of(kt * TK, TK)
            xs = sc_ref[pl.ds(r0, TK), :]
            kext_ref[pl.ds(r0, TK), KV_LATENT:KV_LATENT + TQ] = (
                jnp.where(xs >= lo, 0.0, MASK_NEG).astype(BF16))
            return 0
        lax.fori_loop(0, n_kt, mask_chunk, 0)

    @pl.when(has_ties)
    def _():
        need = kf - count_ge(hi)
        tri = jnp.where(lax.broadcasted_iota(jnp.int32, (TK, TK), 1)
                        <= lax.broadcasted_iota(jnp.int32, (TK, TK), 0), 1.0, 0.0).astype(BF16)

        def mask_chunk(kt, before):
            r0 = pl.multiple_of(kt * TK, TK)
            xs = sc_ref[pl.ds(r0, TK), :]
            band = jnp.where(xs >= lo, jnp.where(xs < hi, 1.0, 0.0), 0.0)
            incl = jnp.dot(tri, band.astype(BF16), preferred_element_type=F32)
            rank = incl - band + before
            keep = jnp.where(xs >= hi, 1.0, jnp.where(rank < need, band, 0.0))
            kext_ref[pl.ds(r0, TK), KV_LATENT:KV_LATENT + TQ] = (
                jnp.where(keep > 0.0, 0.0, MASK_NEG).astype(BF16))
            return before + incl[TK - 1:TK, :]

        lax.fori_loop(0, n_kt, mask_chunk, jnp.zeros((1, TQ), F32))


    def qk_block(r0, rows, kt0, mx):
        st = jnp.dot(kext_ref[pl.ds(r0, rows), :], lhst_ref[...], preferred_element_type=F32)
        posb = pos_ref[pl.ds(r0, rows), :]
        tops = []
        for h in range(N_HEADS):
            sh = st[:, h * TQ:(h + 1) * TQ] + posb * (ALIBI_SLOPES[h] * LOG2E)
            s_ref[pl.ds(r0, rows), h * TQ:(h + 1) * TQ] = sh
            tops.append(fold8(sh, jnp.max))
        return jnp.maximum(mx, jnp.concatenate(tops, axis=1))

    mx = walk(qk_block, jnp.full((SUBLANES, N_HEADS * TQ), M_INIT, F32))
    m = jnp.max(mx, axis=0, keepdims=True)
    acc_ref[...] = jnp.zeros(acc_ref.shape, F32)

    def pv_block(r0, rows, kt0, _):
        pt = jnp.exp2(s_ref[pl.ds(r0, rows), :] - m).astype(BF16)
        vt = jnp.concatenate([vt_ref[kt0 + k] for k in range(rows // TK)], axis=1)
        acc_ref[...] += jnp.dot(vt, pt, preferred_element_type=F32)
        return 0

    walk(pv_block, 0)

    o_parts = []
    for p in range(N_HEADS // 2):
        cols = slice(p * 2 * TQ, (p + 1) * 2 * TQ)
        xt = (acc_ref[0:KV_LATENT, cols] / acc_ref[KV_LATENT:KV_LATENT + 1, cols]).T
        pair = jnp.concatenate([xt[:TQ], xt[TQ:]], axis=1).astype(BF16)
        o_parts.append(jnp.dot(pair, wuv_ref[p], preferred_element_type=F32))
    o = jnp.concatenate(o_parts, axis=1)
    u_ref[0] = (o * sg_ref[0].astype(F32)).astype(BF16)


def _out_kernel(u_ref, ga_ref, r_ref, x_ref, wap_ref, wout_ref, fg_ref, out_ref):
    y_attn = jnp.dot(u_ref[0], wap_ref[...], preferred_element_type=F32)
    mixed = (ga_ref[0].astype(F32) * y_attn + r_ref[0].astype(F32)).astype(BF16)
    res = x_ref[0] + jnp.dot(mixed, wout_ref[...], preferred_element_type=F32)
    out_ref[0] = res * lax.rsqrt(jnp.mean(res * res, axis=-1, keepdims=True) + NORM_EPS) * fg_ref[...]


def _const_spec(shape):
    nd = len(shape)
    return pl.BlockSpec(shape, lambda b, i: (0,) * nd, pipeline_mode=pl.Buffered(1))


def _pack_w_in(w_in):
    offs = np.cumsum([0, ATTN_WIDTH, KV_LATENT, IDX_HEADS * IDX_DIM, IDX_DIM, IDX_HEADS,
                      ATTN_WIDTH, RNN_WIDTH, RNN_WIDTH, 2 * D_MODEL])
    seg = [w_in[:, offs[k]:offs[k + 1]] for k in range(9)]
    seg[8] = seg[8] * 0.5
    pad = jnp.zeros((D_MODEL, LANES - IDX_DIM - IDX_HEADS), w_in.dtype)
    return jnp.concatenate(seg[:5] + [pad] + seg[5:], axis=1).astype(BF16)


def _pair_block_diag(w):
    n2, r, c = w.shape
    w = w.reshape(n2 // 2, 2, r, c)
    z = jnp.zeros((n2 // 2, r, c), w.dtype)
    top = jnp.concatenate([w[:, 0], z], axis=2)
    bot = jnp.concatenate([z, w[:, 1]], axis=2)
    return jnp.concatenate([top, bot], axis=1)


def _layer(x, norm_g, w_in, b_merge, kv_g, w_uk, w_uv, ln_g, ln_b, w_ap, conv_w, conv_b,
           w_a, b_a, w_x, b_x, lam, w_rp, w_out, out_g):
    B, S, _ = x.shape
    assert S % TM == 0 and TM % TK == 0 and TK % TQ == 0 and TQ == KC
    assert S % CC == 0 and CC % TK == 0 and S % (4 * TK) == 0
    topk = min(TOPK_MAX, S // 4)
    row = lambda v: v.reshape(1, -1).astype(F32)
    lane_pad = lambda v: jnp.pad(v.astype(F32), (0, LANES - v.shape[0])).reshape(1, LANES)

    w_all = _pack_w_in(w_in)
    wg = (0.5 * jnp.concatenate([_pair_block_diag(w_a), _pair_block_diag(w_x)], axis=2)).astype(BF16)
    wuk2t = _pair_block_diag(jnp.swapaxes(w_uk, 1, 2)).astype(BF16)

    nq = S // TQ
    tile = lambda n: pl.BlockSpec((1, TM, n), lambda b, j: (b, j, 0))
    outs = pl.pallas_call(
        _proj_kernel,
        grid=(B, S // TM),
        in_specs=[
            tile(D_MODEL),
            _const_spec((1, D_MODEL)),
            _const_spec((D_MODEL, C_END)),
            _const_spec((N_HEADS // 2, 2 * HEAD_DIM, 2 * KV_LATENT)),
            _const_spec((1, KV_LATENT)),
            _const_spec((1, LANES)),
            _const_spec((1, LANES)),
            _const_spec((CONV_WIDTH, RNN_WIDTH)),
            _const_spec((1, RNN_WIDTH)),
            _const_spec((RNN_WIDTH // LANES, LANES, 2 * LANES)),
            _const_spec((1, RNN_WIDTH)),
            _const_spec((1, RNN_WIDTH)),
            _const_spec((1, RNN_WIDTH)),
            _const_spec((RNN_WIDTH, D_MODEL)),
            _const_spec((1, 2 * D_MODEL)),
        ],
        out_specs=[
            pl.BlockSpec((1, TM // TQ, KV_LATENT, N_HEADS * TQ), lambda b, j: (b, j, 0, 0)),
            tile(KV_LATENT),
            pl.BlockSpec((1, TM // TK, KV_LATENT, TK), lambda b, j: (b, j, 0, 0)),
            pl.BlockSpec((1, TM // TQ, IDX_DIM, IDX_HEADS * TQ), lambda b, j: (b, j, 0, 0)),
            tile(IDX_DIM),
            pl.BlockSpec((1, IDX_HEADS, TM), lambda b, j: (b, 0, j)),
            tile(ATTN_WIDTH),
            tile(D_MODEL),
            tile(D_MODEL),
        ],
        out_shape=[
            jax.ShapeDtypeStruct((B, nq, KV_LATENT, N_HEADS * TQ), BF16),
            jax.ShapeDtypeStruct((B, S, KV_LATENT), BF16),
            jax.ShapeDtypeStruct((B, S // TK, KV_LATENT, TK), BF16),
            jax.ShapeDtypeStruct((B, nq, IDX_DIM, IDX_HEADS * TQ), BF16),
            jax.ShapeDtypeStruct((B, S, IDX_DIM), BF16),
            jax.ShapeDtypeStruct((B, IDX_HEADS, S), F32),
            jax.ShapeDtypeStruct((B, S, ATTN_WIDTH), BF16),
            jax.ShapeDtypeStruct((B, S, D_MODEL), BF16),
            jax.ShapeDtypeStruct((B, S, D_MODEL), BF16),
        ],
        scratch_shapes=[
            pltpu.VMEM((TM + SUBLANES, RNN_WIDTH), F32),
            pltpu.VMEM((TM, RNN_WIDTH), F32),
            pltpu.VMEM((TM, RNN_WIDTH), F32),
            pltpu.VMEM((TM, RNN_WIDTH), F32),
            pltpu.VMEM((SUBLANES, RNN_WIDTH), F32),
            pltpu.VMEM((TM, RNN_WIDTH), F32),
            pltpu.VMEM((TM, D_MODEL), F32),
        ],
        compiler_params=pltpu.CompilerParams(
            dimension_semantics=("arbitrary", "arbitrary"),
            vmem_limit_bytes=VMEM_LIMIT_BYTES),
        name="proj_rglru",
    )(x, row(norm_g), w_all, wuk2t, row(kv_g), lane_pad(ln_g), lane_pad(ln_b), conv_w.astype(F32),
      row(conv_b), wg, row(0.5 * b_a), row(0.5 * b_x), row(lam), w_rp.astype(BF16),
      row(0.5 * b_merge))
    lq, c, ct, qit, ki, wt, sg, ga, r = outs

    wuv2 = _pair_block_diag(w_uv).astype(BF16)

    qtile = lambda n: pl.BlockSpec((1, TQ, n), lambda b, i: (b, i, 0))
    u = pl.pallas_call(
        functools.partial(_attn_kernel, topk),
        grid=(B, nq),
        in_specs=[
            pl.BlockSpec((1, 1, KV_LATENT, N_HEADS * TQ), lambda b, i: (b, i, 0, 0)),
            pl.BlockSpec((1, S, KV_LATENT), lambda b, i: (b, 0, 0)),
            pl.BlockSpec((1, S // TK, KV_LATENT, TK), lambda b, i: (b, 0, 0, 0)),
            pl.BlockSpec((1, 1, IDX_DIM, IDX_HEADS * TQ), lambda b, i: (b, i, 0, 0)),
            pl.BlockSpec((1, S, IDX_DIM), lambda b, i: (b, 0, 0)),
            pl.BlockSpec((1, IDX_HEADS, TQ), lambda b, i: (b, 0, i)),
            qtile(ATTN_WIDTH),
            _const_spec((N_HEADS // 2, 2 * KV_LATENT, 2 * HEAD_DIM)),
        ],
        out_specs=qtile(ATTN_WIDTH),
        out_shape=jax.ShapeDtypeStruct((B, S, ATTN_WIDTH), BF16),
        scratch_shapes=[
            pltpu.VMEM((2 * KV_LATENT, N_HEADS * TQ), BF16),
            pltpu.VMEM((S, 2 * KV_LATENT), BF16),
            pltpu.VMEM((S // TK, VROWS, TK), BF16),
            pltpu.VMEM((S, LANES), F32),
            pltpu.VMEM((S, TQ), F32),
            pltpu.VMEM((S, N_HEADS * TQ), F32),
            pltpu.VMEM((VROWS, N_HEADS * TQ), F32),
        ],
        compiler_params=pltpu.CompilerParams(
            dimension_semantics=("arbitrary", "arbitrary"),
            vmem_limit_bytes=VMEM_LIMIT_BYTES),
        name="sparse_attn",
    )(lq, c, ct, qit, ki, wt, sg, wuv2)

    return pl.pallas_call(
        _out_kernel,
        grid=(B, S // TM),
        in_specs=[
            tile(ATTN_WIDTH),
            tile(D_MODEL),
            tile(D_MODEL),
            tile(D_MODEL),
            _const_spec((ATTN_WIDTH, D_MODEL)),
            _const_spec((D_MODEL, D_MODEL)),
            _const_spec((1, D_MODEL)),
        ],
        out_specs=tile(D_MODEL),
        out_shape=jax.ShapeDtypeStruct((B, S, D_MODEL), F32),
        compiler_params=pltpu.CompilerParams(
            dimension_semantics=("arbitrary", "arbitrary"),
            vmem_limit_bytes=VMEM_LIMIT_BYTES),
        name="merge_out",
    )(u, ga, r, x, w_ap.astype(BF16), w_out.astype(BF16), row(out_g))


def kernel(x, norm_gain, w_in, b_merge, kv_norm_gain, w_uk, w_uv, idx_ln_gain, idx_ln_bias,
           w_attn_proj, conv_w, conv_b, w_rg_a, b_rg_a, w_rg_x, b_rg_x, lru_lambda,
           w_rnn_proj, w_out, final_norm_gain):
    assert norm_gain.shape[0] == 1, "only the stated depth-1 stack is supported"
    return _layer(x, norm_gain[0], w_in[0], b_merge[0], kv_norm_gain[0], w_uk[0], w_uv[0],
                  idx_ln_gain[0], idx_ln_bias[0], w_attn_proj[0], conv_w[0], conv_b[0],
                  w_rg_a[0], b_rg_a[0], w_rg_x[0], b_rg_x[0], lru_lambda[0],
                  w_rnn_proj[0], w_out[0], final_norm_gain)
```

```python
import functools

import numpy as np
import jax
import jax.numpy as jnp
from jax import lax
from jax.experimental import pallas as pl
from jax.experimental.pallas import tpu as pltpu

D_MODEL = 1024
N_HEADS = 16
HEAD_DIM = 64
ATTN_WIDTH = N_HEADS * HEAD_DIM
KV_LATENT = 128
IDX_HEADS = 8
IDX_DIM = 64
TOPK_MAX = 256
RNN_WIDTH = 1024
RNN_BLOCKS = 16
RNN_BLOCK_DIM = RNN_WIDTH // RNN_BLOCKS
CONV_WIDTH = 4
LRU_C = 8.0
NORM_EPS = 1e-6

F32 = jnp.float32
BF16 = jnp.bfloat16

LANES = 128
SUBLANES = 8
VMEM_LIMIT_BYTES = 56 * 1024 * 1024

TM = 512
TQ = 128
KC = 128
TK = 256
CC = 512
VROWS = 144
BISECT_FIXED = 18
BISECT_PER_CHECK = 2
BISECT_MAX_CHECKS = 8
LOG2E = 1.4426950408889634
F32_MIN_NORMAL = 2.0 ** -126
MASK_NEG = -(2.0 ** 100)
M_INIT = -1e20

C_Q = 0
C_C = C_Q + ATTN_WIDTH
C_QI = C_C + KV_LATENT
C_KW = C_QI + IDX_HEADS * IDX_DIM
C_GA = C_KW + LANES
C_XR = C_GA + ATTN_WIDTH
C_GR = C_XR + RNN_WIDTH
C_MG = C_GR + RNN_WIDTH
C_END = C_MG + 2 * D_MODEL

ALIBI_SLOPES = [float(2.0 ** (-8.0 * (h + 1) / N_HEADS)) for h in range(N_HEADS)]


def _sigmoid_of_half(h):
    return 0.5 * jnp.tanh(h) + 0.5


def _silu(v):
    h = 0.5 * v
    return h * jnp.tanh(h) + h


def _f32_to_key(v):
    b = lax.bitcast_convert_type(v, jnp.int32)
    return b ^ ((b >> 31) & 0x7FFFFFFF)


def _key_to_f32(k):
    return lax.bitcast_convert_type(k ^ ((k >> 31) & 0x7FFFFFFF), F32)


def _proj_kernel(x_ref, ng_ref, w_ref, wuk_ref, kvg_ref, lng_ref, lnb_ref, cw_ref, cb_ref, wg_ref,
                 ba_ref, bx_ref, lam_ref, wrp_ref, bm_ref,
                 lq_ref, c_ref, ct_ref, qit_ref, ki_ref, wt_ref, sg_ref, ga_ref, r_ref,
                 xext_ref, a_ref, b_ref, h_ref, hc_ref, szr_ref, gr_ref):
    j = pl.program_id(1)

    @pl.when(j == 0)
    def _():
        xext_ref[0:SUBLANES, :] = jnp.zeros((SUBLANES, RNN_WIDTH), F32)
        hc_ref[...] = jnp.zeros((SUBLANES, RNN_WIDTH), F32)

    @pl.when(j > 0)
    def _():
        xext_ref[0:SUBLANES, :] = xext_ref[TM:TM + SUBLANES, :]

    x = x_ref[0]
    xn = x * lax.rsqrt(jnp.mean(x * x, axis=-1, keepdims=True) + NORM_EPS) * ng_ref[...]
    xb = xn.astype(BF16)

    def proj(lo, hi):
        return jnp.dot(xb, w_ref[:, lo:hi], preferred_element_type=F32)

    xext_ref[SUBLANES:SUBLANES + TM, :] = proj(C_XR, C_GR)

    def rnn_inputs(blk):
        cols = slice(blk * 2 * LANES, (blk + 1) * 2 * LANES)
        y = cb_ref[:, cols]
        for k in range(CONV_WIDTH):
            off = SUBLANES - (CONV_WIDTH - 1) + k
            y = y + cw_ref[k:k + 1, cols] * xext_ref[off:off + TM, cols]
        yb = y.astype(BF16)
        g = [jnp.dot(yb[:, h * LANES:(h + 1) * LANES], wg_ref[2 * blk + h],
                     preferred_element_type=F32) for h in range(2)]
        ta = jnp.tanh(jnp.concatenate([g[0][:, :LANES], g[1][:, :LANES]], axis=1) + ba_ref[:, cols])
        ig = _sigmoid_of_half(
            jnp.concatenate([g[0][:, LANES:], g[1][:, LANES:]], axis=1) + bx_ref[:, cols])
        nl = -lam_ref[:, cols]
        softplus = jnp.maximum(nl, 0.0) + jnp.log1p(jnp.exp(-jnp.abs(nl)))
        half_rate = (-0.5 * LRU_C) * softplus
        log_a = half_rate * ta + half_rate
        a_ref[:, cols] = jnp.exp(log_a)
        th = jnp.tanh(log_a)
        b_ref[:, cols] = jnp.sqrt(-2.0 * th / (1.0 - th)) * (ig * y)

    def attn_queries():
        qb16 = proj(C_Q, C_C).astype(BF16)
        ql = [jnp.dot(qb16[:, p * LANES:(p + 1) * LANES], wuk_ref[p], preferred_element_type=F32)
              for p in range(N_HEADS // 2)]
        qlt = (jnp.concatenate(ql, axis=1) * (HEAD_DIM ** -0.5 * LOG2E)).T
        for qb in range(TM // TQ):
            for h in range(N_HEADS):
                lq_ref[0, qb, :, h * TQ:(h + 1) * TQ] = (
                    qlt[h * KV_LATENT:(h + 1) * KV_LATENT, qb * TQ:(qb + 1) * TQ].astype(BF16))

    def attn_keys_and_indexer():
        zc = proj(C_C, C_QI)
        cn = zc * lax.rsqrt(jnp.mean(zc * zc, axis=-1, keepdims=True) + NORM_EPS) * kvg_ref[...]
        c_ref[0] = cn.astype(BF16)
        cnt = cn.T
        for k in range(TM // TK):
            ct_ref[0, k] = cnt[:, k * TK:(k + 1) * TK].astype(BF16)

        zqt = (proj(C_QI, C_KW) * (IDX_DIM ** -0.5)).T
        for qb in range(TM // TQ):
            for jh in range(IDX_HEADS):
                qit_ref[0, qb, :, jh * TQ:(jh + 1) * TQ] = (
                    zqt[jh * IDX_DIM:(jh + 1) * IDX_DIM, qb * TQ:(qb + 1) * TQ].astype(BF16))

        zk = proj(C_KW, C_GA)
        lane = lax.broadcasted_iota(jnp.int32, zk.shape, 1)
        is_k = lane < IDX_DIM
        mu = jnp.sum(jnp.where(is_k, zk, 0.0), axis=-1, keepdims=True) * (1.0 / IDX_DIM)
        kc = jnp.where(is_k, zk - mu, 0.0)
        var = jnp.sum(kc * kc, axis=-1, keepdims=True) * (1.0 / IDX_DIM)
        kin = kc * lax.rsqrt(var + NORM_EPS) * lng_ref[...] + lnb_ref[...]
        ki_ref[0] = kin[:, :IDX_DIM].astype(BF16)
        wt_ref[0] = zk.T[IDX_DIM:IDX_DIM + IDX_HEADS, :] * (IDX_HEADS ** -0.5)

    def attn_gate():
        zg = proj(C_GA, C_XR)
        sg_ref[0] = _silu(zg).astype(BF16)

    rnn_inputs(0)
    ga_ref[0] = _sigmoid_of_half(proj(C_MG, C_MG + D_MODEL) + bm_ref[:, :D_MODEL]).astype(BF16)
    rnn_inputs(1)
    gr_ref[...] = _sigmoid_of_half(proj(C_MG + D_MODEL, C_END) + bm_ref[:, D_MODEL:])
    rnn_inputs(2)
    zr = proj(C_GR, C_MG)
    szr_ref[...] = _silu(zr)
    rnn_inputs(3)

    row = lax.broadcasted_iota(jnp.int32, (SUBLANES, RNN_WIDTH), 0)

    hprev = hc_ref[...]
    for ci in range(TM // SUBLANES):
        rows = slice(ci * SUBLANES, (ci + 1) * SUBLANES)
        av = a_ref[rows, :]
        bv = b_ref[rows, :]
        for d in (1, 2, 4):
            a_sh = jnp.where(row >= d, pltpu.roll(av, d, 0), 1.0)
            b_sh = jnp.where(row >= d, pltpu.roll(bv, d, 0), 0.0)
            bv = av * b_sh + bv
            av = av * a_sh
        hv = av * hprev + bv
        h_ref[rows, :] = hv
        hprev = jnp.broadcast_to(hv[SUBLANES - 1:SUBLANES, :], (SUBLANES, RNN_WIDTH))
    hc_ref[...] = hprev

    attn_queries()
    attn_keys_and_indexer()
    attn_gate()

    u = (h_ref[...] * szr_ref[...]).astype(BF16)
    y_rnn = jnp.dot(u, wrp_ref[...], preferred_element_type=F32)
    r_ref[0] = (gr_ref[...] * y_rnn).astype(BF16)


def _attn_kernel(topk, lq_ref, c_ref, ct_ref, qit_ref, ki_ref, wt_ref, sg_ref, wuv_ref,
                 ga_ref, r_ref, x_ref, wap_ref, wout_ref, fg_ref,
                 out_ref,
                 lhst_ref, kext_ref, vt_ref, pos_ref, sc_ref, s_ref, acc_ref, ubuf_ref):
    i = pl.program_id(1)
    seq = kext_ref.shape[0]
    n_kt = (i * TQ + TQ + TK - 1) // TK
    n_cc = (i * TQ + TQ + CC - 1) // CC

    @pl.when(jnp.logical_and(pl.program_id(0) == 0, i == 0))
    def _():
        er = lax.broadcasted_iota(jnp.int32, (TQ, TQ), 0)
        ec = lax.broadcasted_iota(jnp.int32, (TQ, TQ), 1)
        eye = jnp.where(er == ec, 1.0, 0.0).astype(BF16)
        for h in range(N_HEADS):
            lhst_ref[KV_LATENT:KV_LATENT + TQ, h * TQ:(h + 1) * TQ] = eye
        pos_ref[...] = lax.broadcasted_iota(jnp.int32, (seq, LANES), 0).astype(F32)
        rr = lax.broadcasted_iota(jnp.int32, (VROWS - KV_LATENT, TK), 0)
        for k in range(seq // TK):
            vt_ref[k, KV_LATENT:VROWS, :] = jnp.where(rr == 0, 1.0, 0.0).astype(BF16)

    @pl.when(i == 0)
    def _():
        kext_ref[:, 0:KV_LATENT] = c_ref[0]
        for k in range(seq // TK):
            vt_ref[k, 0:KV_LATENT, :] = ct_ref[0, k]

    lhst_ref[0:KV_LATENT, :] = lq_ref[0, 0]

    qit = qit_ref[0, 0]
    wv = wt_ref[0]
    tpos = i * TQ + lax.broadcasted_iota(jnp.int32, (1, TQ), 1)

    def fold8(v, op):
        parts = [op(v[k * KC:(k + 1) * KC].reshape(KC // SUBLANES, SUBLANES, v.shape[1]), axis=0)
                 for k in range(v.shape[0] // KC)]
        return op(jnp.stack(parts), axis=0)

    def walk(block_fn, carry):
        n4 = n_kt // 4
        carry = lax.fori_loop(
            0, n4, lambda k, cr: block_fn(pl.multiple_of(k * 4 * TK, 4 * TK), 4 * TK, k * 4, cr),
            carry)
        kt2 = n4 * 4
        has2 = n_kt % 4 >= 2
        carry = lax.cond(
            has2, lambda cr: block_fn(pl.multiple_of(kt2 * TK, 2 * TK), 2 * TK, kt2, cr),
            lambda cr: cr, carry)
        kt1 = kt2 + jnp.where(has2, 2, 0)
        return lax.cond(
            n_kt % 2 == 1, lambda cr: block_fn(pl.multiple_of(kt1 * TK, TK), TK, kt1, cr),
            lambda cr: cr, carry)

    def score_block(r0, rows, kt0, carry):
        lo8, hi8, pos8, neg8 = carry
        prod = jnp.dot(ki_ref[0, pl.ds(r0, rows), :], qit, preferred_element_type=F32)
        sc = jnp.maximum(prod[:, 0:TQ], 0.0) * wv[0:1, :]
        for jh in range(1, IDX_HEADS):
            sc = sc + jnp.maximum(prod[:, jh * TQ:(jh + 1) * TQ], 0.0) * wv[jh:jh + 1, :]
        causal = (lax.broadcasted_iota(jnp.int32, (rows, TQ), 0) + r0) <= tpos
        sc_ref[pl.ds(r0, rows), :] = jnp.where(causal, sc, -jnp.inf)
        lo8 = jnp.minimum(lo8, fold8(jnp.where(causal, sc, jnp.inf), jnp.min))
        hi8 = jnp.maximum(hi8, fold8(jnp.where(causal, sc, -jnp.inf), jnp.max))
        scz = jnp.where(causal, sc, 0.0)
        pos8 = jnp.minimum(pos8, fold8(jnp.where(scz > 0.0, scz, jnp.inf), jnp.min))
        neg8 = jnp.maximum(neg8, fold8(jnp.where(scz < 0.0, scz, -jnp.inf), jnp.max))
        return lo8, hi8, pos8, neg8

    inf8 = jnp.full((SUBLANES, TQ), jnp.inf, F32)
    lo8, hi8, pos8, neg8 = walk(score_block, (inf8, -inf8, inf8, -inf8))

    @pl.when(n_kt * TK < n_cc * CC)
    def _():
        r0 = pl.multiple_of(n_kt * TK, TK)
        sc_ref[pl.ds(r0, TK), :] = jnp.full((TK, TQ), -jnp.inf, F32)

    lo_k = _f32_to_key(jnp.min(lo8, axis=0, keepdims=True))
    vmax = jnp.max(hi8, axis=0, keepdims=True)
    hi_k = jnp.where(jnp.abs(vmax) < F32_MIN_NORMAL, _f32_to_key(jnp.full_like(vmax, F32_MIN_NORMAL)),
                     _f32_to_key(vmax) + 1)
    kf = float(topk)

    def count_ge(thr):
        def body(cc, c8):
            r0 = pl.multiple_of(cc * CC, CC)
            xs = sc_ref[pl.ds(r0, CC), :]
            return c8 + fold8(jnp.where(xs >= thr, 1.0, 0.0), jnp.sum)
        c8 = lax.fori_loop(0, n_cc, body, jnp.zeros((SUBLANES, TQ), F32))
        return jnp.sum(c8, axis=0, keepdims=True)

    def midpoint(lo_, hi_):
        return (lo_ >> 1) + (hi_ >> 1) + (lo_ & hi_ & 1)

    def probe(carry, thr_k):
        lo_, hi_, clo_, done_ = carry
        cnt = count_ge(_key_to_f32(thr_k))
        ge = cnt >= kf
        up = jnp.logical_and(ge, thr_k > lo_)
        down = jnp.logical_and(jnp.logical_not(ge), thr_k < hi_)
        return (jnp.where(up, thr_k, lo_), jnp.where(down, thr_k, hi_),
                jnp.where(up, cnt, clo_), done_), ge

    def bisect(carry):
        return probe(carry, midpoint(carry[0], carry[1]))[0]

    def unresolved(carry):
        lo_, hi_, clo_, done_ = carry
        open_ = jnp.where(clo_ > kf, jnp.where(midpoint(lo_, hi_) != lo_, 1.0 - done_, 0.0), 0.0)
        return (jnp.max(open_) > 0.5).astype(jnp.int32)

    n_causal = (tpos[0:1, :] + 1).astype(F32)
    searching = (i + 1) * TQ > topk
    carry = (lo_k, hi_k, n_causal, jnp.zeros((1, TQ), F32))

    def zero_steps(cr):
        zero_k = jnp.zeros((1, TQ), jnp.int32)
        cr, ge0 = probe(cr, zero_k)
        pmin = jnp.min(pos8, axis=0, keepdims=True)
        nmax = jnp.max(neg8, axis=0, keepdims=True)
        near = jnp.where(ge0, pmin, nmax)
        has_near = jnp.abs(near) < jnp.inf
        near_k = jnp.where(has_near, _f32_to_key(near), zero_k)
        cr, ge1 = probe(cr, near_k)
        settled = jnp.logical_and(has_near, ge0 != ge1)
        return cr[0], cr[1], cr[2], jnp.where(settled, 1.0, 0.0)

    carry = lax.cond(searching, zero_steps, lambda cr: cr, carry)
    n_fixed = jnp.where(searching, BISECT_FIXED, 0)
    carry = lax.fori_loop(0, n_fixed, lambda _, cr: bisect(cr), carry)

    def search_more(state):
        it, _, cr = state
        for _ in range(BISECT_PER_CHECK):
            cr = bisect(cr)
        return it + 1, unresolved(cr), cr

    _, _, (lo_k, hi_k, c_lo, _) = lax.while_loop(
        lambda st: jnp.logical_and(st[1] > 0, st[0] < BISECT_MAX_CHECKS),
        search_more, (jnp.int32(0), unresolved(carry), carry))
    lo = _key_to_f32(lo_k)
    hi = _key_to_f32(hi_k)

    has_ties = jnp.max(jnp.where(c_lo > kf, 1.0, 0.0)) > 0.5

    @pl.when(jnp.logical_not(has_ties))
    def _():
        def mask_chunk(kt, _):
            r0 = pl.multiple_of(kt * TK, TK)
            xs = sc_ref[pl.ds(r0, TK), :]
            kext_ref[pl.ds(r0, TK), KV_LATENT:KV_LATENT + TQ] = (
                jnp.where(xs >= lo, 0.0, MASK_NEG).astype(BF16))
            return 0
        lax.fori_loop(0, n_kt, mask_chunk, 0)

    @pl.when(has_ties)
    def _():
        need = kf - count_ge(hi)
        tri = jnp.where(lax.broadcasted_iota(jnp.int32, (TK, TK), 1)
                        <= lax.broadcasted_iota(jnp.int32, (TK, TK), 0), 1.0, 0.0).astype(BF16)

        def mask_chunk(kt, before):
            r0 = pl.multiple_of(kt * TK, TK)
            xs = sc_ref[pl.ds(r0, TK), :]
            band = jnp.where(xs >= lo, jnp.where(xs < hi, 1.0, 0.0), 0.0)
            incl = jnp.dot(tri, band.astype(BF16), preferred_element_type=F32)
            rank = incl - band + before
            keep = jnp.where(xs >= hi, 1.0, jnp.where(rank < need, band, 0.0))
            kext_ref[pl.ds(r0, TK), KV_LATENT:KV_LATENT + TQ] = (
                jnp.where(keep > 0.0, 0.0, MASK_NEG).astype(BF16))
            return before + incl[TK - 1:TK, :]

        lax.fori_loop(0, n_kt, mask_chunk, jnp.zeros((1, TQ), F32))


    def qk_block(r0, rows, kt0, mx):
        st = jnp.dot(kext_ref[pl.ds(r0, rows), :], lhst_ref[...], preferred_element_type=F32)
        posb = pos_ref[pl.ds(r0, rows), :]
        tops = []
        for h in range(N_HEADS):
            sh = st[:, h * TQ:(h + 1) * TQ] + posb * (ALIBI_SLOPES[h] * LOG2E)
            s_ref[pl.ds(r0, rows), h * TQ:(h + 1) * TQ] = sh
            tops.append(fold8(sh, jnp.max))
        return jnp.maximum(mx, jnp.concatenate(tops, axis=1))

    mx = walk(qk_block, jnp.full((SUBLANES, N_HEADS * TQ), M_INIT, F32))
    m = jnp.max(mx, axis=0, keepdims=True)
    acc_ref[...] = jnp.zeros(acc_ref.shape, F32)

    def pv_block(r0, rows, kt0, _):
        pt = jnp.exp2(s_ref[pl.ds(r0, rows), :] - m).astype(BF16)
        vt = jnp.concatenate([vt_ref[kt0 + k] for k in range(rows // TK)], axis=1)
        acc_ref[...] += jnp.dot(vt, pt, preferred_element_type=F32)
        return 0

    walk(pv_block, 0)

    o_parts = []
    for p in range(N_HEADS // 2):
        cols = slice(p * 2 * TQ, (p + 1) * 2 * TQ)
        xt = (acc_ref[0:KV_LATENT, cols] / acc_ref[KV_LATENT:KV_LATENT + 1, cols]).T
        pair = jnp.concatenate([xt[:TQ], xt[TQ:]], axis=1).astype(BF16)
        o_parts.append(jnp.dot(pair, wuv_ref[p], preferred_element_type=F32))
    o = jnp.concatenate(o_parts, axis=1)
    sub = i % (TM // TQ)
    ubuf_ref[pl.ds(pl.multiple_of(sub * TQ, TQ), TQ), :] = (o * sg_ref[0].astype(F32)).astype(BF16)

    @pl.when(sub == TM // TQ - 1)
    def _():
        y_attn = jnp.dot(ubuf_ref[...], wap_ref[...], preferred_element_type=F32)
        mixed = (ga_ref[0].astype(F32) * y_attn + r_ref[0].astype(F32)).astype(BF16)
        res = x_ref[0] + jnp.dot(mixed, wout_ref[...], preferred_element_type=F32)
        out_ref[0] = (res * lax.rsqrt(jnp.mean(res * res, axis=-1, keepdims=True) + NORM_EPS)
                      * fg_ref[...])


def _const_spec(shape):
    nd = len(shape)
    return pl.BlockSpec(shape, lambda b, i: (0,) * nd, pipeline_mode=pl.Buffered(1))


def _pack_w_in(w_in):
    offs = np.cumsum([0, ATTN_WIDTH, KV_LATENT, IDX_HEADS * IDX_DIM, IDX_DIM, IDX_HEADS,
                      ATTN_WIDTH, RNN_WIDTH, RNN_WIDTH, 2 * D_MODEL])
    seg = [w_in[:, offs[k]:offs[k + 1]] for k in range(9)]
    seg[8] = seg[8] * 0.5
    pad = jnp.zeros((D_MODEL, LANES - IDX_DIM - IDX_HEADS), w_in.dtype)
    return jnp.concatenate(seg[:5] + [pad] + seg[5:], axis=1).astype(BF16)


def _pair_block_diag(w):
    n2, r, c = w.shape
    w = w.reshape(n2 // 2, 2, r, c)
    z = jnp.zeros((n2 // 2, r, c), w.dtype)
    top = jnp.concatenate([w[:, 0], z], axis=2)
    bot = jnp.concatenate([z, w[:, 1]], axis=2)
    return jnp.concatenate([top, bot], axis=1)


def _layer(x, norm_g, w_in, b_merge, kv_g, w_uk, w_uv, ln_g, ln_b, w_ap, conv_w, conv_b,
           w_a, b_a, w_x, b_x, lam, w_rp, w_out, out_g):
    B, S, _ = x.shape
    assert S % TM == 0 and TM % TK == 0 and TK % TQ == 0 and TQ == KC
    assert S % CC == 0 and CC % TK == 0 and S % (4 * TK) == 0
    topk = min(TOPK_MAX, S // 4)
    row = lambda v: v.reshape(1, -1).astype(F32)
    lane_pad = lambda v: jnp.pad(v.astype(F32), (0, LANES - v.shape[0])).reshape(1, LANES)

    w_all = _pack_w_in(w_in)
    wg = (0.5 * jnp.concatenate([_pair_block_diag(w_a), _pair_block_diag(w_x)], axis=2)).astype(BF16)
    wuk2t = _pair_block_diag(jnp.swapaxes(w_uk, 1, 2)).astype(BF16)

    nq = S // TQ
    tile = lambda n: pl.BlockSpec((1, TM, n), lambda b, j: (b, j, 0))
    outs = pl.pallas_call(
        _proj_kernel,
        grid=(B, S // TM),
        in_specs=[
            tile(D_MODEL),
            _const_spec((1, D_MODEL)),
            _const_spec((D_MODEL, C_END)),
            _const_spec((N_HEADS // 2, 2 * HEAD_DIM, 2 * KV_LATENT)),
            _const_spec((1, KV_LATENT)),
            _const_spec((1, LANES)),
            _const_spec((1, LANES)),
            _const_spec((CONV_WIDTH, RNN_WIDTH)),
            _const_spec((1, RNN_WIDTH)),
            _const_spec((RNN_WIDTH // LANES, LANES, 2 * LANES)),
            _const_spec((1, RNN_WIDTH)),
            _const_spec((1, RNN_WIDTH)),
            _const_spec((1, RNN_WIDTH)),
            _const_spec((RNN_WIDTH, D_MODEL)),
            _const_spec((1, 2 * D_MODEL)),
        ],
        out_specs=[
            pl.BlockSpec((1, TM // TQ, KV_LATENT, N_HEADS * TQ), lambda b, j: (b, j, 0, 0)),
            tile(KV_LATENT),
            pl.BlockSpec((1, TM // TK, KV_LATENT, TK), lambda b, j: (b, j, 0, 0)),
            pl.BlockSpec((1, TM // TQ, IDX_DIM, IDX_HEADS * TQ), lambda b, j: (b, j, 0, 0)),
            tile(IDX_DIM),
            pl.BlockSpec((1, IDX_HEADS, TM), lambda b, j: (b, 0, j)),
            tile(ATTN_WIDTH),
            tile(D_MODEL),
            tile(D_MODEL),
        ],
        out_shape=[
            jax.ShapeDtypeStruct((B, nq, KV_LATENT, N_HEADS * TQ), BF16),
            jax.ShapeDtypeStruct((B, S, KV_LATENT), BF16),
            jax.ShapeDtypeStruct((B, S // TK, KV_LATENT, TK), BF16),
            jax.ShapeDtypeStruct((B, nq, IDX_DIM, IDX_HEADS * TQ), BF16),
            jax.ShapeDtypeStruct((B, S, IDX_DIM), BF16),
            jax.ShapeDtypeStruct((B, IDX_HEADS, S), F32),
            jax.ShapeDtypeStruct((B, S, ATTN_WIDTH), BF16),
            jax.ShapeDtypeStruct((B, S, D_MODEL), BF16),
            jax.ShapeDtypeStruct((B, S, D_MODEL), BF16),
        ],
        scratch_shapes=[
            pltpu.VMEM((TM + SUBLANES, RNN_WIDTH), F32),
            pltpu.VMEM((TM, RNN_WIDTH), F32),
            pltpu.VMEM((TM, RNN_WIDTH), F32),
            pltpu.VMEM((TM, RNN_WIDTH), F32),
            pltpu.VMEM((SUBLANES, RNN_WIDTH), F32),
            pltpu.VMEM((TM, RNN_WIDTH), F32),
            pltpu.VMEM((TM, D_MODEL), F32),
        ],
        compiler_params=pltpu.CompilerParams(
            dimension_semantics=("arbitrary", "arbitrary"),
            vmem_limit_bytes=VMEM_LIMIT_BYTES),
        name="proj_rglru",
    )(x, row(norm_g), w_all, wuk2t, row(kv_g), lane_pad(ln_g), lane_pad(ln_b), conv_w.astype(F32),
      row(conv_b), wg, row(0.5 * b_a), row(0.5 * b_x), row(lam), w_rp.astype(BF16),
      row(0.5 * b_merge))
    lq, c, ct, qit, ki, wt, sg, ga, r = outs

    wuv2 = _pair_block_diag(w_uv).astype(BF16)

    qtile = lambda n: pl.BlockSpec((1, TQ, n), lambda b, i: (b, i, 0))
    otile = lambda n: pl.BlockSpec((1, TM, n), lambda b, i: (b, i // (TM // TQ), 0))
    return pl.pallas_call(
        functools.partial(_attn_kernel, topk),
        grid=(B, nq),
        in_specs=[
            pl.BlockSpec((1, 1, KV_LATENT, N_HEADS * TQ), lambda b, i: (b, i, 0, 0)),
            pl.BlockSpec((1, S, KV_LATENT), lambda b, i: (b, 0, 0)),
            pl.BlockSpec((1, S // TK, KV_LATENT, TK), lambda b, i: (b, 0, 0, 0)),
            pl.BlockSpec((1, 1, IDX_DIM, IDX_HEADS * TQ), lambda b, i: (b, i, 0, 0)),
            pl.BlockSpec((1, S, IDX_DIM), lambda b, i: (b, 0, 0)),
            pl.BlockSpec((1, IDX_HEADS, TQ), lambda b, i: (b, 0, i)),
            qtile(ATTN_WIDTH),
            _const_spec((N_HEADS // 2, 2 * KV_LATENT, 2 * HEAD_DIM)),
            otile(D_MODEL),
            otile(D_MODEL),
            otile(D_MODEL),
            _const_spec((ATTN_WIDTH, D_MODEL)),
            _const_spec((D_MODEL, D_MODEL)),
            _const_spec((1, D_MODEL)),
        ],
        out_specs=otile(D_MODEL),
        out_shape=jax.ShapeDtypeStruct((B, S, D_MODEL), F32),
        scratch_shapes=[
            pltpu.VMEM((2 * KV_LATENT, N_HEADS * TQ), BF16),
            pltpu.VMEM((S, 2 * KV_LATENT), BF16),
            pltpu.VMEM((S // TK, VROWS, TK), BF16),
            pltpu.VMEM((S, LANES), F32),
            pltpu.VMEM((S, TQ), F32),
            pltpu.VMEM((S, N_HEADS * TQ), F32),
            pltpu.VMEM((VROWS, N_HEADS * TQ), F32),
            pltpu.VMEM((TM, ATTN_WIDTH), BF16),
        ],
        compiler_params=pltpu.CompilerParams(
            dimension_semantics=("arbitrary", "arbitrary"),
            vmem_limit_bytes=VMEM_LIMIT_BYTES),
        name="sparse_attn_out",
    )(lq, c, ct, qit, ki, wt, sg, wuv2, ga, r, x, w_ap.astype(BF16), w_out.astype(BF16),
      row(out_g))


def kernel(x, norm_gain, w_in, b_merge, kv_norm_gain, w_uk, w_uv, idx_ln_gain, idx_ln_bias,
           w_attn_proj, conv_w, conv_b, w_rg_a, b_rg_a, w_rg_x, b_rg_x, lru_lambda,
           w_rnn_proj, w_out, final_norm_gain):
    assert norm_gain.shape[0] == 1, "only the stated depth-1 stack is supported"
    return _layer(x, norm_gain[0], w_in[0], b_merge[0], kv_norm_gain[0], w_uk[0], w_uv[0],
                  idx_ln_gain[0], idx_ln_bias[0], w_attn_proj[0], conv_w[0], conv_b[0],
                  w_rg_a[0], b_rg_a[0], w_rg_x[0], b_rg_x[0], lru_lambda[0],
                  w_rnn_proj[0], w_out[0], final_norm_gain)
```

```python
import functools

import numpy as np
import jax
import jax.numpy as jnp
from jax import lax
from jax.experimental import pallas as pl
from jax.experimental.pallas import tpu as pltpu

D_MODEL = 1024
N_HEADS = 16
HEAD_DIM = 64
ATTN_WIDTH = N_HEADS * HEAD_DIM
KV_LATENT = 128
IDX_HEADS = 8
IDX_DIM = 64
TOPK_MAX = 256
RNN_WIDTH = 1024
RNN_BLOCKS = 16
RNN_BLOCK_DIM = RNN_WIDTH // RNN_BLOCKS
CONV_WIDTH = 4
LRU_C = 8.0
NORM_EPS = 1e-6

F32 = jnp.float32
BF16 = jnp.bfloat16

LANES = 128
SUBLANES = 8
VMEM_LIMIT_BYTES = 56 * 1024 * 1024

TM = 512
TQ = 128
QB = 2
KC = 128
TK = 256
CC = 512
VROWS = 144
BISECT_FIXED = 18
BISECT_PER_CHECK = 2
BISECT_MAX_CHECKS = 8
LOG2E = 1.4426950408889634
F32_MIN_NORMAL = 2.0 ** -126
MASK_NEG = -(2.0 ** 100)
M_INIT = -1e20

C_Q = 0
C_C = C_Q + ATTN_WIDTH
C_QI = C_C + KV_LATENT
C_KW = C_QI + IDX_HEADS * IDX_DIM
C_GA = C_KW + LANES
C_XR = C_GA + ATTN_WIDTH
C_GR = C_XR + RNN_WIDTH
C_MG = C_GR + RNN_WIDTH
C_END = C_MG + 2 * D_MODEL

ALIBI_SLOPES = [float(2.0 ** (-8.0 * (h + 1) / N_HEADS)) for h in range(N_HEADS)]


def _sigmoid_of_half(h):
    return 0.5 * jnp.tanh(h) + 0.5


def _silu(v):
    h = 0.5 * v
    return h * jnp.tanh(h) + h


def _f32_to_key(v):
    b = lax.bitcast_convert_type(v, jnp.int32)
    return b ^ ((b >> 31) & 0x7FFFFFFF)


def _key_to_f32(k):
    return lax.bitcast_convert_type(k ^ ((k >> 31) & 0x7FFFFFFF), F32)


def _proj_kernel(x_ref, ng_ref, w_ref, wuk_ref, kvg_ref, lng_ref, lnb_ref, cw_ref, cb_ref, wg_ref,
                 ba_ref, bx_ref, lam_ref, wrp_ref, bm_ref,
                 lq_ref, c_ref, ct_ref, qit_ref, ki_ref, wt_ref, sg_ref, ga_ref, r_ref,
                 xext_ref, a_ref, b_ref, h_ref, hc_ref, szr_ref, gr_ref):
    j = pl.program_id(1)

    @pl.when(j == 0)
    def _():
        xext_ref[0:SUBLANES, :] = jnp.zeros((SUBLANES, RNN_WIDTH), F32)
        hc_ref[...] = jnp.zeros((SUBLANES, RNN_WIDTH), F32)

    @pl.when(j > 0)
    def _():
        xext_ref[0:SUBLANES, :] = xext_ref[TM:TM + SUBLANES, :]

    x = x_ref[0]
    xn = x * lax.rsqrt(jnp.mean(x * x, axis=-1, keepdims=True) + NORM_EPS) * ng_ref[...]
    xb = xn.astype(BF16)

    def proj(lo, hi):
        return jnp.dot(xb, w_ref[:, lo:hi], preferred_element_type=F32)

    xext_ref[SUBLANES:SUBLANES + TM, :] = proj(C_XR, C_GR)

    def rnn_inputs(blk):
        cols = slice(blk * 2 * LANES, (blk + 1) * 2 * LANES)
        y = cb_ref[:, cols]
        for k in range(CONV_WIDTH):
            off = SUBLANES - (CONV_WIDTH - 1) + k
            y = y + cw_ref[k:k + 1, cols] * xext_ref[off:off + TM, cols]
        yb = y.astype(BF16)
        g = [jnp.dot(yb[:, h * LANES:(h + 1) * LANES], wg_ref[2 * blk + h],
                     preferred_element_type=F32) for h in range(2)]
        ta = jnp.tanh(jnp.concatenate([g[0][:, :LANES], g[1][:, :LANES]], axis=1) + ba_ref[:, cols])
        ig = _sigmoid_of_half(
            jnp.concatenate([g[0][:, LANES:], g[1][:, LANES:]], axis=1) + bx_ref[:, cols])
        nl = -lam_ref[:, cols]
        softplus = jnp.maximum(nl, 0.0) + jnp.log1p(jnp.exp(-jnp.abs(nl)))
        half_rate = (-0.5 * LRU_C) * softplus
        log_a = half_rate * ta + half_rate
        a_ref[:, cols] = jnp.exp(log_a)
        th = jnp.tanh(log_a)
        b_ref[:, cols] = jnp.sqrt(-2.0 * th / (1.0 - th)) * (ig * y)

    def attn_queries():
        qb16 = proj(C_Q, C_C).astype(BF16)
        ql = [jnp.dot(qb16[:, p * LANES:(p + 1) * LANES], wuk_ref[p], preferred_element_type=F32)
              for p in range(N_HEADS // 2)]
        qlt = (jnp.concatenate(ql, axis=1) * (HEAD_DIM ** -0.5 * LOG2E)).T
        for qb in range(TM // TQ):
            for h in range(N_HEADS):
                lq_ref[0, qb, :, h * TQ:(h + 1) * TQ] = (
                    qlt[h * KV_LATENT:(h + 1) * KV_LATENT, qb * TQ:(qb + 1) * TQ].astype(BF16))

    def attn_keys_and_indexer():
        zc = proj(C_C, C_QI)
        cn = zc * lax.rsqrt(jnp.mean(zc * zc, axis=-1, keepdims=True) + NORM_EPS) * kvg_ref[...]
        c_ref[0] = cn.astype(BF16)
        cnt = cn.T
        for k in range(TM // TK):
            ct_ref[0, k] = cnt[:, k * TK:(k + 1) * TK].astype(BF16)

        zqt = (proj(C_QI, C_KW) * (IDX_DIM ** -0.5)).T
        for qb in range(TM // TQ):
            for jh in range(IDX_HEADS):
                qit_ref[0, qb, :, jh * TQ:(jh + 1) * TQ] = (
                    zqt[jh * IDX_DIM:(jh + 1) * IDX_DIM, qb * TQ:(qb + 1) * TQ].astype(BF16))

        zk = proj(C_KW, C_GA)
        lane = lax.broadcasted_iota(jnp.int32, zk.shape, 1)
        is_k = lane < IDX_DIM
        mu = jnp.sum(jnp.where(is_k, zk, 0.0), axis=-1, keepdims=True) * (1.0 / IDX_DIM)
        kc = jnp.where(is_k, zk - mu, 0.0)
        var = jnp.sum(kc * kc, axis=-1, keepdims=True) * (1.0 / IDX_DIM)
        kin = kc * lax.rsqrt(var + NORM_EPS) * lng_ref[...] + lnb_ref[...]
        ki_ref[0] = kin[:, :IDX_DIM].astype(BF16)
        wtt = zk.T[IDX_DIM:IDX_DIM + IDX_HEADS, :] * (IDX_HEADS ** -0.5)
        for qb in range(TM // TQ):
            wt_ref[0, qb] = wtt[:, qb * TQ:(qb + 1) * TQ]

    def attn_gate():
        zg = proj(C_GA, C_XR)
        sg_ref[0] = _silu(zg).astype(BF16)

    rnn_inputs(0)
    ga_ref[0] = _sigmoid_of_half(proj(C_MG, C_MG + D_MODEL) + bm_ref[:, :D_MODEL]).astype(BF16)
    rnn_inputs(1)
    gr_ref[...] = _sigmoid_of_half(proj(C_MG + D_MODEL, C_END) + bm_ref[:, D_MODEL:])
    rnn_inputs(2)
    zr = proj(C_GR, C_MG)
    szr_ref[...] = _silu(zr)
    rnn_inputs(3)

    row = lax.broadcasted_iota(jnp.int32, (SUBLANES, RNN_WIDTH), 0)

    hprev = hc_ref[...]
    for ci in range(TM // SUBLANES):
        rows = slice(ci * SUBLANES, (ci + 1) * SUBLANES)
        av = a_ref[rows, :]
        bv = b_ref[rows, :]
        for d in (1, 2, 4):
            a_sh = jnp.where(row >= d, pltpu.roll(av, d, 0), 1.0)
            b_sh = jnp.where(row >= d, pltpu.roll(bv, d, 0), 0.0)
            bv = av * b_sh + bv
            av = av * a_sh
        hv = av * hprev + bv
        h_ref[rows, :] = hv
        hprev = jnp.broadcast_to(hv[SUBLANES - 1:SUBLANES, :], (SUBLANES, RNN_WIDTH))
    hc_ref[...] = hprev

    attn_queries()
    attn_keys_and_indexer()
    attn_gate()

    u = (h_ref[...] * szr_ref[...]).astype(BF16)
    y_rnn = jnp.dot(u, wrp_ref[...], preferred_element_type=F32)
    r_ref[0] = (gr_ref[...] * y_rnn).astype(BF16)


def _attn_kernel(topk, *refs):
    first_row = pl.program_id(0) == 0
    step = pl.program_id(1)

    def one_block(blk, _):
        _attn_block(topk, step * QB + blk, blk, first_row, *refs)
        return 0

    lax.fori_loop(0, QB, one_block, 0)


def _attn_block(topk, i, blk, first_row,
                lq_ref, c_ref, ct_ref, qit_ref, ki_ref, wt_ref, sg_ref, wuv_ref,
                ga_ref, r_ref, x_ref, wap_ref, wout_ref, fg_ref,
                out_ref,
                lhst_ref, kext_ref, vt_ref, pos_ref, sc_ref, s_ref, acc_ref, ubuf_ref):
    seq = kext_ref.shape[0]
    n_kt = (i * TQ + TQ + TK - 1) // TK
    n_cc = (i * TQ + TQ + CC - 1) // CC

    @pl.when(jnp.logical_and(first_row, i == 0))
    def _():
        er = lax.broadcasted_iota(jnp.int32, (TQ, TQ), 0)
        ec = lax.broadcasted_iota(jnp.int32, (TQ, TQ), 1)
        eye = jnp.where(er == ec, 1.0, 0.0).astype(BF16)
        for h in range(N_HEADS):
            lhst_ref[KV_LATENT:KV_LATENT + TQ, h * TQ:(h + 1) * TQ] = eye
        pos_ref[...] = lax.broadcasted_iota(jnp.int32, (seq, LANES), 0).astype(F32)
        rr = lax.broadcasted_iota(jnp.int32, (VROWS - KV_LATENT, TK), 0)
        for k in range(seq // TK):
            vt_ref[k, KV_LATENT:VROWS, :] = jnp.where(rr == 0, 1.0, 0.0).astype(BF16)

    @pl.when(i == 0)
    def _():
        kext_ref[:, 0:KV_LATENT] = c_ref[0]
        for k in range(seq // TK):
            vt_ref[k, 0:KV_LATENT, :] = ct_ref[0, k]

    lhst_ref[0:KV_LATENT, :] = lq_ref[0, blk]

    qit = qit_ref[0, blk]
    wv = wt_ref[0, blk]
    tpos = i * TQ + lax.broadcasted_iota(jnp.int32, (1, TQ), 1)

    def fold8(v, op):
        parts = [op(v[k * KC:(k + 1) * KC].reshape(KC // SUBLANES, SUBLANES, v.shape[1]), axis=0)
                 for k in range(v.shape[0] // KC)]
        return op(jnp.stack(parts), axis=0)

    def walk(block_fn, carry):
        n4 = n_kt // 4
        carry = lax.fori_loop(
            0, n4, lambda k, cr: block_fn(pl.multiple_of(k * 4 * TK, 4 * TK), 4 * TK, k * 4, cr),
            carry)
        kt2 = n4 * 4
        has2 = n_kt % 4 >= 2
        carry = lax.cond(
            has2, lambda cr: block_fn(pl.multiple_of(kt2 * TK, 2 * TK), 2 * TK, kt2, cr),
            lambda cr: cr, carry)
        kt1 = kt2 + jnp.where(has2, 2, 0)
        return lax.cond(
            n_kt % 2 == 1, lambda cr: block_fn(pl.multiple_of(kt1 * TK, TK), TK, kt1, cr),
            lambda cr: cr, carry)

    def score_block(r0, rows, kt0, carry):
        lo8, hi8, pos8, neg8 = carry
        prod = jnp.dot(ki_ref[0, pl.ds(r0, rows), :], qit, preferred_element_type=F32)
        sc = jnp.maximum(prod[:, 0:TQ], 0.0) * wv[0:1, :]
        for jh in range(1, IDX_HEADS):
            sc = sc + jnp.maximum(prod[:, jh * TQ:(jh + 1) * TQ], 0.0) * wv[jh:jh + 1, :]
        causal = (lax.broadcasted_iota(jnp.int32, (rows, TQ), 0) + r0) <= tpos
        sc_ref[pl.ds(r0, rows), :] = jnp.where(causal, sc, -jnp.inf)
        lo8 = jnp.minimum(lo8, fold8(jnp.where(causal, sc, jnp.inf), jnp.min))
        hi8 = jnp.maximum(hi8, fold8(jnp.where(causal, sc, -jnp.inf), jnp.max))
        scz = jnp.where(causal, sc, 0.0)
        pos8 = jnp.minimum(pos8, fold8(jnp.where(scz > 0.0, scz, jnp.inf), jnp.min))
        neg8 = jnp.maximum(neg8, fold8(jnp.where(scz < 0.0, scz, -jnp.inf), jnp.max))
        return lo8, hi8, pos8, neg8

    inf8 = jnp.full((SUBLANES, TQ), jnp.inf, F32)
    lo8, hi8, pos8, neg8 = walk(score_block, (inf8, -inf8, inf8, -inf8))

    @pl.when(n_kt * TK < n_cc * CC)
    def _():
        r0 = pl.multiple_of(n_kt * TK, TK)
        sc_ref[pl.ds(r0, TK), :] = jnp.full((TK, TQ), -jnp.inf, F32)

    lo_k = _f32_to_key(jnp.min(lo8, axis=0, keepdims=True))
    vmax = jnp.max(hi8, axis=0, keepdims=True)
    hi_k = jnp.where(jnp.abs(vmax) < F32_MIN_NORMAL, _f32_to_key(jnp.full_like(vmax, F32_MIN_NORMAL)),
                     _f32_to_key(vmax) + 1)
    kf = float(topk)

    def count_ge(thr):
        def body(cc, c8):
            r0 = pl.multiple_of(cc * CC, CC)
            xs = sc_ref[pl.ds(r0, CC), :]
            return c8 + fold8(jnp.where(xs >= thr, 1.0, 0.0), jnp.sum)
        c8 = lax.fori_loop(0, n_cc, body, jnp.zeros((SUBLANES, TQ), F32))
        return jnp.sum(c8, axis=0, keepdims=True)

    def midpoint(lo_, hi_):
        return (lo_ >> 1) + (hi_ >> 1) + (lo_ & hi_ & 1)

    def probe(carry, thr_k):
        lo_, hi_, clo_, done_ = carry
        cnt = count_ge(_key_to_f32(thr_k))
        ge = cnt >= kf
        up = jnp.logical_and(ge, thr_k > lo_)
        down = jnp.logical_and(jnp.logical_not(ge), thr_k < hi_)
        return (jnp.where(up, thr_k, lo_), jnp.where(down, thr_k, hi_),
                jnp.where(up, cnt, clo_), done_), ge

    def bisect(carry):
        return probe(carry, midpoint(carry[0], carry[1]))[0]

    def unresolved(carry):
        lo_, hi_, clo_, done_ = carry
        open_ = jnp.where(clo_ > kf, jnp.where(midpoint(lo_, hi_) != lo_, 1.0 - done_, 0.0), 0.0)
        return (jnp.max(open_) > 0.5).astype(jnp.int32)

    n_causal = (tpos[0:1, :] + 1).astype(F32)
    searching = (i + 1) * TQ > topk
    carry = (lo_k, hi_k, n_causal, jnp.zeros((1, TQ), F32))

    def zero_steps(cr):
        zero_k = jnp.zeros((1, TQ), jnp.int32)
        cr, ge0 = probe(cr, zero_k)
        pmin = jnp.min(pos8, axis=0, keepdims=True)
        nmax = jnp.max(neg8, axis=0, keepdims=True)
        near = jnp.where(ge0, pmin, nmax)
        has_near = jnp.abs(near) < jnp.inf
        near_k = jnp.where(has_near, _f32_to_key(near), zero_k)
        cr, ge1 = probe(cr, near_k)
        settled = jnp.logical_and(has_near, ge0 != ge1)
        return cr[0], cr[1], cr[2], jnp.where(settled, 1.0, 0.0)

    carry = lax.cond(searching, zero_steps, lambda cr: cr, carry)
    n_fixed = jnp.where(searching, BISECT_FIXED, 0)
    carry = lax.fori_loop(0, n_fixed, lambda _, cr: bisect(cr), carry)

    def search_more(state):
        it, _, cr = state
        for _ in range(BISECT_PER_CHECK):
            cr = bisect(cr)
        return it + 1, unresolved(cr), cr

    _, _, (lo_k, hi_k, c_lo, _) = lax.while_loop(
        lambda st: jnp.logical_and(st[1] > 0, st[0] < BISECT_MAX_CHECKS),
        search_more, (jnp.int32(0), unresolved(carry), carry))
    lo = _key_to_f32(lo_k)
    hi = _key_to_f32(hi_k)

    has_ties = jnp.max(jnp.where(c_lo > kf, 1.0, 0.0)) > 0.5

    @pl.when(jnp.logical_not(has_ties))
    def _():
        def mask_chunk(kt, _):
            r0 = pl.multiple_of(kt * TK, TK)
            xs = sc_ref[pl.ds(r0, TK), :]
            kext_ref[pl.ds(r0, TK), KV_LATENT:KV_LATENT + TQ] = (
                jnp.where(xs >= lo, 0.0, MASK_NEG).astype(BF16))
            return 0
        lax.fori_loop(0, n_kt, mask_chunk, 0)

    @pl.when(has_ties)
    def _():
        need = kf - count_ge(hi)
        tri = jnp.where(lax.broadcasted_iota(jnp.int32, (TK, TK), 1)
                        <= lax.broadcasted_iota(jnp.int32, (TK, TK), 0), 1.0, 0.0).astype(BF16)

        def mask_chunk(kt, before):
            r0 = pl.multiple_of(kt * TK, TK)
            xs = sc_ref[pl.ds(r0, TK), :]
            band = jnp.where(xs >= lo, jnp.where(xs < hi, 1.0, 0.0), 0.0)
            incl = jnp.dot(tri, band.astype(BF16), preferred_element_type=F32)
            rank = incl - band + before
            keep = jnp.where(xs >= hi, 1.0, jnp.where(rank < need, band, 0.0))
            kext_ref[pl.ds(r0, TK), KV_LATENT:KV_LATENT + TQ] = (
                jnp.where(keep > 0.0, 0.0, MASK_NEG).astype(BF16))
            return before + incl[TK - 1:TK, :]

        lax.fori_loop(0, n_kt, mask_chunk, jnp.zeros((1, TQ), F32))


    def qk_block(r0, rows, kt0, mx):
        st = jnp.dot(kext_ref[pl.ds(r0, rows), :], lhst_ref[...], preferred_element_type=F32)
        posb = pos_ref[pl.ds(r0, rows), :]
        tops = []
        for h in range(N_HEADS):
            sh = st[:, h * TQ:(h + 1) * TQ] + posb * (ALIBI_SLOPES[h] * LOG2E)
            s_ref[pl.ds(r0, rows), h * TQ:(h + 1) * TQ] = sh
            tops.append(fold8(sh, jnp.max))
        return jnp.maximum(mx, jnp.concatenate(tops, axis=1))

    mx = walk(qk_block, jnp.full((SUBLANES, N_HEADS * TQ), M_INIT, F32))
    m = jnp.max(mx, axis=0, keepdims=True)
    acc_ref[...] = jnp.zeros(acc_ref.shape, F32)

    def pv_block(r0, rows, kt0, _):
        pt = jnp.exp2(s_ref[pl.ds(r0, rows), :] - m).astype(BF16)
        vt = jnp.concatenate([vt_ref[kt0 + k] for k in range(rows // TK)], axis=1)
        acc_ref[...] += jnp.dot(vt, pt, preferred_element_type=F32)
        return 0

    walk(pv_block, 0)

    o_parts = []
    for p in range(N_HEADS // 2):
        cols = slice(p * 2 * TQ, (p + 1) * 2 * TQ)
        xt = (acc_ref[0:KV_LATENT, cols] / acc_ref[KV_LATENT:KV_LATENT + 1, cols]).T
        pair = jnp.concatenate([xt[:TQ], xt[TQ:]], axis=1).astype(BF16)
        o_parts.append(jnp.dot(pair, wuv_ref[p], preferred_element_type=F32))
    o = jnp.concatenate(o_parts, axis=1)
    slot = i % (TM // TQ)
    sg = sg_ref[0, pl.ds(pl.multiple_of(blk * TQ, TQ), TQ), :]
    ubuf_ref[pl.ds(pl.multiple_of(slot * TQ, TQ), TQ), :] = (o * sg.astype(F32)).astype(BF16)

    @pl.when(slot == TM // TQ - 1)
    def _():
        y_attn = jnp.dot(ubuf_ref[...], wap_ref[...], preferred_element_type=F32)
        mixed = (ga_ref[0].astype(F32) * y_attn + r_ref[0].astype(F32)).astype(BF16)
        res = x_ref[0] + jnp.dot(mixed, wout_ref[...], preferred_element_type=F32)
        out_ref[0] = (res * lax.rsqrt(jnp.mean(res * res, axis=-1, keepdims=True) + NORM_EPS)
                      * fg_ref[...])


def _const_spec(shape):
    nd = len(shape)
    return pl.BlockSpec(shape, lambda b, i: (0,) * nd, pipeline_mode=pl.Buffered(1))


def _pack_w_in(w_in):
    offs = np.cumsum([0, ATTN_WIDTH, KV_LATENT, IDX_HEADS * IDX_DIM, IDX_DIM, IDX_HEADS,
                      ATTN_WIDTH, RNN_WIDTH, RNN_WIDTH, 2 * D_MODEL])
    seg = [w_in[:, offs[k]:offs[k + 1]] for k in range(9)]
    seg[8] = seg[8] * 0.5
    pad = jnp.zeros((D_MODEL, LANES - IDX_DIM - IDX_HEADS), w_in.dtype)
    return jnp.concatenate(seg[:5] + [pad] + seg[5:], axis=1).astype(BF16)


def _pair_block_diag(w):
    n2, r, c = w.shape
    w = w.reshape(n2 // 2, 2, r, c)
    z = jnp.zeros((n2 // 2, r, c), w.dtype)
    top = jnp.concatenate([w[:, 0], z], axis=2)
    bot = jnp.concatenate([z, w[:, 1]], axis=2)
    return jnp.concatenate([top, bot], axis=1)


def _layer(x, norm_g, w_in, b_merge, kv_g, w_uk, w_uv, ln_g, ln_b, w_ap, conv_w, conv_b,
           w_a, b_a, w_x, b_x, lam, w_rp, w_out, out_g):
    B, S, _ = x.shape
    assert S % TM == 0 and TM % TK == 0 and TK % TQ == 0 and TQ == KC
    assert S % CC == 0 and CC % TK == 0 and S % (4 * TK) == 0 and TM % (QB * TQ) == 0
    topk = min(TOPK_MAX, S // 4)
    row = lambda v: v.reshape(1, -1).astype(F32)
    lane_pad = lambda v: jnp.pad(v.astype(F32), (0, LANES - v.shape[0])).reshape(1, LANES)

    w_all = _pack_w_in(w_in)
    wg = (0.5 * jnp.concatenate([_pair_block_diag(w_a), _pair_block_diag(w_x)], axis=2)).astype(BF16)
    wuk2t = _pair_block_diag(jnp.swapaxes(w_uk, 1, 2)).astype(BF16)

    nq = S // TQ
    tile = lambda n: pl.BlockSpec((1, TM, n), lambda b, j: (b, j, 0))
    outs = pl.pallas_call(
        _proj_kernel,
        grid=(B, S // TM),
        in_specs=[
            tile(D_MODEL),
            _const_spec((1, D_MODEL)),
            _const_spec((D_MODEL, C_END)),
            _const_spec((N_HEADS // 2, 2 * HEAD_DIM, 2 * KV_LATENT)),
            _const_spec((1, KV_LATENT)),
            _const_spec((1, LANES)),
            _const_spec((1, LANES)),
            _const_spec((CONV_WIDTH, RNN_WIDTH)),
            _const_spec((1, RNN_WIDTH)),
            _const_spec((RNN_WIDTH // LANES, LANES, 2 * LANES)),
            _const_spec((1, RNN_WIDTH)),
            _const_spec((1, RNN_WIDTH)),
            _const_spec((1, RNN_WIDTH)),
            _const_spec((RNN_WIDTH, D_MODEL)),
            _const_spec((1, 2 * D_MODEL)),
        ],
        out_specs=[
            pl.BlockSpec((1, TM // TQ, KV_LATENT, N_HEADS * TQ), lambda b, j: (b, j, 0, 0)),
            tile(KV_LATENT),
            pl.BlockSpec((1, TM // TK, KV_LATENT, TK), lambda b, j: (b, j, 0, 0)),
            pl.BlockSpec((1, TM // TQ, IDX_DIM, IDX_HEADS * TQ), lambda b, j: (b, j, 0, 0)),
            tile(IDX_DIM),
            pl.BlockSpec((1, TM // TQ, IDX_HEADS, TQ), lambda b, j: (b, j, 0, 0)),
            tile(ATTN_WIDTH),
            tile(D_MODEL),
            tile(D_MODEL),
        ],
        out_shape=[
            jax.ShapeDtypeStruct((B, nq, KV_LATENT, N_HEADS * TQ), BF16),
            jax.ShapeDtypeStruct((B, S, KV_LATENT), BF16),
            jax.ShapeDtypeStruct((B, S // TK, KV_LATENT, TK), BF16),
            jax.ShapeDtypeStruct((B, nq, IDX_DIM, IDX_HEADS * TQ), BF16),
            jax.ShapeDtypeStruct((B, S, IDX_DIM), BF16),
            jax.ShapeDtypeStruct((B, nq, IDX_HEADS, TQ), F32),
            jax.ShapeDtypeStruct((B, S, ATTN_WIDTH), BF16),
            jax.ShapeDtypeStruct((B, S, D_MODEL), BF16),
            jax.ShapeDtypeStruct((B, S, D_MODEL), BF16),
        ],
        scratch_shapes=[
            pltpu.VMEM((TM + SUBLANES, RNN_WIDTH), F32),
            pltpu.VMEM((TM, RNN_WIDTH), F32),
            pltpu.VMEM((TM, RNN_WIDTH), F32),
            pltpu.VMEM((TM, RNN_WIDTH), F32),
            pltpu.VMEM((SUBLANES, RNN_WIDTH), F32),
            pltpu.VMEM((TM, RNN_WIDTH), F32),
            pltpu.VMEM((TM, D_MODEL), F32),
        ],
        compiler_params=pltpu.CompilerParams(
            dimension_semantics=("arbitrary", "arbitrary"),
            vmem_limit_bytes=VMEM_LIMIT_BYTES),
        name="proj_rglru",
    )(x, row(norm_g), w_all, wuk2t, row(kv_g), lane_pad(ln_g), lane_pad(ln_b), conv_w.astype(F32),
      row(conv_b), wg, row(0.5 * b_a), row(0.5 * b_x), row(lam), w_rp.astype(BF16),
      row(0.5 * b_merge))
    lq, c, ct, qit, ki, wt, sg, ga, r = outs

    wuv2 = _pair_block_diag(w_uv).astype(BF16)

    steps_per_out = TM // (QB * TQ)
    otile = lambda n: pl.BlockSpec((1, TM, n), lambda b, p: (b, p // steps_per_out, 0))
    return pl.pallas_call(
        functools.partial(_attn_kernel, topk),
        grid=(B, nq // QB),
        in_specs=[
            pl.BlockSpec((1, QB, KV_LATENT, N_HEADS * TQ), lambda b, p: (b, p, 0, 0)),
            pl.BlockSpec((1, S, KV_LATENT), lambda b, p: (b, 0, 0)),
            pl.BlockSpec((1, S // TK, KV_LATENT, TK), lambda b, p: (b, 0, 0, 0)),
            pl.BlockSpec((1, QB, IDX_DIM, IDX_HEADS * TQ), lambda b, p: (b, p, 0, 0)),
            pl.BlockSpec((1, S, IDX_DIM), lambda b, p: (b, 0, 0)),
            pl.BlockSpec((1, QB, IDX_HEADS, TQ), lambda b, p: (b, p, 0, 0)),
            pl.BlockSpec((1, QB * TQ, ATTN_WIDTH), lambda b, p: (b, p, 0)),
            _const_spec((N_HEADS // 2, 2 * KV_LATENT, 2 * HEAD_DIM)),
            otile(D_MODEL),
            otile(D_MODEL),
            otile(D_MODEL),
            _const_spec((ATTN_WIDTH, D_MODEL)),
            _const_spec((D_MODEL, D_MODEL)),
            _const_spec((1, D_MODEL)),
        ],
        out_specs=otile(D_MODEL),
        out_shape=jax.ShapeDtypeStruct((B, S, D_MODEL), F32),
        scratch_shapes=[
            pltpu.VMEM((2 * KV_LATENT, N_HEADS * TQ), BF16),
            pltpu.VMEM((S, 2 * KV_LATENT), BF16),
            pltpu.VMEM((S // TK, VROWS, TK), BF16),
            pltpu.VMEM((S, LANES), F32),
            pltpu.VMEM((S, TQ), F32),
            pltpu.VMEM((S, N_HEADS * TQ), F32),
            pltpu.VMEM((VROWS, N_HEADS * TQ), F32),
            pltpu.VMEM((TM, ATTN_WIDTH), BF16),
        ],
        compiler_params=pltpu.CompilerParams(
            dimension_semantics=("arbitrary", "arbitrary"),
            vmem_limit_bytes=VMEM_LIMIT_BYTES),
        name="sparse_attn_out",
    )(lq, c, ct, qit, ki, wt, sg, wuv2, ga, r, x, w_ap.astype(BF16), w_out.astype(BF16),
      row(out_g))


def kernel(x, norm_gain, w_in, b_merge, kv_norm_gain, w_uk, w_uv, idx_ln_gain, idx_ln_bias,
           w_attn_proj, conv_w, conv_b, w_rg_a, b_rg_a, w_rg_x, b_rg_x, lru_lambda,
           w_rnn_proj, w_out, final_norm_gain):
    assert norm_gain.shape[0] == 1, "only the stated depth-1 stack is supported"
    return _layer(x, norm_gain[0], w_in[0], b_merge[0], kv_norm_gain[0], w_uk[0], w_uv[0],
                  idx_ln_gain[0], idx_ln_bias[0], w_attn_proj[0], conv_w[0], conv_b[0],
                  w_rg_a[0], b_rg_a[0], w_rg_x[0], b_rg_x[0], lru_lambda[0],
                  w_rnn_proj[0], w_out[0], final_norm_gain)
```

```python
import functools

import numpy as np
import jax
import jax.numpy as jnp
from jax import lax
from jax.experimental import pallas as pl
from jax.experimental.pallas import tpu as pltpu

D_MODEL = 1024
N_HEADS = 16
HEAD_DIM = 64
ATTN_WIDTH = N_HEADS * HEAD_DIM
KV_LATENT = 128
IDX_HEADS = 8
IDX_DIM = 64
TOPK_MAX = 256
RNN_WIDTH = 1024
RNN_BLOCKS = 16
RNN_BLOCK_DIM = RNN_WIDTH // RNN_BLOCKS
CONV_WIDTH = 4
LRU_C = 8.0
NORM_EPS = 1e-6

F32 = jnp.float32
BF16 = jnp.bfloat16

LANES = 128
SUBLANES = 8
VMEM_LIMIT_BYTES = 56 * 1024 * 1024

TM = 512
TQ = 128
QB = 4
KC = 128
TK = 256
CC = 512
VROWS = 144
BISECT_FIXED = 18
BISECT_PER_CHECK = 2
BISECT_MAX_CHECKS = 8
LOG2E = 1.4426950408889634
F32_MIN_NORMAL = 2.0 ** -126
MASK_NEG = -(2.0 ** 100)
M_INIT = -1e20

C_Q = 0
C_C = C_Q + ATTN_WIDTH
C_QI = C_C + KV_LATENT
C_KW = C_QI + IDX_HEADS * IDX_DIM
C_GA = C_KW + LANES
C_XR = C_GA + ATTN_WIDTH
C_GR = C_XR + RNN_WIDTH
C_MG = C_GR + RNN_WIDTH
C_END = C_MG + 2 * D_MODEL

ALIBI_SLOPES = [float(2.0 ** (-8.0 * (h + 1) / N_HEADS)) for h in range(N_HEADS)]


def _sigmoid_of_half(h):
    return 0.5 * jnp.tanh(h) + 0.5


def _silu(v):
    h = 0.5 * v
    return h * jnp.tanh(h) + h


def _f32_to_key(v):
    b = lax.bitcast_convert_type(v, jnp.int32)
    return b ^ ((b >> 31) & 0x7FFFFFFF)


def _key_to_f32(k):
    return lax.bitcast_convert_type(k ^ ((k >> 31) & 0x7FFFFFFF), F32)


def _proj_kernel(x_ref, ng_ref, w_ref, wuk_ref, kvg_ref, lng_ref, lnb_ref, cw_ref, cb_ref, wg_ref,
                 ba_ref, bx_ref, lam_ref, wrp_ref, bm_ref,
                 lq_ref, c_ref, ct_ref, qit_ref, ki_ref, wt_ref, sg_ref, ga_ref, r_ref,
                 xext_ref, a_ref, b_ref, h_ref, hc_ref, szr_ref, gr_ref):
    j = pl.program_id(1)

    @pl.when(j == 0)
    def _():
        xext_ref[0:SUBLANES, :] = jnp.zeros((SUBLANES, RNN_WIDTH), F32)
        hc_ref[...] = jnp.zeros((SUBLANES, RNN_WIDTH), F32)

    @pl.when(j > 0)
    def _():
        xext_ref[0:SUBLANES, :] = xext_ref[TM:TM + SUBLANES, :]

    x = x_ref[0]
    xn = x * lax.rsqrt(jnp.mean(x * x, axis=-1, keepdims=True) + NORM_EPS) * ng_ref[...]
    xb = xn.astype(BF16)

    def proj(lo, hi):
        return jnp.dot(xb, w_ref[:, lo:hi], preferred_element_type=F32)

    xext_ref[SUBLANES:SUBLANES + TM, :] = proj(C_XR, C_GR)

    def rnn_inputs(blk):
        cols = slice(blk * 2 * LANES, (blk + 1) * 2 * LANES)
        y = cb_ref[:, cols]
        for k in range(CONV_WIDTH):
            off = SUBLANES - (CONV_WIDTH - 1) + k
            y = y + cw_ref[k:k + 1, cols] * xext_ref[off:off + TM, cols]
        yb = y.astype(BF16)
        g = [jnp.dot(yb[:, h * LANES:(h + 1) * LANES], wg_ref[2 * blk + h],
                     preferred_element_type=F32) for h in range(2)]
        ta = jnp.tanh(jnp.concatenate([g[0][:, :LANES], g[1][:, :LANES]], axis=1) + ba_ref[:, cols])
        ig = _sigmoid_of_half(
            jnp.concatenate([g[0][:, LANES:], g[1][:, LANES:]], axis=1) + bx_ref[:, cols])
        nl = -lam_ref[:, cols]
        softplus = jnp.maximum(nl, 0.0) + jnp.log1p(jnp.exp(-jnp.abs(nl)))
        half_rate = (-0.5 * LRU_C) * softplus
        log_a = half_rate * ta + half_rate
        a_ref[:, cols] = jnp.exp(log_a)
        th = jnp.tanh(log_a)
        b_ref[:, cols] = jnp.sqrt(-2.0 * th / (1.0 - th)) * (ig * y)

    def attn_queries():
        qb16 = proj(C_Q, C_C).astype(BF16)
        ql = [jnp.dot(qb16[:, p * LANES:(p + 1) * LANES], wuk_ref[p], preferred_element_type=F32)
              for p in range(N_HEADS // 2)]
        qlt = (jnp.concatenate(ql, axis=1) * (HEAD_DIM ** -0.5 * LOG2E)).T
        for qb in range(TM // TQ):
            for h in range(N_HEADS):
                lq_ref[0, qb, :, h * TQ:(h + 1) * TQ] = (
                    qlt[h * KV_LATENT:(h + 1) * KV_LATENT, qb * TQ:(qb + 1) * TQ].astype(BF16))

    def attn_keys_and_indexer():
        zc = proj(C_C, C_QI)
        cn = zc * lax.rsqrt(jnp.mean(zc * zc, axis=-1, keepdims=True) + NORM_EPS) * kvg_ref[...]
        c_ref[0] = cn.astype(BF16)
        cnt = cn.T
        for k in range(TM // TK):
            ct_ref[0, k] = cnt[:, k * TK:(k + 1) * TK].astype(BF16)

        zqt = (proj(C_QI, C_KW) * (IDX_DIM ** -0.5)).T
        for qb in range(TM // TQ):
            for jh in range(IDX_HEADS):
                qit_ref[0, qb, :, jh * TQ:(jh + 1) * TQ] = (
                    zqt[jh * IDX_DIM:(jh + 1) * IDX_DIM, qb * TQ:(qb + 1) * TQ].astype(BF16))

        zk = proj(C_KW, C_GA)
        lane = lax.broadcasted_iota(jnp.int32, zk.shape, 1)
        is_k = lane < IDX_DIM
        mu = jnp.sum(jnp.where(is_k, zk, 0.0), axis=-1, keepdims=True) * (1.0 / IDX_DIM)
        kc = jnp.where(is_k, zk - mu, 0.0)
        var = jnp.sum(kc * kc, axis=-1, keepdims=True) * (1.0 / IDX_DIM)
        kin = kc * lax.rsqrt(var + NORM_EPS) * lng_ref[...] + lnb_ref[...]
        ki_ref[0] = kin[:, :IDX_DIM].astype(BF16)
        wtt = zk.T[IDX_DIM:IDX_DIM + IDX_HEADS, :] * (IDX_HEADS ** -0.5)
        for qb in range(TM // TQ):
            wt_ref[0, qb] = wtt[:, qb * TQ:(qb + 1) * TQ]

    def attn_gate():
        zg = proj(C_GA, C_XR)
        sg_ref[0] = _silu(zg).astype(BF16)

    rnn_inputs(0)
    ga_ref[0] = _sigmoid_of_half(proj(C_MG, C_MG + D_MODEL) + bm_ref[:, :D_MODEL]).astype(BF16)
    rnn_inputs(1)
    gr_ref[...] = _sigmoid_of_half(proj(C_MG + D_MODEL, C_END) + bm_ref[:, D_MODEL:])
    rnn_inputs(2)
    zr = proj(C_GR, C_MG)
    szr_ref[...] = _silu(zr)
    rnn_inputs(3)

    row = lax.broadcasted_iota(jnp.int32, (SUBLANES, RNN_WIDTH), 0)

    hprev = hc_ref[...]
    for ci in range(TM // SUBLANES):
        rows = slice(ci * SUBLANES, (ci + 1) * SUBLANES)
        av = a_ref[rows, :]
        bv = b_ref[rows, :]
        for d in (1, 2, 4):
            a_sh = jnp.where(row >= d, pltpu.roll(av, d, 0), 1.0)
            b_sh = jnp.where(row >= d, pltpu.roll(bv, d, 0), 0.0)
            bv = av * b_sh + bv
            av = av * a_sh
        hv = av * hprev + bv
        h_ref[rows, :] = hv
        hprev = jnp.broadcast_to(hv[SUBLANES - 1:SUBLANES, :], (SUBLANES, RNN_WIDTH))
    hc_ref[...] = hprev

    attn_queries()
    attn_keys_and_indexer()
    attn_gate()

    u = (h_ref[...] * szr_ref[...]).astype(BF16)
    y_rnn = jnp.dot(u, wrp_ref[...], preferred_element_type=F32)
    r_ref[0] = (gr_ref[...] * y_rnn).astype(BF16)


def _attn_kernel(topk, *refs):
    first_row = pl.program_id(0) == 0
    step = pl.program_id(1)

    def one_block(blk, _):
        _attn_block(topk, step * QB + blk, blk, first_row, *refs)
        return 0

    lax.fori_loop(0, QB, one_block, 0)


def _attn_block(topk, i, blk, first_row,
                lq_ref, c_ref, ct_ref, qit_ref, ki_ref, wt_ref, sg_ref, wuv_ref,
                ga_ref, r_ref, x_ref, wap_ref, wout_ref, fg_ref,
                out_ref,
                lhst_ref, kext_ref, vt_ref, pos_ref, sc_ref, s_ref, acc_ref, ubuf_ref):
    seq = kext_ref.shape[0]
    n_kt = (i * TQ + TQ + TK - 1) // TK
    n_cc = (i * TQ + TQ + CC - 1) // CC

    @pl.when(jnp.logical_and(first_row, i == 0))
    def _():
        er = lax.broadcasted_iota(jnp.int32, (TQ, TQ), 0)
        ec = lax.broadcasted_iota(jnp.int32, (TQ, TQ), 1)
        eye = jnp.where(er == ec, 1.0, 0.0).astype(BF16)
        for h in range(N_HEADS):
            lhst_ref[KV_LATENT:KV_LATENT + TQ, h * TQ:(h + 1) * TQ] = eye
        pos_ref[...] = lax.broadcasted_iota(jnp.int32, (seq, LANES), 0).astype(F32)
        rr = lax.broadcasted_iota(jnp.int32, (VROWS - KV_LATENT, TK), 0)
        for k in range(seq // TK):
            vt_ref[k, KV_LATENT:VROWS, :] = jnp.where(rr == 0, 1.0, 0.0).astype(BF16)

    @pl.when(i == 0)
    def _():
        kext_ref[:, 0:KV_LATENT] = c_ref[0]
        for k in range(seq // TK):
            vt_ref[k, 0:KV_LATENT, :] = ct_ref[0, k]

    lhst_ref[0:KV_LATENT, :] = lq_ref[0, blk]

    qit = qit_ref[0, blk]
    wv = wt_ref[0, blk]
    tpos = i * TQ + lax.broadcasted_iota(jnp.int32, (1, TQ), 1)

    def fold8(v, op):
        parts = [op(v[k * KC:(k + 1) * KC].reshape(KC // SUBLANES, SUBLANES, v.shape[1]), axis=0)
                 for k in range(v.shape[0] // KC)]
        return op(jnp.stack(parts), axis=0)

    def walk(block_fn, carry):
        n4 = n_kt // 4
        carry = lax.fori_loop(
            0, n4, lambda k, cr: block_fn(pl.multiple_of(k * 4 * TK, 4 * TK), 4 * TK, k * 4, cr),
            carry)
        kt2 = n4 * 4
        has2 = n_kt % 4 >= 2
        carry = lax.cond(
            has2, lambda cr: block_fn(pl.multiple_of(kt2 * TK, 2 * TK), 2 * TK, kt2, cr),
            lambda cr: cr, carry)
        kt1 = kt2 + jnp.where(has2, 2, 0)
        return lax.cond(
            n_kt % 2 == 1, lambda cr: block_fn(pl.multiple_of(kt1 * TK, TK), TK, kt1, cr),
            lambda cr: cr, carry)

    def score_block(r0, rows, kt0, carry):
        lo8, hi8, pos8, neg8 = carry
        prod = jnp.dot(ki_ref[0, pl.ds(r0, rows), :], qit, preferred_element_type=F32)
        sc = jnp.maximum(prod[:, 0:TQ], 0.0) * wv[0:1, :]
        for jh in range(1, IDX_HEADS):
            sc = sc + jnp.maximum(prod[:, jh * TQ:(jh + 1) * TQ], 0.0) * wv[jh:jh + 1, :]
        causal = (lax.broadcasted_iota(jnp.int32, (rows, TQ), 0) + r0) <= tpos
        sc_ref[pl.ds(r0, rows), :] = jnp.where(causal, sc, -jnp.inf)
        lo8 = jnp.minimum(lo8, fold8(jnp.where(causal, sc, jnp.inf), jnp.min))
        hi8 = jnp.maximum(hi8, fold8(jnp.where(causal, sc, -jnp.inf), jnp.max))
        scz = jnp.where(causal, sc, 0.0)
        pos8 = jnp.minimum(pos8, fold8(jnp.where(scz > 0.0, scz, jnp.inf), jnp.min))
        neg8 = jnp.maximum(neg8, fold8(jnp.where(scz < 0.0, scz, -jnp.inf), jnp.max))
        return lo8, hi8, pos8, neg8

    inf8 = jnp.full((SUBLANES, TQ), jnp.inf, F32)
    lo8, hi8, pos8, neg8 = walk(score_block, (inf8, -inf8, inf8, -inf8))

    @pl.when(n_kt * TK < n_cc * CC)
    def _():
        r0 = pl.multiple_of(n_kt * TK, TK)
        sc_ref[pl.ds(r0, TK), :] = jnp.full((TK, TQ), -jnp.inf, F32)

    lo_k = _f32_to_key(jnp.min(lo8, axis=0, keepdims=True))
    vmax = jnp.max(hi8, axis=0, keepdims=True)
    hi_k = jnp.where(jnp.abs(vmax) < F32_MIN_NORMAL, _f32_to_key(jnp.full_like(vmax, F32_MIN_NORMAL)),
                     _f32_to_key(vmax) + 1)
    kf = float(topk)

    def count_ge(thr):
        def body(cc, c8):
            r0 = pl.multiple_of(cc * CC, CC)
            xs = sc_ref[pl.ds(r0, CC), :]
            return c8 + fold8(jnp.where(xs >= thr, 1.0, 0.0), jnp.sum)
        c8 = lax.fori_loop(0, n_cc, body, jnp.zeros((SUBLANES, TQ), F32))
        return jnp.sum(c8, axis=0, keepdims=True)

    def midpoint(lo_, hi_):
        return (lo_ >> 1) + (hi_ >> 1) + (lo_ & hi_ & 1)

    def probe(carry, thr_k):
        lo_, hi_, clo_, done_ = carry
        cnt = count_ge(_key_to_f32(thr_k))
        ge = cnt >= kf
        up = jnp.logical_and(ge, thr_k > lo_)
        down = jnp.logical_and(jnp.logical_not(ge), thr_k < hi_)
        return (jnp.where(up, thr_k, lo_), jnp.where(down, thr_k, hi_),
                jnp.where(up, cnt, clo_), done_), ge

    def bisect(carry):
        return probe(carry, midpoint(carry[0], carry[1]))[0]

    def unresolved(carry):
        lo_, hi_, clo_, done_ = carry
        open_ = jnp.where(clo_ > kf, jnp.where(midpoint(lo_, hi_) != lo_, 1.0 - done_, 0.0), 0.0)
        return (jnp.max(open_) > 0.5).astype(jnp.int32)

    n_causal = (tpos[0:1, :] + 1).astype(F32)
    searching = (i + 1) * TQ > topk
    carry = (lo_k, hi_k, n_causal, jnp.zeros((1, TQ), F32))

    def zero_steps(cr):
        zero_k = jnp.zeros((1, TQ), jnp.int32)
        cr, ge0 = probe(cr, zero_k)
        pmin = jnp.min(pos8, axis=0, keepdims=True)
        nmax = jnp.max(neg8, axis=0, keepdims=True)
        near = jnp.where(ge0, pmin, nmax)
        has_near = jnp.abs(near) < jnp.inf
        near_k = jnp.where(has_near, _f32_to_key(near), zero_k)
        cr, ge1 = probe(cr, near_k)
        settled = jnp.logical_and(has_near, ge0 != ge1)
        return cr[0], cr[1], cr[2], jnp.where(settled, 1.0, 0.0)

    carry = lax.cond(searching, zero_steps, lambda cr: cr, carry)
    n_fixed = jnp.where(searching, BISECT_FIXED, 0)
    carry = lax.fori_loop(0, n_fixed, lambda _, cr: bisect(cr), carry)

    def search_more(state):
        it, _, cr = state
        for _ in range(BISECT_PER_CHECK):
            cr = bisect(cr)
        return it + 1, unresolved(cr), cr

    _, _, (lo_k, hi_k, c_lo, _) = lax.while_loop(
        lambda st: jnp.logical_and(st[1] > 0, st[0] < BISECT_MAX_CHECKS),
        search_more, (jnp.int32(0), unresolved(carry), carry))
    lo = _key_to_f32(lo_k)
    hi = _key_to_f32(hi_k)

    has_ties = jnp.max(jnp.where(c_lo > kf, 1.0, 0.0)) > 0.5

    @pl.when(jnp.logical_not(has_ties))
    def _():
        def mask_chunk(kt, _):
            r0 = pl.multiple_of(kt * TK, TK)
            xs = sc_ref[pl.ds(r0, TK), :]
            kext_ref[pl.ds(r0, TK), KV_LATENT:KV_LATENT + TQ] = (
                jnp.where(xs >= lo, 0.0, MASK_NEG).astype(BF16))
            return 0
        lax.fori_loop(0, n_kt, mask_chunk, 0)

    @pl.when(has_ties)
    def _():
        need = kf - count_ge(hi)
        tri = jnp.where(lax.broadcasted_iota(jnp.int32, (TK, TK), 1)
                        <= lax.broadcasted_iota(jnp.int32, (TK, TK), 0), 1.0, 0.0).astype(BF16)

        def mask_chunk(kt, before):
            r0 = pl.multiple_of(kt * TK, TK)
            xs = sc_ref[pl.ds(r0, TK), :]
            band = jnp.where(xs >= lo, jnp.where(xs < hi, 1.0, 0.0), 0.0)
            incl = jnp.dot(tri, band.astype(BF16), preferred_element_type=F32)
            rank = incl - band + before
            keep = jnp.where(xs >= hi, 1.0, jnp.where(rank < need, band, 0.0))
            kext_ref[pl.ds(r0, TK), KV_LATENT:KV_LATENT + TQ] = (
                jnp.where(keep > 0.0, 0.0, MASK_NEG).astype(BF16))
            return before + incl[TK - 1:TK, :]

        lax.fori_loop(0, n_kt, mask_chunk, jnp.zeros((1, TQ), F32))


    def qk_block(r0, rows, kt0, mx):
        st = jnp.dot(kext_ref[pl.ds(r0, rows), :], lhst_ref[...], preferred_element_type=F32)
        posb = pos_ref[pl.ds(r0, rows), :]
        tops = []
        for h in range(N_HEADS):
            sh = st[:, h * TQ:(h + 1) * TQ] + posb * (ALIBI_SLOPES[h] * LOG2E)
            s_ref[pl.ds(r0, rows), h * TQ:(h + 1) * TQ] = sh
            tops.append(fold8(sh, jnp.max))
        return jnp.maximum(mx, jnp.concatenate(tops, axis=1))

    mx = walk(qk_block, jnp.full((SUBLANES, N_HEADS * TQ), M_INIT, F32))
    m = jnp.max(mx, axis=0, keepdims=True)
    acc_ref[...] = jnp.zeros(acc_ref.shape, F32)

    def pv_block(r0, rows, kt0, _):
        pt = jnp.exp2(s_ref[pl.ds(r0, rows), :] - m).astype(BF16)
        vt = jnp.concatenate([vt_ref[kt0 + k] for k in range(rows // TK)], axis=1)
        acc_ref[...] += jnp.dot(vt, pt, preferred_element_type=F32)
        return 0

    walk(pv_block, 0)

    o_parts = []
    for p in range(N_HEADS // 2):
        cols = slice(p * 2 * TQ, (p + 1) * 2 * TQ)
        xt = (acc_ref[0:KV_LATENT, cols] / acc_ref[KV_LATENT:KV_LATENT + 1, cols]).T
        pair = jnp.concatenate([xt[:TQ], xt[TQ:]], axis=1).astype(BF16)
        o_parts.append(jnp.dot(pair, wuv_ref[p], preferred_element_type=F32))
    o = jnp.concatenate(o_parts, axis=1)
    slot = i % (TM // TQ)
    sg = sg_ref[0, pl.ds(pl.multiple_of(blk * TQ, TQ), TQ), :]
    ubuf_ref[pl.ds(pl.multiple_of(slot * TQ, TQ), TQ), :] = (o * sg.astype(F32)).astype(BF16)

    @pl.when(slot == TM // TQ - 1)
    def _():
        y_attn = jnp.dot(ubuf_ref[...], wap_ref[...], preferred_element_type=F32)
        mixed = (ga_ref[0].astype(F32) * y_attn + r_ref[0].astype(F32)).astype(BF16)
        res = x_ref[0] + jnp.dot(mixed, wout_ref[...], preferred_element_type=F32)
        out_ref[0] = (res * lax.rsqrt(jnp.mean(res * res, axis=-1, keepdims=True) + NORM_EPS)
                      * fg_ref[...])


def _const_spec(shape):
    nd = len(shape)
    return pl.BlockSpec(shape, lambda b, i: (0,) * nd, pipeline_mode=pl.Buffered(1))


def _pack_w_in(w_in):
    offs = np.cumsum([0, ATTN_WIDTH, KV_LATENT, IDX_HEADS * IDX_DIM, IDX_DIM, IDX_HEADS,
                      ATTN_WIDTH, RNN_WIDTH, RNN_WIDTH, 2 * D_MODEL])
    seg = [w_in[:, offs[k]:offs[k + 1]] for k in range(9)]
    seg[8] = seg[8] * 0.5
    pad = jnp.zeros((D_MODEL, LANES - IDX_DIM - IDX_HEADS), w_in.dtype)
    return jnp.concatenate(seg[:5] + [pad] + seg[5:], axis=1).astype(BF16)


def _pair_block_diag(w):
    n2, r, c = w.shape
    w = w.reshape(n2 // 2, 2, r, c)
    z = jnp.zeros((n2 // 2, r, c), w.dtype)
    top = jnp.concatenate([w[:, 0], z], axis=2)
    bot = jnp.concatenate([z, w[:, 1]], axis=2)
    return jnp.concatenate([top, bot], axis=1)


def _layer(x, norm_g, w_in, b_merge, kv_g, w_uk, w_uv, ln_g, ln_b, w_ap, conv_w, conv_b,
           w_a, b_a, w_x, b_x, lam, w_rp, w_out, out_g):
    B, S, _ = x.shape
    assert S % TM == 0 and TM % TK == 0 and TK % TQ == 0 and TQ == KC
    assert S % CC == 0 and CC % TK == 0 and S % (4 * TK) == 0 and TM % (QB * TQ) == 0
    topk = min(TOPK_MAX, S // 4)
    row = lambda v: v.reshape(1, -1).astype(F32)
    lane_pad = lambda v: jnp.pad(v.astype(F32), (0, LANES - v.shape[0])).reshape(1, LANES)

    w_all = _pack_w_in(w_in)
    wg = (0.5 * jnp.concatenate([_pair_block_diag(w_a), _pair_block_diag(w_x)], axis=2)).astype(BF16)
    wuk2t = _pair_block_diag(jnp.swapaxes(w_uk, 1, 2)).astype(BF16)

    nq = S // TQ
    tile = lambda n: pl.BlockSpec((1, TM, n), lambda b, j: (b, j, 0))
    outs = pl.pallas_call(
        _proj_kernel,
        grid=(B, S // TM),
        in_specs=[
            tile(D_MODEL),
            _const_spec((1, D_MODEL)),
            _const_spec((D_MODEL, C_END)),
            _const_spec((N_HEADS // 2, 2 * HEAD_DIM, 2 * KV_LATENT)),
            _const_spec((1, KV_LATENT)),
            _const_spec((1, LANES)),
            _const_spec((1, LANES)),
            _const_spec((CONV_WIDTH, RNN_WIDTH)),
            _const_spec((1, RNN_WIDTH)),
            _const_spec((RNN_WIDTH // LANES, LANES, 2 * LANES)),
            _const_spec((1, RNN_WIDTH)),
            _const_spec((1, RNN_WIDTH)),
            _const_spec((1, RNN_WIDTH)),
            _const_spec((RNN_WIDTH, D_MODEL)),
            _const_spec((1, 2 * D_MODEL)),
        ],
        out_specs=[
            pl.BlockSpec((1, TM // TQ, KV_LATENT, N_HEADS * TQ), lambda b, j: (b, j, 0, 0)),
            tile(KV_LATENT),
            pl.BlockSpec((1, TM // TK, KV_LATENT, TK), lambda b, j: (b, j, 0, 0)),
            pl.BlockSpec((1, TM // TQ, IDX_DIM, IDX_HEADS * TQ), lambda b, j: (b, j, 0, 0)),
            tile(IDX_DIM),
            pl.BlockSpec((1, TM // TQ, IDX_HEADS, TQ), lambda b, j: (b, j, 0, 0)),
            tile(ATTN_WIDTH),
            tile(D_MODEL),
            tile(D_MODEL),
        ],
        out_shape=[
            jax.ShapeDtypeStruct((B, nq, KV_LATENT, N_HEADS * TQ), BF16),
            jax.ShapeDtypeStruct((B, S, KV_LATENT), BF16),
            jax.ShapeDtypeStruct((B, S // TK, KV_LATENT, TK), BF16),
            jax.ShapeDtypeStruct((B, nq, IDX_DIM, IDX_HEADS * TQ), BF16),
            jax.ShapeDtypeStruct((B, S, IDX_DIM), BF16),
            jax.ShapeDtypeStruct((B, nq, IDX_HEADS, TQ), F32),
            jax.ShapeDtypeStruct((B, S, ATTN_WIDTH), BF16),
            jax.ShapeDtypeStruct((B, S, D_MODEL), BF16),
            jax.ShapeDtypeStruct((B, S, D_MODEL), BF16),
        ],
        scratch_shapes=[
            pltpu.VMEM((TM + SUBLANES, RNN_WIDTH), F32),
            pltpu.VMEM((TM, RNN_WIDTH), F32),
            pltpu.VMEM((TM, RNN_WIDTH), F32),
            pltpu.VMEM((TM, RNN_WIDTH), F32),
            pltpu.VMEM((SUBLANES, RNN_WIDTH), F32),
            pltpu.VMEM((TM, RNN_WIDTH), F32),
            pltpu.VMEM((TM, D_MODEL), F32),
        ],
        compiler_params=pltpu.CompilerParams(
            dimension_semantics=("arbitrary", "arbitrary"),
            vmem_limit_bytes=VMEM_LIMIT_BYTES),
        name="proj_rglru",
    )(x, row(norm_g), w_all, wuk2t, row(kv_g), lane_pad(ln_g), lane_pad(ln_b), conv_w.astype(F32),
      row(conv_b), wg, row(0.5 * b_a), row(0.5 * b_x), row(lam), w_rp.astype(BF16),
      row(0.5 * b_merge))
    lq, c, ct, qit, ki, wt, sg, ga, r = outs

    wuv2 = _pair_block_diag(w_uv).astype(BF16)

    steps_per_out = TM // (QB * TQ)
    otile = lambda n: pl.BlockSpec((1, TM, n), lambda b, p: (b, p // steps_per_out, 0))
    return pl.pallas_call(
        functools.partial(_attn_kernel, topk),
        grid=(B, nq // QB),
        in_specs=[
            pl.BlockSpec((1, QB, KV_LATENT, N_HEADS * TQ), lambda b, p: (b, p, 0, 0)),
            pl.BlockSpec((1, S, KV_LATENT), lambda b, p: (b, 0, 0)),
            pl.BlockSpec((1, S // TK, KV_LATENT, TK), lambda b, p: (b, 0, 0, 0)),
            pl.BlockSpec((1, QB, IDX_DIM, IDX_HEADS * TQ), lambda b, p: (b, p, 0, 0)),
            pl.BlockSpec((1, S, IDX_DIM), lambda b, p: (b, 0, 0)),
            pl.BlockSpec((1, QB, IDX_HEADS, TQ), lambda b, p: (b, p, 0, 0)),
            pl.BlockSpec((1, QB * TQ, ATTN_WIDTH), lambda b, p: (b, p, 0)),
            _const_spec((N_HEADS // 2, 2 * KV_LATENT, 2 * HEAD_DIM)),
            otile(D_MODEL),
            otile(D_MODEL),
            otile(D_MODEL),
            _const_spec((ATTN_WIDTH, D_MODEL)),
            _const_spec((D_MODEL, D_MODEL)),
            _const_spec((1, D_MODEL)),
        ],
        out_specs=otile(D_MODEL),
        out_shape=jax.ShapeDtypeStruct((B, S, D_MODEL), F32),
        scratch_shapes=[
            pltpu.VMEM((2 * KV_LATENT, N_HEADS * TQ), BF16),
            pltpu.VMEM((S, 2 * KV_LATENT), BF16),
            pltpu.VMEM((S // TK, VROWS, TK), BF16),
            pltpu.VMEM((S, LANES), F32),
            pltpu.VMEM((S, TQ), F32),
            pltpu.VMEM((S, N_HEADS * TQ), F32),
            pltpu.VMEM((VROWS, N_HEADS * TQ), F32),
            pltpu.VMEM((TM, ATTN_WIDTH), BF16),
        ],
        compiler_params=pltpu.CompilerParams(
            dimension_semantics=("arbitrary", "arbitrary"),
            vmem_limit_bytes=VMEM_LIMIT_BYTES),
        name="sparse_attn_out",
    )(lq, c, ct, qit, ki, wt, sg, wuv2, ga, r, x, w_ap.astype(BF16), w_out.astype(BF16),
      row(out_g))


def kernel(x, norm_gain, w_in, b_merge, kv_norm_gain, w_uk, w_uv, idx_ln_gain, idx_ln_bias,
           w_attn_proj, conv_w, conv_b, w_rg_a, b_rg_a, w_rg_x, b_rg_x, lru_lambda,
           w_rnn_proj, w_out, final_norm_gain):
    assert norm_gain.shape[0] == 1, "only the stated depth-1 stack is supported"
    return _layer(x, norm_gain[0], w_in[0], b_merge[0], kv_norm_gain[0], w_uk[0], w_uv[0],
                  idx_ln_gain[0], idx_ln_bias[0], w_attn_proj[0], conv_w[0], conv_b[0],
                  w_rg_a[0], b_rg_a[0], w_rg_x[0], b_rg_x[0], lru_lambda[0],
                  w_rnn_proj[0], w_out[0], final_norm_gain)
```

```python
import functools

import numpy as np
import jax
import jax.numpy as jnp
from jax import lax
from jax.experimental import pallas as pl
from jax.experimental.pallas import tpu as pltpu

D_MODEL = 1024
N_HEADS = 16
HEAD_DIM = 64
ATTN_WIDTH = N_HEADS * HEAD_DIM
KV_LATENT = 128
IDX_HEADS = 8
IDX_DIM = 64
TOPK_MAX = 256
RNN_WIDTH = 1024
RNN_BLOCKS = 16
RNN_BLOCK_DIM = RNN_WIDTH // RNN_BLOCKS
CONV_WIDTH = 4
LRU_C = 8.0
NORM_EPS = 1e-6

F32 = jnp.float32
BF16 = jnp.bfloat16

LANES = 128
SUBLANES = 8
VMEM_LIMIT_BYTES = 56 * 1024 * 1024

TM = 512
TQ = 128
QB = 2
KC = 128
TK = 256
CC = 512
VROWS = 144
BISECT_FIXED = 18
BISECT_PER_CHECK = 2
BISECT_MAX_CHECKS = 8
LOG2E = 1.4426950408889634
F32_MIN_NORMAL = 2.0 ** -126
MASK_NEG = -(2.0 ** 100)
M_INIT = -1e20

C_Q = 0
C_C = C_Q + ATTN_WIDTH
C_QI = C_C + KV_LATENT
C_KW = C_QI + IDX_HEADS * IDX_DIM
C_GA = C_KW + LANES
C_XR = C_GA + ATTN_WIDTH
C_GR = C_XR + RNN_WIDTH
C_MG = C_GR + RNN_WIDTH
C_END = C_MG + 2 * D_MODEL

ALIBI_SLOPES = [float(2.0 ** (-8.0 * (h + 1) / N_HEADS)) for h in range(N_HEADS)]


def _sigmoid_of_half(h):
    return 0.5 * jnp.tanh(h) + 0.5


def _silu(v):
    h = 0.5 * v
    return h * jnp.tanh(h) + h


def _f32_to_key(v):
    b = lax.bitcast_convert_type(v, jnp.int32)
    return b ^ ((b >> 31) & 0x7FFFFFFF)


def _key_to_f32(k):
    return lax.bitcast_convert_type(k ^ ((k >> 31) & 0x7FFFFFFF), F32)


def _proj_kernel(x_ref, ng_ref, w_ref, wuk_ref, kvg_ref, lng_ref, lnb_ref, cw_ref, cb_ref, wg_ref,
                 ba_ref, bx_ref, lam_ref, wrp_ref, bm_ref,
                 lq_ref, c_ref, ct_ref, qit_ref, ki_ref, wt_ref, sg_ref, ga_ref, r_ref,
                 xext_ref, a_ref, b_ref, h_ref, hc_ref, szr_ref, gr_ref):
    j = pl.program_id(1)

    @pl.when(j == 0)
    def _():
        xext_ref[0:SUBLANES, :] = jnp.zeros((SUBLANES, RNN_WIDTH), F32)
        hc_ref[...] = jnp.zeros((SUBLANES, RNN_WIDTH), F32)

    @pl.when(j > 0)
    def _():
        xext_ref[0:SUBLANES, :] = xext_ref[TM:TM + SUBLANES, :]

    x = x_ref[0]
    xn = x * lax.rsqrt(jnp.mean(x * x, axis=-1, keepdims=True) + NORM_EPS) * ng_ref[...]
    xb = xn.astype(BF16)

    def proj(lo, hi):
        return jnp.dot(xb, w_ref[:, lo:hi], preferred_element_type=F32)

    xext_ref[SUBLANES:SUBLANES + TM, :] = proj(C_XR, C_GR)

    def rnn_inputs(blk):
        cols = slice(blk * 2 * LANES, (blk + 1) * 2 * LANES)
        y = cb_ref[:, cols]
        for k in range(CONV_WIDTH):
            off = SUBLANES - (CONV_WIDTH - 1) + k
            y = y + cw_ref[k:k + 1, cols] * xext_ref[off:off + TM, cols]
        yb = y.astype(BF16)
        g = [jnp.dot(yb[:, h * LANES:(h + 1) * LANES], wg_ref[2 * blk + h],
                     preferred_element_type=F32) for h in range(2)]
        ta = jnp.tanh(jnp.concatenate([g[0][:, :LANES], g[1][:, :LANES]], axis=1) + ba_ref[:, cols])
        ig = _sigmoid_of_half(
            jnp.concatenate([g[0][:, LANES:], g[1][:, LANES:]], axis=1) + bx_ref[:, cols])
        nl = -lam_ref[:, cols]
        softplus = jnp.maximum(nl, 0.0) + jnp.log1p(jnp.exp(-jnp.abs(nl)))
        half_rate = (-0.5 * LRU_C) * softplus
        log_a = half_rate * ta + half_rate
        a_ref[:, cols] = jnp.exp(log_a)
        th = jnp.tanh(log_a)
        b_ref[:, cols] = jnp.sqrt(-2.0 * th / (1.0 - th)) * (ig * y)

    def attn_queries():
        qb16 = proj(C_Q, C_C).astype(BF16)
        ql = [jnp.dot(qb16[:, p * LANES:(p + 1) * LANES], wuk_ref[p], preferred_element_type=F32)
              for p in range(N_HEADS // 2)]
        qlt = (jnp.concatenate(ql, axis=1) * (HEAD_DIM ** -0.5 * LOG2E)).T
        for qb in range(TM // TQ):
            for h in range(N_HEADS):
                lq_ref[0, qb, :, h * TQ:(h + 1) * TQ] = (
                    qlt[h * KV_LATENT:(h + 1) * KV_LATENT, qb * TQ:(qb + 1) * TQ].astype(BF16))

    def attn_keys_and_indexer():
        zc = proj(C_C, C_QI)
        cn = zc * lax.rsqrt(jnp.mean(zc * zc, axis=-1, keepdims=True) + NORM_EPS) * kvg_ref[...]
        c_ref[0] = cn.astype(BF16)
        cnt = cn.T
        for k in range(TM // TK):
            ct_ref[0, k] = cnt[:, k * TK:(k + 1) * TK].astype(BF16)

        zqt = (proj(C_QI, C_KW) * (IDX_DIM ** -0.5)).T
        for qb in range(TM // TQ):
            for jh in range(IDX_HEADS):
                qit_ref[0, qb, :, jh * TQ:(jh + 1) * TQ] = (
                    zqt[jh * IDX_DIM:(jh + 1) * IDX_DIM, qb * TQ:(qb + 1) * TQ].astype(BF16))

        zk = proj(C_KW, C_GA)
        lane = lax.broadcasted_iota(jnp.int32, zk.shape, 1)
        is_k = lane < IDX_DIM
        mu = jnp.sum(jnp.where(is_k, zk, 0.0), axis=-1, keepdims=True) * (1.0 / IDX_DIM)
        kc = jnp.where(is_k, zk - mu, 0.0)
        var = jnp.sum(kc * kc, axis=-1, keepdims=True) * (1.0 / IDX_DIM)
        kin = kc * lax.rsqrt(var + NORM_EPS) * lng_ref[...] + lnb_ref[...]
        ki_ref[0] = kin[:, :IDX_DIM].astype(BF16)
        wtt = zk.T[IDX_DIM:IDX_DIM + IDX_HEADS, :] * (IDX_HEADS ** -0.5)
        for qb in range(TM // TQ):
            wt_ref[0, qb] = wtt[:, qb * TQ:(qb + 1) * TQ]

    def attn_gate():
        zg = proj(C_GA, C_XR)
        sg_ref[0] = _silu(zg).astype(BF16)

    rnn_inputs(0)
    ga_ref[0] = _sigmoid_of_half(proj(C_MG, C_MG + D_MODEL) + bm_ref[:, :D_MODEL]).astype(BF16)
    rnn_inputs(1)
    gr_ref[...] = _sigmoid_of_half(proj(C_MG + D_MODEL, C_END) + bm_ref[:, D_MODEL:])
    rnn_inputs(2)
    zr = proj(C_GR, C_MG)
    szr_ref[...] = _silu(zr)
    rnn_inputs(3)

    row = lax.broadcasted_iota(jnp.int32, (SUBLANES, RNN_WIDTH), 0)

    hprev = hc_ref[...]
    for ci in range(TM // SUBLANES):
        rows = slice(ci * SUBLANES, (ci + 1) * SUBLANES)
        av = a_ref[rows, :]
        bv = b_ref[rows, :]
        for d in (1, 2, 4):
            a_sh = jnp.where(row >= d, pltpu.roll(av, d, 0), 1.0)
            b_sh = jnp.where(row >= d, pltpu.roll(bv, d, 0), 0.0)
            bv = av * b_sh + bv
            av = av * a_sh
        hv = av * hprev + bv
        h_ref[rows, :] = hv
        hprev = jnp.broadcast_to(hv[SUBLANES - 1:SUBLANES, :], (SUBLANES, RNN_WIDTH))
    hc_ref[...] = hprev

    attn_queries()
    attn_keys_and_indexer()
    attn_gate()

    u = (h_ref[...] * szr_ref[...]).astype(BF16)
    y_rnn = jnp.dot(u, wrp_ref[...], preferred_element_type=F32)
    r_ref[0] = (gr_ref[...] * y_rnn).astype(BF16)


def _attn_kernel(topk, *refs):
    first_row = pl.program_id(0) == 0
    step = pl.program_id(1)

    def one_block(blk, _):
        _attn_block(topk, step * QB + blk, blk, first_row, *refs)
        return 0

    lax.fori_loop(0, QB, one_block, 0)


def _attn_block(topk, i, blk, first_row,
                lq_ref, c_ref, ct_ref, qit_ref, ki_ref, wt_ref, sg_ref, wuv_ref,
                ga_ref, r_ref, x_ref, wap_ref, wout_ref, fg_ref,
                out_ref,
                lhst_ref, kext_ref, vt_ref, pos_ref, sc_ref, s_ref, acc_ref, ubuf_ref):
    seq = kext_ref.shape[0]
    n_kt = (i * TQ + TQ + TK - 1) // TK
    n_cc = (i * TQ + TQ + CC - 1) // CC

    @pl.when(jnp.logical_and(first_row, i == 0))
    def _():
        er = lax.broadcasted_iota(jnp.int32, (TQ, TQ), 0)
        ec = lax.broadcasted_iota(jnp.int32, (TQ, TQ), 1)
        eye = jnp.where(er == ec, 1.0, 0.0).astype(BF16)
        for h in range(N_HEADS):
            lhst_ref[KV_LATENT:KV_LATENT + TQ, h * TQ:(h + 1) * TQ] = eye
        pos_ref[...] = lax.broadcasted_iota(jnp.int32, (seq, LANES), 0).astype(F32)
        rr = lax.broadcasted_iota(jnp.int32, (VROWS - KV_LATENT, TK), 0)
        for k in range(seq // TK):
            vt_ref[k, KV_LATENT:VROWS, :] = jnp.where(rr == 0, 1.0, 0.0).astype(BF16)

    @pl.when(i == 0)
    def _():
        kext_ref[:, 0:KV_LATENT] = c_ref[0]
        for k in range(seq // TK):
            vt_ref[k, 0:KV_LATENT, :] = ct_ref[0, k]

    lhst_ref[0:KV_LATENT, :] = lq_ref[0, blk]

    qit = qit_ref[0, blk]
    wv = wt_ref[0, blk]
    tpos = i * TQ + lax.broadcasted_iota(jnp.int32, (1, TQ), 1)

    def fold8(v, op):
        parts = [op(v[k * KC:(k + 1) * KC].reshape(KC // SUBLANES, SUBLANES, v.shape[1]), axis=0)
                 for k in range(v.shape[0] // KC)]
        return op(jnp.stack(parts), axis=0)

    def walk(block_fn, carry):
        n4 = n_kt // 4
        carry = lax.fori_loop(
            0, n4, lambda k, cr: block_fn(pl.multiple_of(k * 4 * TK, 4 * TK), 4 * TK, k * 4, cr),
            carry)
        kt2 = n4 * 4
        has2 = n_kt % 4 >= 2
        carry = lax.cond(
            has2, lambda cr: block_fn(pl.multiple_of(kt2 * TK, 2 * TK), 2 * TK, kt2, cr),
            lambda cr: cr, carry)
        kt1 = kt2 + jnp.where(has2, 2, 0)
        return lax.cond(
            n_kt % 2 == 1, lambda cr: block_fn(pl.multiple_of(kt1 * TK, TK), TK, kt1, cr),
            lambda cr: cr, carry)

    def score_block(r0, rows, kt0, carry):
        lo8, hi8, pos8, neg8 = carry
        prod = jnp.dot(ki_ref[0, pl.ds(r0, rows), :], qit, preferred_element_type=F32)
        sc = jnp.maximum(prod[:, 0:TQ], 0.0) * wv[0:1, :]
        for jh in range(1, IDX_HEADS):
            sc = sc + jnp.maximum(prod[:, jh * TQ:(jh + 1) * TQ], 0.0) * wv[jh:jh + 1, :]
        causal = (lax.broadcasted_iota(jnp.int32, (rows, TQ), 0) + r0) <= tpos
        sc_ref[pl.ds(r0, rows), :] = jnp.where(causal, sc, -jnp.inf)
        lo8 = jnp.minimum(lo8, fold8(jnp.where(causal, sc, jnp.inf), jnp.min))
        hi8 = jnp.maximum(hi8, fold8(jnp.where(causal, sc, -jnp.inf), jnp.max))
        scz = jnp.where(causal, sc, 0.0)
        pos8 = jnp.minimum(pos8, fold8(jnp.where(scz > 0.0, scz, jnp.inf), jnp.min))
        neg8 = jnp.maximum(neg8, fold8(jnp.where(scz < 0.0, scz, -jnp.inf), jnp.max))
        return lo8, hi8, pos8, neg8

    inf8 = jnp.full((SUBLANES, TQ), jnp.inf, F32)
    lo8, hi8, pos8, neg8 = walk(score_block, (inf8, -inf8, inf8, -inf8))

    @pl.when(n_kt * TK < n_cc * CC)
    def _():
        r0 = pl.multiple_of(n_kt * TK, TK)
        sc_ref[pl.ds(r0, TK), :] = jnp.full((TK, TQ), -jnp.inf, F32)

    lo_k = _f32_to_key(jnp.min(lo8, axis=0, keepdims=True))
    vmax = jnp.max(hi8, axis=0, keepdims=True)
    hi_k = jnp.where(jnp.abs(vmax) < F32_MIN_NORMAL, _f32_to_key(jnp.full_like(vmax, F32_MIN_NORMAL)),
                     _f32_to_key(vmax) + 1)
    kf = float(topk)

    def count_ge(thr):
        def body(cc, c8):
            r0 = pl.multiple_of(cc * CC, CC)
            xs = sc_ref[pl.ds(r0, CC), :]
            return c8 + fold8(jnp.where(xs >= thr, 1.0, 0.0), jnp.sum)
        c8 = lax.fori_loop(0, n_cc, body, jnp.zeros((SUBLANES, TQ), F32))
        return jnp.sum(c8, axis=0, keepdims=True)

    def midpoint(lo_, hi_):
        return (lo_ >> 1) + (hi_ >> 1) + (lo_ & hi_ & 1)

    def probe(carry, thr_k):
        lo_, hi_, clo_, done_ = carry
        cnt = count_ge(_key_to_f32(thr_k))
        ge = cnt >= kf
        up = jnp.logical_and(ge, thr_k > lo_)
        down = jnp.logical_and(jnp.logical_not(ge), thr_k < hi_)
        return (jnp.where(up, thr_k, lo_), jnp.where(down, thr_k, hi_),
                jnp.where(up, cnt, clo_), done_), ge

    def bisect(carry):
        return probe(carry, midpoint(carry[0], carry[1]))[0]

    def unresolved(carry):
        lo_, hi_, clo_, done_ = carry
        open_ = jnp.where(clo_ > kf, jnp.where(midpoint(lo_, hi_) != lo_, 1.0 - done_, 0.0), 0.0)
        return (jnp.max(open_) > 0.5).astype(jnp.int32)

    n_causal = (tpos[0:1, :] + 1).astype(F32)
    searching = (i + 1) * TQ > topk
    carry = (lo_k, hi_k, n_causal, jnp.zeros((1, TQ), F32))

    def zero_steps(cr):
        zero_k = jnp.zeros((1, TQ), jnp.int32)
        cr, ge0 = probe(cr, zero_k)
        pmin = jnp.min(pos8, axis=0, keepdims=True)
        nmax = jnp.max(neg8, axis=0, keepdims=True)
        near = jnp.where(ge0, pmin, nmax)
        has_near = jnp.abs(near) < jnp.inf
        near_k = jnp.where(has_near, _f32_to_key(near), zero_k)
        cr, ge1 = probe(cr, near_k)
        settled = jnp.logical_and(has_near, ge0 != ge1)
        return cr[0], cr[1], cr[2], jnp.where(settled, 1.0, 0.0)

    carry = lax.cond(searching, zero_steps, lambda cr: cr, carry)
    n_fixed = jnp.where(searching, BISECT_FIXED, 0)
    carry = lax.fori_loop(0, n_fixed, lambda _, cr: bisect(cr), carry)

    def search_more(state):
        it, _, cr = state
        for _ in range(BISECT_PER_CHECK):
            cr = bisect(cr)
        return it + 1, unresolved(cr), cr

    _, _, (lo_k, hi_k, c_lo, _) = lax.while_loop(
        lambda st: jnp.logical_and(st[1] > 0, st[0] < BISECT_MAX_CHECKS),
        search_more, (jnp.int32(0), unresolved(carry), carry))
    lo = _key_to_f32(lo_k)
    hi = _key_to_f32(hi_k)

    has_ties = jnp.max(jnp.where(c_lo > kf, 1.0, 0.0)) > 0.5

    @pl.when(jnp.logical_not(has_ties))
    def _():
        def mask_chunk(kt, _):
            r0 = pl.multiple_of(kt * TK, TK)
            xs = sc_ref[pl.ds(r0, TK), :]
            kext_ref[pl.ds(r0, TK), KV_LATENT:KV_LATENT + TQ] = (
                jnp.where(xs >= lo, 0.0, MASK_NEG).astype(BF16))
            return 0
        lax.fori_loop(0, n_kt, mask_chunk, 0)

    @pl.when(has_ties)
    def _():
        need = kf - count_ge(hi)
        tri = jnp.where(lax.broadcasted_iota(jnp.int32, (TK, TK), 1)
                        <= lax.broadcasted_iota(jnp.int32, (TK, TK), 0), 1.0, 0.0).astype(BF16)

        def mask_chunk(kt, before):
            r0 = pl.multiple_of(kt * TK, TK)
            xs = sc_ref[pl.ds(r0, TK), :]
            band = jnp.where(xs >= lo, jnp.where(xs < hi, 1.0, 0.0), 0.0)
            incl = jnp.dot(tri, band.astype(BF16), preferred_element_type=F32)
            rank = incl - band + before
            keep = jnp.where(xs >= hi, 1.0, jnp.where(rank < need, band, 0.0))
            kext_ref[pl.ds(r0, TK), KV_LATENT:KV_LATENT + TQ] = (
                jnp.where(keep > 0.0, 0.0, MASK_NEG).astype(BF16))
            return before + incl[TK - 1:TK, :]

        lax.fori_loop(0, n_kt, mask_chunk, jnp.zeros((1, TQ), F32))


    def qk_block(r0, rows, kt0, mx):
        st = jnp.dot(kext_ref[pl.ds(r0, rows), :], lhst_ref[...], preferred_element_type=F32)
        posb = pos_ref[pl.ds(r0, rows), :]
        tops = []
        for h in range(N_HEADS):
            sh = st[:, h * TQ:(h + 1) * TQ] + posb * (ALIBI_SLOPES[h] * LOG2E)
            s_ref[pl.ds(r0, rows), h * TQ:(h + 1) * TQ] = sh
            tops.append(fold8(sh, jnp.max))
        return jnp.maximum(mx, jnp.concatenate(tops, axis=1))

    mx = walk(qk_block, jnp.full((SUBLANES, N_HEADS * TQ), M_INIT, F32))
    m = jnp.max(mx, axis=0, keepdims=True)
    acc_ref[...] = jnp.zeros(acc_ref.shape, F32)

    def pv_block(r0, rows, kt0, _):
        pt = jnp.exp2(s_ref[pl.ds(r0, rows), :] - m).astype(BF16)
        vt = jnp.concatenate([vt_ref[kt0 + k] for k in range(rows // TK)], axis=1)
        acc_ref[...] += jnp.dot(vt, pt, preferred_element_type=F32)
        return 0

    walk(pv_block, 0)

    o_parts = []
    for p in range(N_HEADS // 2):
        cols = slice(p * 2 * TQ, (p + 1) * 2 * TQ)
        xt = (acc_ref[0:KV_LATENT, cols] / acc_ref[KV_LATENT:KV_LATENT + 1, cols]).T
        pair = jnp.concatenate([xt[:TQ], xt[TQ:]], axis=1).astype(BF16)
        o_parts.append(jnp.dot(pair, wuv_ref[p], preferred_element_type=F32))
    o = jnp.concatenate(o_parts, axis=1)
    slot = i % (TM // TQ)
    sg = sg_ref[0, pl.ds(pl.multiple_of(blk * TQ, TQ), TQ), :]
    ubuf_ref[pl.ds(pl.multiple_of(slot * TQ, TQ), TQ), :] = (o * sg.astype(F32)).astype(BF16)

    @pl.when(slot == TM // TQ - 1)
    def _():
        y_attn = jnp.dot(ubuf_ref[...], wap_ref[...], preferred_element_type=F32)
        mixed = (ga_ref[0].astype(F32) * y_attn + r_ref[0].astype(F32)).astype(BF16)
        res = x_ref[0] + jnp.dot(mixed, wout_ref[...], preferred_element_type=F32)
        out_ref[0] = (res * lax.rsqrt(jnp.mean(res * res, axis=-1, keepdims=True) + NORM_EPS)
                      * fg_ref[...])


def _const_spec(shape):
    nd = len(shape)
    return pl.BlockSpec(shape, lambda b, i: (0,) * nd, pipeline_mode=pl.Buffered(1))


def _pack_w_in(w_in):
    n_head = C_KW + IDX_DIM + IDX_HEADS
    pad = jnp.zeros((D_MODEL, LANES - IDX_DIM - IDX_HEADS), w_in.dtype)
    packed = jnp.concatenate([w_in[:, :n_head], pad, w_in[:, n_head:]], axis=1)
    col = lax.broadcasted_iota(jnp.int32, (1, C_END), 1)
    return (packed * jnp.where(col >= C_MG, 0.5, 1.0)).astype(BF16)


def _pair_block_diag(w):
    n2, r, c = w.shape
    w = w.reshape(n2 // 2, 2, r, c)
    z = jnp.zeros((n2 // 2, r, c), w.dtype)
    top = jnp.concatenate([w[:, 0], z], axis=2)
    bot = jnp.concatenate([z, w[:, 1]], axis=2)
    return jnp.concatenate([top, bot], axis=1)


def _layer(x, norm_g, w_in, b_merge, kv_g, w_uk, w_uv, ln_g, ln_b, w_ap, conv_w, conv_b,
           w_a, b_a, w_x, b_x, lam, w_rp, w_out, out_g):
    B, S, _ = x.shape
    assert S % TM == 0 and TM % TK == 0 and TK % TQ == 0 and TQ == KC
    assert S % CC == 0 and CC % TK == 0 and S % (4 * TK) == 0 and TM % (QB * TQ) == 0
    topk = min(TOPK_MAX, S // 4)
    row = lambda v: v.reshape(1, -1).astype(F32)
    lane_pad = lambda v: jnp.pad(v.astype(F32), (0, LANES - v.shape[0])).reshape(1, LANES)

    w_all = _pack_w_in(w_in)
    wg = (0.5 * jnp.concatenate([_pair_block_diag(w_a), _pair_block_diag(w_x)], axis=2)).astype(BF16)
    wuk2t = _pair_block_diag(jnp.swapaxes(w_uk, 1, 2)).astype(BF16)

    nq = S // TQ
    tile = lambda n: pl.BlockSpec((1, TM, n), lambda b, j: (b, j, 0))
    outs = pl.pallas_call(
        _proj_kernel,
        grid=(B, S // TM),
        in_specs=[
            tile(D_MODEL),
            _const_spec((1, D_MODEL)),
            _const_spec((D_MODEL, C_END)),
            _const_spec((N_HEADS // 2, 2 * HEAD_DIM, 2 * KV_LATENT)),
            _const_spec((1, KV_LATENT)),
            _const_spec((1, LANES)),
            _const_spec((1, LANES)),
            _const_spec((CONV_WIDTH, RNN_WIDTH)),
            _const_spec((1, RNN_WIDTH)),
            _const_spec((RNN_WIDTH // LANES, LANES, 2 * LANES)),
            _const_spec((1, RNN_WIDTH)),
            _const_spec((1, RNN_WIDTH)),
            _const_spec((1, RNN_WIDTH)),
            _const_spec((RNN_WIDTH, D_MODEL)),
            _const_spec((1, 2 * D_MODEL)),
        ],
        out_specs=[
            pl.BlockSpec((1, TM // TQ, KV_LATENT, N_HEADS * TQ), lambda b, j: (b, j, 0, 0)),
            tile(KV_LATENT),
            pl.BlockSpec((1, TM // TK, KV_LATENT, TK), lambda b, j: (b, j, 0, 0)),
            pl.BlockSpec((1, TM // TQ, IDX_DIM, IDX_HEADS * TQ), lambda b, j: (b, j, 0, 0)),
            tile(IDX_DIM),
            pl.BlockSpec((1, TM // TQ, IDX_HEADS, TQ), lambda b, j: (b, j, 0, 0)),
            tile(ATTN_WIDTH),
            tile(D_MODEL),
            tile(D_MODEL),
        ],
        out_shape=[
            jax.ShapeDtypeStruct((B, nq, KV_LATENT, N_HEADS * TQ), BF16),
            jax.ShapeDtypeStruct((B, S, KV_LATENT), BF16),
            jax.ShapeDtypeStruct((B, S // TK, KV_LATENT, TK), BF16),
            jax.ShapeDtypeStruct((B, nq, IDX_DIM, IDX_HEADS * TQ), BF16),
            jax.ShapeDtypeStruct((B, S, IDX_DIM), BF16),
            jax.ShapeDtypeStruct((B, nq, IDX_HEADS, TQ), F32),
            jax.ShapeDtypeStruct((B, S, ATTN_WIDTH), BF16),
            jax.ShapeDtypeStruct((B, S, D_MODEL), BF16),
            jax.ShapeDtypeStruct((B, S, D_MODEL), BF16),
        ],
        scratch_shapes=[
            pltpu.VMEM((TM + SUBLANES, RNN_WIDTH), F32),
            pltpu.VMEM((TM, RNN_WIDTH), F32),
            pltpu.VMEM((TM, RNN_WIDTH), F32),
            pltpu.VMEM((TM, RNN_WIDTH), F32),
            pltpu.VMEM((SUBLANES, RNN_WIDTH), F32),
            pltpu.VMEM((TM, RNN_WIDTH), F32),
            pltpu.VMEM((TM, D_MODEL), F32),
        ],
        compiler_params=pltpu.CompilerParams(
            dimension_semantics=("arbitrary", "arbitrary"),
            vmem_limit_bytes=VMEM_LIMIT_BYTES),
        name="proj_rglru",
    )(x, row(norm_g), w_all, wuk2t, row(kv_g), lane_pad(ln_g), lane_pad(ln_b), conv_w.astype(F32),
      row(conv_b), wg, row(0.5 * b_a), row(0.5 * b_x), row(lam), w_rp.astype(BF16),
      row(0.5 * b_merge))
    lq, c, ct, qit, ki, wt, sg, ga, r = outs

    wuv2 = _pair_block_diag(w_uv).astype(BF16)

    steps_per_out = TM // (QB * TQ)
    otile = lambda n: pl.BlockSpec((1, TM, n), lambda b, p: (b, p // steps_per_out, 0))
    return pl.pallas_call(
        functools.partial(_attn_kernel, topk),
        grid=(B, nq // QB),
        in_specs=[
            pl.BlockSpec((1, QB, KV_LATENT, N_HEADS * TQ), lambda b, p: (b, p, 0, 0)),
            pl.BlockSpec((1, S, KV_LATENT), lambda b, p: (b, 0, 0)),
            pl.BlockSpec((1, S // TK, KV_LATENT, TK), lambda b, p: (b, 0, 0, 0)),
            pl.BlockSpec((1, QB, IDX_DIM, IDX_HEADS * TQ), lambda b, p: (b, p, 0, 0)),
            pl.BlockSpec((1, S, IDX_DIM), lambda b, p: (b, 0, 0)),
            pl.BlockSpec((1, QB, IDX_HEADS, TQ), lambda b, p: (b, p, 0, 0)),
            pl.BlockSpec((1, QB * TQ, ATTN_WIDTH), lambda b, p: (b, p, 0)),
            _const_spec((N_HEADS // 2, 2 * KV_LATENT, 2 * HEAD_DIM)),
            otile(D_MODEL),
            otile(D_MODEL),
            otile(D_MODEL),
            _const_spec((ATTN_WIDTH, D_MODEL)),
            _const_spec((D_MODEL, D_MODEL)),
            _const_spec((1, D_MODEL)),
        ],
        out_specs=otile(D_MODEL),
        out_shape=jax.ShapeDtypeStruct((B, S, D_MODEL), F32),
        scratch_shapes=[
            pltpu.VMEM((2 * KV_LATENT, N_HEADS * TQ), BF16),
            pltpu.VMEM((S, 2 * KV_LATENT), BF16),
            pltpu.VMEM((S // TK, VROWS, TK), BF16),
            pltpu.VMEM((S, LANES), F32),
            pltpu.VMEM((S, TQ), F32),
            pltpu.VMEM((S, N_HEADS * TQ), F32),
            pltpu.VMEM((VROWS, N_HEADS * TQ), F32),
            pltpu.VMEM((TM, ATTN_WIDTH), BF16),
        ],
        compiler_params=pltpu.CompilerParams(
            dimension_semantics=("arbitrary", "arbitrary"),
            vmem_limit_bytes=VMEM_LIMIT_BYTES),
        name="sparse_attn_out",
    )(lq, c, ct, qit, ki, wt, sg, wuv2, ga, r, x, w_ap.astype(BF16), w_out.astype(BF16),
      row(out_g))


def kernel(x, norm_gain, w_in, b_merge, kv_norm_gain, w_uk, w_uv, idx_ln_gain, idx_ln_bias,
           w_attn_proj, conv_w, conv_b, w_rg_a, b_rg_a, w_rg_x, b_rg_x, lru_lambda,
           w_rnn_proj, w_out, final_norm_gain):
    assert norm_gain.shape[0] == 1, "only the stated depth-1 stack is supported"
    return _layer(x, norm_gain[0], w_in[0], b_merge[0], kv_norm_gain[0], w_uk[0], w_uv[0],
                  idx_ln_gain[0], idx_ln_bias[0], w_attn_proj[0], conv_w[0], conv_b[0],
                  w_rg_a[0], b_rg_a[0], w_rg_x[0], b_rg_x[0], lru_lambda[0],
                  w_rnn_proj[0], w_out[0], final_norm_gain)
```
